```python
import math
import jax, jax.numpy as jnp
from jax import lax
import numpy as np

D_MODEL = 2048
BATCH = 4
SEQ = 4096
DEPTH = 2

N_MIXERS = 2
NORM_EPS = 1e-6

ATT_HEADS = 16
ATT_HEAD_DIM = 128
ATT_WIDTH = ATT_HEADS * ATT_HEAD_DIM
IDX_HEADS = 16
IDX_HEAD_DIM = 64
TOPK_MAX = 256
Q_BLOCK = 128
A_IN = 4 * ATT_WIDTH + IDX_HEADS * IDX_HEAD_DIM + IDX_HEAD_DIM + IDX_HEADS

REL_BUCKETS = 32
REL_MAX_DIST = 128

SSM_EXPAND = 2
SSM_INNER = SSM_EXPAND * D_MODEL
SSM_HEAD_DIM = 64
SSM_HEADS = SSM_INNER // SSM_HEAD_DIM
SSM_GROUPS = 8
SSM_STATE = 128
SSM_CONV = 4
SSM_CHUNK = 128
SSM_CONV_CH = SSM_INNER + 2 * SSM_GROUPS * SSM_STATE
B_IN = SSM_INNER + SSM_CONV_CH + SSM_HEADS

N_A_LAYERS = (DEPTH + N_MIXERS - 1) // N_MIXERS
N_B_LAYERS = DEPTH // N_MIXERS

kernel_name = "hybrid_dsa_ssd_interleaved"


def rmsnorm(x, w):
    xf = x.astype(jnp.float32)
    y = xf * lax.rsqrt(jnp.mean(xf * xf, axis=-1, keepdims=True) + NORM_EPS)
    return (y * w.astype(jnp.float32)).astype(x.dtype)


def rel_bucket(n):
    n = jnp.maximum(n, 0)
    max_exact = REL_BUCKETS // 2
    nf = jnp.maximum(n, 1).astype(jnp.float32)
    large = max_exact + (jnp.log(nf / max_exact) / math.log(REL_MAX_DIST / max_exact)
                         * (REL_BUCKETS - max_exact)).astype(jnp.int32)
    large = jnp.minimum(large, REL_BUCKETS - 1)
    return jnp.where(n < max_exact, n, large)


def dsa_mixer(h, w_in, w_out, rel_bias):
    b, s, _ = h.shape
    proj = h @ w_in
    sizes = [ATT_WIDTH, ATT_WIDTH, ATT_WIDTH, ATT_WIDTH, IDX_HEADS * IDX_HEAD_DIM, IDX_HEAD_DIM, IDX_HEADS]
    offs = list(np.cumsum(sizes)[:-1])
    q, k, v, g, iq, ik, iw = jnp.split(proj, offs, axis=-1)
    q = q.reshape(b, s, ATT_HEADS, ATT_HEAD_DIM)
    k = k.reshape(b, s, ATT_HEADS, ATT_HEAD_DIM)
    v = v.reshape(b, s, ATT_HEADS, ATT_HEAD_DIM)
    iq = iq.reshape(b, s, IDX_HEADS, IDX_HEAD_DIM)
    iw = iw * (IDX_HEADS ** -0.5 * IDX_HEAD_DIM ** -0.5)
    topk = min(TOPK_MAX, s // 4)
    nblk = s // Q_BLOCK
    pos = jnp.arange(s, dtype=jnp.int32)
    scale = ATT_HEAD_DIM ** -0.5

    def blockify(a):
        return jnp.moveaxis(a.reshape(b, nblk, Q_BLOCK, *a.shape[2:]), 1, 0)

    def attend_block(args):
        qb, iqb, iwb, tb = args
        dots = jnp.einsum('bqhd,bsd->bqsh', iqb, ik)
        score = jnp.einsum('bqsh,bqh->bqs', jax.nn.relu(dots), iwb).astype(jnp.float32)
        causal = pos[None, :] <= tb[:, None]
        score = jnp.where(causal[None], score, -jnp.inf)
        _, idx = lax.top_k(score, topk)
        valid = idx <= tb[None, :, None]
        k_sel = jax.vmap(lambda kb, ib: kb[ib])(k, idx)
        v_sel = jax.vmap(lambda vb, ib: vb[ib])(v, idx)
        bias = rel_bias[rel_bucket(tb[None, :, None] - idx)]
        logits = (jnp.einsum('bqhd,bqkhd->bqhk', qb, k_sel).astype(jnp.float32) * scale
                  + jnp.swapaxes(bias, -1, -2).astype(jnp.float32))
        logits = jnp.where(valid[:, :, None, :], logits, -jnp.inf)
        p = jax.nn.softmax(logits, axis=-1).astype(v.dtype)
        return jnp.einsum('bqhk,bqkhd->bqhd', p, v_sel)

    out = lax.map(attend_block, (blockify(q), blockify(iq), blockify(iw), pos.reshape(nblk, Q_BLOCK)))
    out = jnp.moveaxis(out, 0, 1).reshape(b, s, ATT_WIDTH)
    return (out * jax.nn.silu(g)) @ w_out


def segsum(a):
    t = a.shape[-1]
    cs = jnp.cumsum(a, axis=-1)
    diff = cs[..., :, None] - cs[..., None, :]
    mask = jnp.tril(jnp.ones((t, t), dtype=bool))
    return jnp.where(mask, diff, -jnp.inf)


def ssd_scan(xdt, adt, bm, cm):
    b, s, nh, p = xdt.shape
    ng, n = bm.shape[-2:]
    r = nh // ng
    c, l = s // SSM_CHUNK, SSM_CHUNK
    x_ = xdt.reshape(b, c, l, ng, r, p)
    a_ = adt.reshape(b, c, l, ng, r).transpose(0, 3, 4, 1, 2)
    b_ = bm.reshape(b, c, l, ng, n).astype(jnp.float32)
    c_ = cm.reshape(b, c, l, ng, n).astype(jnp.float32)
    a_cs = jnp.cumsum(a_, axis=-1)
    lmat = jnp.exp(segsum(a_))
    cb = jnp.einsum('bclgn,bcsgn->bgcls', c_, b_)
    y_diag = jnp.einsum('bgcls,bgrcls,bcsgrp->bclgrp', cb, lmat, x_)
    decay_states = jnp.exp(a_cs[..., -1:] - a_cs)
    states = jnp.einsum('bclgn,bgrcl,bclgrp->bcgrpn', b_, decay_states, x_)
    chunk_decay = jnp.exp(a_cs[..., -1])

    def step(carry, inp):
        st, dec = inp
        return carry * dec[..., None, None] + st, carry

    init = jnp.zeros((b, ng, r, p, n), jnp.float32)
    _, prev = lax.scan(step, init, (jnp.moveaxis(states, 1, 0), jnp.moveaxis(chunk_decay, 3, 0)))
    prev = jnp.moveaxis(prev, 0, 1)
    y_off = jnp.einsum('bclgn,bcgrpn,bgrcl->bclgrp', c_, prev, jnp.exp(a_cs))
    return (y_diag + y_off).reshape(b, s, nh, p)


def ssd_mixer(h, w_in, conv_w, conv_b, dt_bias, a_log, d_skip, norm_w, w_out):
    b, s, _ = h.shape
    proj = h @ w_in
    z = proj[..., :SSM_INNER]
    xbc = proj[..., SSM_INNER:SSM_INNER + SSM_CONV_CH]
    dt = proj[..., SSM_INNER + SSM_CONV_CH:]
    xbc = lax.conv_general_dilated(xbc, conv_w[:, None, :].astype(xbc.dtype), window_strides=(1,),
                                   padding=[(SSM_CONV - 1, 0)], dimension_numbers=('NWC', 'WIO', 'NWC'),
                                   feature_group_count=SSM_CONV_CH) + conv_b
    xbc = jax.nn.silu(xbc)
    xs = xbc[..., :SSM_INNER].reshape(b, s, SSM_HEADS, SSM_HEAD_DIM)
    gn = SSM_GROUPS * SSM_STATE
    bm = xbc[..., SSM_INNER:SSM_INNER + gn].reshape(b, s, SSM_GROUPS, SSM_STATE)
    cm = xbc[..., SSM_INNER + gn:].reshape(b, s, SSM_GROUPS, SSM_STATE)
    dt = jax.nn.softplus((dt + dt_bias).astype(jnp.float32))
    a = -jnp.exp(a_log.astype(jnp.float32))
    xf = xs.astype(jnp.float32)
    y = ssd_scan(xf * dt[..., None], dt * a, bm, cm)
    y = y + xf * d_skip.astype(jnp.float32)[:, None]
    y = y.reshape(b, s, SSM_INNER).astype(h.dtype)
    y = rmsnorm(y * jax.nn.silu(z), norm_w)
    return y @ w_out


def setup_inputs(seed: int = 0) -> dict:
    key = jax.random.key(seed)
    ks = jax.random.split(key, 16)
    f32 = jnp.float32
    x = jax.random.normal(ks[0], (BATCH, SEQ, D_MODEL), f32)
    norm_w = 1.0 + 0.02 * jax.random.normal(ks[1], (DEPTH, D_MODEL), f32)
    a_w_in = jax.random.normal(ks[2], (N_A_LAYERS, D_MODEL, A_IN), f32) * D_MODEL ** -0.5
    a_w_out = jax.random.normal(ks[3], (N_A_LAYERS, ATT_WIDTH, D_MODEL), f32) * ATT_WIDTH ** -0.5
    rel_bias = 0.5 * jax.random.normal(ks[4], (REL_BUCKETS, ATT_HEADS), f32)
    b_w_in = jax.random.normal(ks[5], (N_B_LAYERS, D_MODEL, B_IN), f32) * D_MODEL ** -0.5
    b_conv_w = jax.random.normal(ks[6], (N_B_LAYERS, SSM_CONV, SSM_CONV_CH), f32) * SSM_CONV ** -0.5
    b_conv_b = 0.01 * jax.random.normal(ks[7], (N_B_LAYERS, SSM_CONV_CH), f32)
    dt0 = jnp.exp(jax.random.uniform(ks[8], (N_B_LAYERS, SSM_HEADS), f32,
                                     math.log(1e-3), math.log(1e-1)))
    b_dt_bias = dt0 + jnp.log(-jnp.expm1(-dt0))
    b_a_log = jnp.log(jax.random.uniform(ks[9], (N_B_LAYERS, SSM_HEADS), f32, 1.0, 16.0))
    b_d = 1.0 + 0.1 * jax.random.normal(ks[10], (N_B_LAYERS, SSM_HEADS), f32)
    b_norm_w = 1.0 + 0.02 * jax.random.normal(ks[11], (N_B_LAYERS, SSM_INNER), f32)
    b_w_out = jax.random.normal(ks[12], (N_B_LAYERS, SSM_INNER, D_MODEL), f32) * SSM_INNER ** -0.5
    final_norm_w = 1.0 + 0.02 * jax.random.normal(ks[13], (D_MODEL,), f32)
    return {"x": x, "norm_w": norm_w, "a_w_in": a_w_in, "a_w_out": a_w_out, "rel_bias": rel_bias,
            "b_w_in": b_w_in, "b_conv_w": b_conv_w, "b_conv_b": b_conv_b, "b_dt_bias": b_dt_bias,
            "b_a_log": b_a_log, "b_d": b_d, "b_norm_w": b_norm_w, "b_w_out": b_w_out,
            "final_norm_w": final_norm_w}


def reference(x, norm_w, a_w_in, a_w_out, rel_bias, b_w_in, b_conv_w, b_conv_b, b_dt_bias,
              b_a_log, b_d, b_norm_w, b_w_out, final_norm_w):
    for i in range(DEPTH):
        h = rmsnorm(x, norm_w[i])
        j = i // N_MIXERS
        if i % N_MIXERS == 0:
            x = x + dsa_mixer(h, a_w_in[j], a_w_out[j], rel_bias)
        else:
            x = x + ssd_mixer(h, b_w_in[j], b_conv_w[j], b_conv_b[j], b_dt_bias[j], b_a_log[j],
                              b_d[j], b_norm_w[j], b_w_out[j])
    return rmsnorm(x, final_norm_w)
```

```python
import functools
import math

import jax
import jax.numpy as jnp
from jax import lax
from jax.experimental import pallas as pl
from jax.experimental.pallas import tpu as pltpu

NORM_EPS = 1e-6

ATT_HEAD_DIM = 128
IDX_HEADS = 16
IDX_HEAD_DIM = 64
TOPK_MAX = 256
REL_BUCKETS = 32
REL_MAX_DIST = 128

SSM_HEAD_DIM = 64
SSM_STATE = 128
SSM_CONV = 4
SSM_CHUNK = 128

LANES = 128
SUBLANES = 8
VMEM_LIMIT_BYTES = 56 * 1024 * 1024

INT_MIN = -2 ** 31
NEG_BIG = -1e30
ATT_Q_BLOCK = 512

_NT = (((1,), (1,)), ((), ()))


def _cparams(sem):
    return pltpu.CompilerParams(dimension_semantics=sem, vmem_limit_bytes=VMEM_LIMIT_BYTES)


def _silu(x):
    return x * (0.5 * jnp.tanh(0.5 * x) + 0.5)


def _split3(x):
    hi = x.astype(jnp.bfloat16)
    r1 = x - hi.astype(jnp.float32)
    mid = r1.astype(jnp.bfloat16)
    lo = (r1 - mid.astype(jnp.float32)).astype(jnp.bfloat16)
    return hi, mid, lo


def _dot_exact_lhs(a01, x):
    hi, mid, lo = _split3(x)
    f = functools.partial(jnp.dot, preferred_element_type=jnp.float32)
    return f(a01, hi) + f(a01, mid) + f(a01, lo)


def _dot_exact_rhs(x, b01):
    hi, mid, lo = _split3(x)
    f = functools.partial(jnp.dot, preferred_element_type=jnp.float32)
    return f(hi, b01) + f(mid, b01) + f(lo, b01)


def _norm_matmul_kernel(x_ref, nw_ref, w_ref, o_ref, xn_ref):
    @pl.when(pl.program_id(1) == 0)
    def _():
        x = x_ref[...]
        ms = jnp.mean(x * x, axis=-1, keepdims=True)
        xn_ref[...] = (x * lax.rsqrt(ms + NORM_EPS) * nw_ref[...]).astype(xn_ref.dtype)

    o_ref[...] = jnp.dot(xn_ref[...], w_ref[...],
                         preferred_element_type=jnp.float32).astype(o_ref.dtype)


def _norm_matmul(x, nw, w, out_dtype, tm, tn):
    t, d = x.shape
    n = w.shape[1]
    return pl.pallas_call(
        _norm_matmul_kernel,
        grid=(t // tm, n // tn),
        in_specs=[pl.BlockSpec((tm, d), lambda i, j: (i, 0)),
                  pl.BlockSpec((1, d), lambda i, j: (0, 0)),
                  pl.BlockSpec((d, tn), lambda i, j: (0, j))],
        out_specs=pl.BlockSpec((tm, tn), lambda i, j: (i, j)),
        out_shape=jax.ShapeDtypeStruct((t, n), out_dtype),
        scratch_shapes=[pltpu.VMEM((tm, d), jnp.bfloat16)],
        compiler_params=_cparams(("parallel", "arbitrary")),
    )(x, nw.reshape(1, d), w)


def _matmul_kernel(a_ref, w_ref, o_ref):
    o_ref[...] = jnp.dot(a_ref[...], w_ref[...],
                         preferred_element_type=jnp.float32).astype(o_ref.dtype)


def _matmul(a, w, out_dtype, tm, tn):
    t, d = a.shape
    n = w.shape[1]
    return pl.pallas_call(
        _matmul_kernel,
        grid=(n // tn, t // tm),
        in_specs=[pl.BlockSpec((tm, d), lambda j, i: (i, 0)),
                  pl.BlockSpec((d, tn), lambda j, i: (0, j))],
        out_specs=pl.BlockSpec((tm, tn), lambda j, i: (i, j)),
        out_shape=jax.ShapeDtypeStruct((t, n), out_dtype),
        compiler_params=_cparams(("parallel", "parallel")),
    )(a, w)


def _proj_residual_norm_kernel(a_ref, w_ref, x_ref, nw_ref, xo_ref, ho_ref):
    x1 = x_ref[...] + jnp.dot(a_ref[...], w_ref[...], preferred_element_type=jnp.float32)
    xo_ref[...] = x1
    ms = jnp.mean(x1 * x1, axis=-1, keepdims=True)
    ho_ref[...] = (x1 * lax.rsqrt(ms + NORM_EPS) * nw_ref[...]).astype(ho_ref.dtype)


def _proj_residual_norm(a, w, x, nw, h_dtype, tm):
    t, k = a.shape
    d = w.shape[1]
    return pl.pallas_call(
        _proj_residual_norm_kernel,
        grid=(t // tm,),
        in_specs=[pl.BlockSpec((tm, k), lambda i: (i, 0)),
                  pl.BlockSpec((k, d), lambda i: (0, 0)),
                  pl.BlockSpec((tm, d), lambda i: (i, 0)),
                  pl.BlockSpec((1, d), lambda i: (0, 0))],
        out_specs=[pl.BlockSpec((tm, d), lambda i: (i, 0)),
                   pl.BlockSpec((tm, d), lambda i: (i, 0))],
        out_shape=[jax.ShapeDtypeStruct((t, d), jnp.float32),
                   jax.ShapeDtypeStruct((t, d), h_dtype)],
        compiler_params=_cparams(("parallel",)),
    )(a, w, x, nw.reshape(1, d))


def _dsa_kernel(q_ref, k_ref, v_ref, g_ref, iq_ref, ikq_ref, ika_ref, bias_ref, o_ref,
                ikbd_ref, iqb_ref, key_ref, bdiag_ref, bprev_ref, m_ref, l_ref, acc_ref,
                *, tq, topk):
    qb = pl.program_id(1)
    h = pl.program_id(2)
    nck = qb + 1
    pairs = IDX_HEADS // 2
    iw_lo = IDX_HEAD_DIM
    f32 = jnp.float32

    @pl.when((h == 0) & (qb == 0))
    def _():
        blk = ika_ref[...]
        lane = lax.broadcasted_iota(jnp.int32, blk.shape, 1)
        a = jnp.where(lane < IDX_HEAD_DIM, blk, 0.0)
        ikbd_ref[0] = a.astype(jnp.bfloat16)
        ikbd_ref[1] = pltpu.roll(a, IDX_HEAD_DIM, 1).astype(jnp.bfloat16)

    @pl.when(h == 0)
    def _():
        iqb_ref[...] = iq_ref[...].astype(jnp.bfloat16)
        iw = ikq_ref[:, iw_lo:iw_lo + IDX_HEADS] * (IDX_HEADS ** -0.5 * IDX_HEAD_DIM ** -0.5)
        row = qb * tq + lax.broadcasted_iota(jnp.int32, (tq, tq), 0)
        colb = lax.broadcasted_iota(jnp.int32, (tq, tq), 1)

        def score_chunk(c, carry):
            ka = ikbd_ref[0, pl.ds(pl.multiple_of(c * tq, tq), tq), :]
            kb = ikbd_ref[1, pl.ds(pl.multiple_of(c * tq, tq), tq), :]
            sc = jnp.zeros((tq, tq), f32)
            for j in range(pairs):
                lhs = iqb_ref[:, j * LANES:(j + 1) * LANES]
                d0 = lax.dot_general(lhs, ka, _NT, preferred_element_type=f32)
                d1 = lax.dot_general(lhs, kb, _NT, preferred_element_type=f32)
                sc = sc + jnp.maximum(d0, 0.0) * iw[:, 2 * j:2 * j + 1]
                sc = sc + jnp.maximum(d1, 0.0) * iw[:, 2 * j + 1:2 * j + 2]
            bits = lax.bitcast_convert_type(sc, jnp.int32)
            key = bits ^ ((bits >> 31) & jnp.int32(0x7FFFFFFF))
            key_ref[c] = jnp.where(c * tq + colb <= row, key, jnp.int32(INT_MIN))
            return carry

        lax.fori_loop(0, nck, score_chunk, 0)

        def bit_step(i, tau_u):
            cand_u = tau_u | lax.shift_left(jnp.int32(1), 31 - i)
            cand_s = cand_u ^ jnp.int32(INT_MIN)

            def count_chunk(c, cnt):
                ge = key_ref[c] >= cand_s
                return cnt + jnp.sum(jnp.where(ge, 1.0, 0.0), axis=-1, keepdims=True)

            cnt = lax.fori_loop(0, nck, count_chunk, jnp.zeros((tq, 1), f32))
            return jnp.where(cnt >= topk, cand_u, tau_u)

        tau_u = lax.fori_loop(0, 32, bit_step, jnp.zeros((tq, 1), jnp.int32))
        tau = tau_u ^ jnp.int32(INT_MIN)

        def mask_chunk(c, carry):
            key = key_ref[c]
            sel = (key >= tau) & (key != jnp.int32(INT_MIN))
            key_ref[c] = lax.bitcast_convert_type(jnp.where(sel, 0.0, NEG_BIG).astype(f32), jnp.int32)
            return carry

        lax.fori_loop(0, nck, mask_chunk, 0)

        bdiag_ref[...] = jnp.zeros(bdiag_ref.shape, f32)
        bprev_ref[...] = jnp.zeros(bprev_ref.shape, f32)

    nb = tq // LANES
    dtile = bias_ref[0, 0]
    ptile = bias_ref[0, 1]
    for a in range(nb):
        bdiag_ref[a * LANES:(a + 1) * LANES, a * LANES:(a + 1) * LANES] = dtile
        if a >= 1:
            bdiag_ref[a * LANES:(a + 1) * LANES, (a - 1) * LANES:a * LANES] = ptile
    bprev_ref[0:LANES, (nb - 1) * LANES:nb * LANES] = ptile

    m_ref[...] = jnp.full(m_ref.shape, NEG_BIG, f32)
    l_ref[...] = jnp.zeros(l_ref.shape, f32)
    acc_ref[...] = jnp.zeros(acc_ref.shape, f32)
    q = q_ref[...]

    def att_chunk(c, bias_tile_ref):
        start = pl.multiple_of(c * tq, tq)
        kc = k_ref[pl.ds(start, tq), :]
        vc = v_ref[pl.ds(start, tq), :]
        s = lax.dot_general(q, kc, _NT, preferred_element_type=f32)
        s = s + lax.bitcast_convert_type(key_ref[c], f32)
        if bias_tile_ref is not None:
            s = s + bias_tile_ref[...]
        m_prev = m_ref[...]
        m_new = jnp.maximum(m_prev, jnp.max(s, axis=-1, keepdims=True))
        alpha = jnp.exp(m_prev - m_new)
        p = jnp.exp(s - m_new)
        l_ref[...] = alpha * l_ref[...] + jnp.sum(p, axis=-1, keepdims=True)
        acc_ref[...] = alpha * acc_ref[...] + jnp.dot(p.astype(vc.dtype), vc,
                                                      preferred_element_type=f32)
        m_ref[...] = m_new

    def far_chunk(c, carry):
        att_chunk(c, None)
        return carry

    lax.fori_loop(0, jnp.maximum(qb - 1, 0), far_chunk, 0)

    @pl.when(qb >= 1)
    def _():
        att_chunk(qb - 1, bprev_ref)

    att_chunk(qb, bdiag_ref)

    out = acc_ref[...] / l_ref[...]
    o_ref[...] = (out * _silu(g_ref[...].astype(f32))).astype(o_ref.dtype)


def _dsa_attention(qkvg, idx, bias_tiles, batch, seq, heads, tq):
    t = qkvg.shape[0]
    nq = seq // tq
    topk = min(TOPK_MAX, seq // 4)
    iq_w = IDX_HEADS * IDX_HEAD_DIM
    small_blk = iq_w // LANES
    kern = functools.partial(_dsa_kernel, tq=tq, topk=topk)
    return pl.pallas_call(
        kern,
        grid=(batch, nq, heads),
        in_specs=[
            pl.BlockSpec((tq, LANES), lambda b, i, h: (b * nq + i, h)),
            pl.BlockSpec((seq, LANES), lambda b, i, h: (b, heads + h)),
            pl.BlockSpec((seq, LANES), lambda b, i, h: (b, 2 * heads + h)),
            pl.BlockSpec((tq, LANES), lambda b, i, h: (b * nq + i, 3 * heads + h)),
            pl.BlockSpec((tq, iq_w), lambda b, i, h: (b * nq + i, 0)),
            pl.BlockSpec((tq, LANES), lambda b, i, h: (b * nq + i, small_blk)),
            pl.BlockSpec((seq, LANES), lambda b, i, h: (b, small_blk)),
            pl.BlockSpec((1, 2, LANES, LANES), lambda b, i, h: (h, 0, 0, 0)),
        ],
        out_specs=pl.BlockSpec((tq, LANES), lambda b, i, h: (b * nq + i, h)),
        out_shape=jax.ShapeDtypeStruct((t, heads * LANES), jnp.bfloat16),
        scratch_shapes=[
            pltpu.VMEM((2, seq, LANES), jnp.bfloat16),
            pltpu.VMEM((tq, iq_w), jnp.bfloat16),
            pltpu.VMEM((nq, tq, tq), jnp.int32),
            pltpu.VMEM((tq, tq), jnp.float32),
            pltpu.VMEM((tq, tq), jnp.float32),
            pltpu.VMEM((tq, 1), jnp.float32),
            pltpu.VMEM((tq, 1), jnp.float32),
            pltpu.VMEM((tq, LANES), jnp.float32),
        ],
        compiler_params=_cparams(("arbitrary", "arbitrary", "arbitrary")),
    )(qkvg, qkvg, qkvg, qkvg, idx, idx, idx, bias_tiles)


def _rel_bucket_of(n):
    max_exact = REL_BUCKETS // 2
    nf = jnp.maximum(n, 1).astype(jnp.float32)
    large = max_exact + (jnp.log(nf / max_exact) / math.log(REL_MAX_DIST / max_exact)
                         * (REL_BUCKETS - max_exact)).astype(jnp.int32)
    large = jnp.minimum(large, REL_BUCKETS - 1)
    return jnp.where(n < max_exact, n, large)


def _bias_tiles(rel_bias):
    assert REL_MAX_DIST <= LANES
    dist = jnp.arange(2 * LANES, dtype=jnp.int32)
    bucket = jnp.where(dist >= REL_MAX_DIST, REL_BUCKETS - 1, _rel_bucket_of(dist))
    bv = rel_bias[bucket] - rel_bias[REL_BUCKETS - 1][None, :]
    i = jnp.arange(LANES, dtype=jnp.int32)[:, None]
    j = jnp.arange(LANES, dtype=jnp.int32)[None, :]
    diag = jnp.where((i >= j)[..., None], bv[jnp.maximum(i - j, 0)], 0.0)
    prev = bv[LANES + i - j]
    return jnp.stack([diag, prev], axis=0).transpose(3, 0, 1, 2).astype(jnp.float32)


def _ssd_kernel(z_ref, x_ref, bc_ref, dt_ref, cw_ref, cb_ref, dtb_ref, alog_ref, dexp_ref,
                nw_ref, rexp_ref, o_ref, pad_ref, state_ref, y_ref,
                *, inner, groups):
    f32 = jnp.float32
    bf16 = jnp.bfloat16
    L = SSM_CHUNK
    P = SSM_HEAD_DIM
    N = SSM_STATE
    heads = inner // P
    hpg = heads // groups
    gw = hpg * P
    gn = groups * N
    halo = SUBLANES

    @pl.when(pl.program_id(1) == 0)
    def _():
        pad_ref[0:halo, :] = jnp.zeros((halo, pad_ref.shape[1]), f32)
        state_ref[...] = jnp.zeros(state_ref.shape, f32)

    pad_ref[halo:halo + L, 0:inner] = x_ref[...].astype(f32)
    pad_ref[halo:halo + L, inner:inner + 2 * gn] = bc_ref[...].astype(f32)
    conv = cb_ref[...] + jnp.zeros((L, inner + 2 * gn), f32)
    for j in range(SSM_CONV):
        off = halo - (SSM_CONV - 1) + j
        conv = conv + cw_ref[j:j + 1, :] * pad_ref[off:off + L, :]
    pad_ref[0:halo, :] = pad_ref[L:L + halo, :]
    xbc = _silu(conv)
    xs = xbc[:, 0:inner]
    bm = xbc[:, inner:inner + gn]
    cm = xbc[:, inner + gn:inner + 2 * gn]

    dtr = dt_ref[...] + dtb_ref[...]
    dt = jnp.maximum(dtr, 0.0) + jnp.log1p(jnp.exp(-jnp.abs(dtr)))
    adt = dt * (-jnp.exp(alog_ref[...]))
    ri = lax.broadcasted_iota(jnp.int32, (L, L), 0)
    ci = lax.broadcasted_iota(jnp.int32, (L, L), 1)
    tri = ri >= ci
    cs = _dot_exact_lhs(jnp.where(tri, 1.0, 0.0).astype(bf16), adt)
    cst = cs.T
    rexp = rexp_ref[...]
    dt_e = _dot_exact_rhs(dt, rexp)
    cs_e = _dot_exact_rhs(cs, rexp)
    csl_e = cs_e[L - 1:L, :]
    xdt = xs * dt_e
    xdt_b = xdt.astype(bf16)
    xdec_b = (xdt * jnp.exp(csl_e - cs_e)).astype(bf16)
    ecs_e = jnp.exp(cs_e)
    chunk_decay = jnp.exp(csl_e)
    lane = lax.broadcasted_iota(jnp.int32, (L, LANES), 1)

    for g in range(groups):
        cg = cm[:, g * N:(g + 1) * N].astype(bf16)
        bg = bm[:, g * N:(g + 1) * N]
        gmat = lax.dot_general(cg, bg.astype(bf16), _NT, preferred_element_type=f32)
        bgt = bg.T.astype(bf16)
        sl = slice(g * gw, (g + 1) * gw)
        st_prev = state_ref[g]
        y_off = jnp.dot(cg, st_prev.astype(bf16), preferred_element_type=f32) * ecs_e[:, sl]
        state_ref[g] = st_prev * chunk_decay[:, sl] + jnp.dot(bgt, xdec_b[:, sl],
                                                              preferred_element_type=f32)
        y_ref[:, sl] = y_off
        for pr in range(hpg * P // LANES):
            col = g * gw + pr * LANES
            xp = xdt_b[:, col:col + LANES]
            yp = []
            for sub in range(LANES // P):
                hh = (col // P) + sub
                seg = cs[:, hh:hh + 1] - cst[hh:hh + 1, :]
                lm = jnp.exp(jnp.where(tri, seg, -jnp.inf))
                yp.append(jnp.dot((gmat * lm).astype(bf16), xp, preferred_element_type=f32))
            y_ref[:, col:col + LANES] += jnp.where(lane < P, yp[0], yp[1])

    y = y_ref[...] + xs * dexp_ref[...]
    yg = y * _silu(z_ref[...].astype(f32))
    ms = jnp.mean(yg * yg, axis=-1, keepdims=True)
    o_ref[...] = (yg * lax.rsqrt(ms + NORM_EPS) * nw_ref[...]).astype(o_ref.dtype)


def _ssd(zxbc, dt, conv_w, conv_b, dt_bias, a_log, d_skip, norm_w, batch, seq, inner, groups):
    t = zxbc.shape[0]
    L = SSM_CHUNK
    heads = inner // SSM_HEAD_DIM
    gn = groups * SSM_STATE
    conv_ch = inner + 2 * gn
    assert heads <= LANES and LANES % SSM_HEAD_DIM == 0 and inner % (2 * gn) == 0
    nc = seq // L
    pad_h = LANES - heads
    dtb = jnp.pad(dt_bias, (0, pad_h)).reshape(1, LANES)
    alog = jnp.pad(a_log, (0, pad_h)).reshape(1, LANES)
    dexp = jnp.repeat(d_skip, SSM_HEAD_DIM).reshape(1, inner)
    rexp = (jnp.arange(LANES, dtype=jnp.int32)[:, None]
            == (jnp.arange(inner, dtype=jnp.int32) // SSM_HEAD_DIM)[None, :]).astype(jnp.bfloat16)
    bc_blk = inner * 2 // (2 * gn)
    kern = functools.partial(_ssd_kernel, inner=inner, groups=groups)
    const = lambda b, c: (0, 0)
    return pl.pallas_call(
        kern,
        grid=(batch, nc),
        in_specs=[
            pl.BlockSpec((L, inner), lambda b, c: (b * nc + c, 0)),
            pl.BlockSpec((L, inner), lambda b, c: (b * nc + c, 1)),
            pl.BlockSpec((L, 2 * gn), lambda b, c: (b * nc + c, bc_blk)),
            pl.BlockSpec((L, LANES), lambda b, c: (b * nc + c, 0)),
            pl.BlockSpec((SSM_CONV, conv_ch), const),
            pl.BlockSpec((1, conv_ch), const),
            pl.BlockSpec((1, LANES), const),
            pl.BlockSpec((1, LANES), const),
            pl.BlockSpec((1, inner), const),
            pl.BlockSpec((1, inner), const),
            pl.BlockSpec((LANES, inner), const),
        ],
        out_specs=pl.BlockSpec((L, inner), lambda b, c: (b * nc + c, 0)),
        out_shape=jax.ShapeDtypeStruct((t, inner), jnp.bfloat16),
        scratch_shapes=[
            pltpu.VMEM((SUBLANES + L, conv_ch), jnp.float32),
            pltpu.VMEM((groups, SSM_STATE, inner // groups), jnp.float32),
            pltpu.VMEM((L, inner), jnp.float32),
        ],
        compiler_params=_cparams(("arbitrary", "arbitrary")),
    )(zxbc, zxbc, zxbc, dt, conv_w, conv_b.reshape(1, conv_ch), dtb, alog, dexp,
      norm_w.reshape(1, inner), rexp)


def _proj_residual_final_kernel(a_ref, w_ref, x_ref, nw_ref, o_ref):
    x2 = x_ref[...] + jnp.dot(a_ref[...], w_ref[...], preferred_element_type=jnp.float32)
    ms = jnp.mean(x2 * x2, axis=-1, keepdims=True)
    o_ref[...] = (x2 * lax.rsqrt(ms + NORM_EPS) * nw_ref[...]).astype(o_ref.dtype)


def _proj_residual_final(a, w, x, nw, tm):
    t, k = a.shape
    d = w.shape[1]
    return pl.pallas_call(
        _proj_residual_final_kernel,
        grid=(t // tm,),
        in_specs=[pl.BlockSpec((tm, k), lambda i: (i, 0)),
                  pl.BlockSpec((k, d), lambda i: (0, 0)),
                  pl.BlockSpec((tm, d), lambda i: (i, 0)),
                  pl.BlockSpec((1, d), lambda i: (0, 0))],
        out_specs=pl.BlockSpec((tm, d), lambda i: (i, 0)),
        out_shape=jax.ShapeDtypeStruct((t, d), jnp.float32),
        compiler_params=_cparams(("parallel",)),
    )(a, w, x, nw.reshape(1, d))


def _row_tile(t, want):
    while t % want:
        want //= 2
    return want


def kernel(x, norm_w, a_w_in, a_w_out, rel_bias, b_w_in, b_conv_w, b_conv_b, b_dt_bias, b_a_log,
           b_d, b_norm_w, b_w_out, final_norm_w):
    batch, seq, d = x.shape
    t = batch * seq
    bf16 = jnp.bfloat16
    assert norm_w.shape[0] == 2 and a_w_in.shape[0] == 1 and b_w_in.shape[0] == 1
    xf = x.reshape(t, d)

    att_w = a_w_out.shape[1]
    heads = att_w // ATT_HEAD_DIM
    iq_w = IDX_HEADS * IDX_HEAD_DIM
    wa = a_w_in[0]
    scale = ATT_HEAD_DIM ** -0.5
    w_qkvg = jnp.concatenate([wa[:, :att_w] * scale, wa[:, att_w:4 * att_w]], axis=1).astype(bf16)
    n_small = wa.shape[1] - 4 * att_w - iq_w
    assert n_small == IDX_HEAD_DIM + IDX_HEADS <= LANES
    w_idx = jnp.pad(wa[:, 4 * att_w:], ((0, 0), (0, LANES - n_small))).astype(bf16)

    tm = _row_tile(t, 1024)
    qkvg = _norm_matmul(xf, norm_w[0], w_qkvg, bf16, tm, _row_tile(4 * att_w, 512))
    idx = _norm_matmul(xf, norm_w[0], w_idx, jnp.float32, _row_tile(t, 512), iq_w + LANES)
    tq = _row_tile(seq, ATT_Q_BLOCK)
    att = _dsa_attention(qkvg, idx, _bias_tiles(rel_bias), batch, seq, heads, tq)
    x1, h1 = _proj_residual_norm(att, a_w_out[0].astype(bf16), xf, norm_w[1], bf16,
                                 _row_tile(t, 256))

    inner = b_w_out.shape[1]
    ssm_heads = b_dt_bias.shape[1]
    conv_ch = b_conv_w.shape[2]
    groups = (conv_ch - inner) // (2 * SSM_STATE)
    wb = b_w_in[0]
    w_zxbc = wb[:, :inner + conv_ch].astype(bf16)
    w_dt = jnp.pad(wb[:, inner + conv_ch:], ((0, 0), (0, LANES - ssm_heads))).astype(bf16)
    zxbc = _matmul(h1, w_zxbc, bf16, tm, _row_tile(inner + conv_ch, 1024))
    dt = _matmul(h1, w_dt, jnp.float32, tm, LANES)
    y = _ssd(zxbc, dt, b_conv_w[0], b_conv_b[0], b_dt_bias[0], b_a_log[0], b_d[0], b_norm_w[0],
             batch, seq, inner, groups)
    out = _proj_residual_final(y, b_w_out[0].astype(bf16), x1, final_norm_w, _row_tile(t, 256))
    return out.reshape(batch, seq, d)
```

```python
import functools
import math

import jax
import jax.numpy as jnp
from jax import lax
from jax.experimental import pallas as pl
from jax.experimental.pallas import tpu as pltpu

NORM_EPS = 1e-6

ATT_HEAD_DIM = 128
IDX_HEADS = 16
IDX_HEAD_DIM = 64
TOPK_MAX = 256
REL_BUCKETS = 32
REL_MAX_DIST = 128

SSM_HEAD_DIM = 64
SSM_STATE = 128
SSM_CONV = 4
SSM_CHUNK = 128

LANES = 128
SUBLANES = 8
BF16_ROWS = 16
VMEM_LIMIT_BYTES = 56 * 1024 * 1024

INT_MIN = -2 ** 31
NEG_BIG = -1e30
ATT_Q_BLOCK = 512

_NT = (((1,), (1,)), ((), ()))


def _cparams(sem):
    return pltpu.CompilerParams(dimension_semantics=sem, vmem_limit_bytes=VMEM_LIMIT_BYTES)


def _silu(x):
    return x * (0.5 * jnp.tanh(0.5 * x) + 0.5)


def _split3(x):
    hi = x.astype(jnp.bfloat16)
    r1 = x - hi.astype(jnp.float32)
    mid = r1.astype(jnp.bfloat16)
    lo = (r1 - mid.astype(jnp.float32)).astype(jnp.bfloat16)
    return hi, mid, lo


def _dot_exact_lhs(a01, x):
    hi, mid, lo = _split3(x)
    f = functools.partial(jnp.dot, preferred_element_type=jnp.float32)
    return f(a01, hi) + f(a01, mid) + f(a01, lo)


def _dot_exact_rhs(x, b01):
    hi, mid, lo = _split3(x)
    f = functools.partial(jnp.dot, preferred_element_type=jnp.float32)
    return f(hi, b01) + f(mid, b01) + f(lo, b01)


def _rmsnorm_kernel(x_ref, nw_ref, o_ref):
    x = x_ref[...]
    ms = jnp.mean(x * x, axis=-1, keepdims=True)
    o_ref[...] = (x * lax.rsqrt(ms + NORM_EPS) * nw_ref[...]).astype(o_ref.dtype)


def _rmsnorm(x, nw, out_dtype, tm):
    t, d = x.shape
    return pl.pallas_call(
        _rmsnorm_kernel,
        grid=(t // tm,),
        in_specs=[pl.BlockSpec((tm, d), lambda i: (i, 0)),
                  pl.BlockSpec((1, d), lambda i: (0, 0))],
        out_specs=pl.BlockSpec((tm, d), lambda i: (i, 0)),
        out_shape=jax.ShapeDtypeStruct((t, d), out_dtype),
        compiler_params=_cparams(("parallel",)),
    )(x, nw.reshape(1, d))


def _matmul_kernel(a_ref, w_ref, o_ref):
    o_ref[...] = jnp.dot(a_ref[...], w_ref[...],
                         preferred_element_type=jnp.float32).astype(o_ref.dtype)


def _matmul(a, w, out_dtype, tm, tn):
    t, d = a.shape
    n = w.shape[1]
    return pl.pallas_call(
        _matmul_kernel,
        grid=(n // tn, t // tm),
        in_specs=[pl.BlockSpec((tm, d), lambda j, i: (i, 0)),
                  pl.BlockSpec((d, tn), lambda j, i: (0, j))],
        out_specs=pl.BlockSpec((tm, tn), lambda j, i: (i, j)),
        out_shape=jax.ShapeDtypeStruct((t, n), out_dtype),
        compiler_params=_cparams(("parallel", "parallel")),
    )(a, w)


def _matmul_heads_kernel(a_ref, w_ref, o_ref):
    acc = jnp.dot(a_ref[...], w_ref[...], preferred_element_type=jnp.float32)
    for j in range(o_ref.shape[0]):
        o_ref[j] = acc[:, j * LANES:(j + 1) * LANES].astype(o_ref.dtype)


def _matmul_heads(a, w, out_dtype, tm, tn):
    t, d = a.shape
    n = w.shape[1]
    return pl.pallas_call(
        _matmul_heads_kernel,
        grid=(n // tn, t // tm),
        in_specs=[pl.BlockSpec((tm, d), lambda j, i: (i, 0)),
                  pl.BlockSpec((d, tn), lambda j, i: (0, j))],
        out_specs=pl.BlockSpec((tn // LANES, tm, LANES), lambda j, i: (j, i, 0)),
        out_shape=jax.ShapeDtypeStruct((n // LANES, t, LANES), out_dtype),
        compiler_params=_cparams(("parallel", "parallel")),
    )(a, w)


def _matmul_t_kernel(wt_ref, a_ref, o_ref):
    o_ref[0] = lax.dot_general(wt_ref[...], a_ref[...], _NT,
                               preferred_element_type=jnp.float32).astype(o_ref.dtype)


def _matmul_t(wt, a, batch, seq, out_dtype, tm, tn):
    t, d = a.shape
    n = wt.shape[0]
    mb = seq // tm
    return pl.pallas_call(
        _matmul_t_kernel,
        grid=(n // tn, t // tm),
        in_specs=[pl.BlockSpec((tn, d), lambda j, i: (j, 0)),
                  pl.BlockSpec((tm, d), lambda j, i: (i, 0))],
        out_specs=pl.BlockSpec((1, tn, tm), lambda j, i: (i // mb, j, i % mb)),
        out_shape=jax.ShapeDtypeStruct((batch, n, seq), out_dtype),
        compiler_params=_cparams(("parallel", "parallel")),
    )(wt, a)


def _proj_residual_norm_kernel(a_ref, w_ref, x_ref, nw_ref, xo_ref, ho_ref):
    x1 = x_ref[...] + jnp.dot(a_ref[...], w_ref[...], preferred_element_type=jnp.float32)
    xo_ref[...] = x1
    ms = jnp.mean(x1 * x1, axis=-1, keepdims=True)
    ho_ref[...] = (x1 * lax.rsqrt(ms + NORM_EPS) * nw_ref[...]).astype(ho_ref.dtype)


def _proj_residual_norm(a, w, x, nw, h_dtype, tm):
    t, k = a.shape
    d = w.shape[1]
    return pl.pallas_call(
        _proj_residual_norm_kernel,
        grid=(t // tm,),
        in_specs=[pl.BlockSpec((tm, k), lambda i: (i, 0)),
                  pl.BlockSpec((k, d), lambda i: (0, 0)),
                  pl.BlockSpec((tm, d), lambda i: (i, 0)),
                  pl.BlockSpec((1, d), lambda i: (0, 0))],
        out_specs=[pl.BlockSpec((tm, d), lambda i: (i, 0)),
                   pl.BlockSpec((tm, d), lambda i: (i, 0))],
        out_shape=[jax.ShapeDtypeStruct((t, d), jnp.float32),
                   jax.ShapeDtypeStruct((t, d), h_dtype)],
        compiler_params=_cparams(("parallel",)),
    )(a, w, x, nw.reshape(1, d))


def _dsa_kernel(q_ref, g_ref, k_ref, vt_ref, iq_ref, ikq_ref, ika_ref, bias_ref, o_ref,
                ikbd_ref, iqb_ref, key_ref, vx_ref, bdiag_ref, bprev_ref, m_ref, acc_ref,
                *, tq, topk):
    qb = pl.program_id(1)
    h = pl.program_id(2)
    nck = qb + 1
    nq = key_ref.shape[0]
    pairs = IDX_HEADS // 2
    f32 = jnp.float32
    bf16 = jnp.bfloat16
    hd = ATT_HEAD_DIM

    @pl.when((h == 0) & (qb == 0))
    def _():
        blk = ika_ref[...]
        lane = lax.broadcasted_iota(jnp.int32, blk.shape, 1)
        a = jnp.where(lane < IDX_HEAD_DIM, blk, 0.0)
        ikbd_ref[0] = a.astype(bf16)
        ikbd_ref[1] = pltpu.roll(a, IDX_HEAD_DIM, 1).astype(bf16)
        vx_ref[:, hd:hd + BF16_ROWS, :] = jnp.ones((nq, BF16_ROWS, tq), bf16)

    @pl.when(h == 0)
    def _():
        iqb_ref[...] = iq_ref[...].astype(bf16)
        iwt = ikq_ref[...].T[IDX_HEAD_DIM:IDX_HEAD_DIM + IDX_HEADS, :] * (
            IDX_HEADS ** -0.5 * IDX_HEAD_DIM ** -0.5)
        krow = lax.broadcasted_iota(jnp.int32, (tq, tq), 0)
        qcol = qb * tq + lax.broadcasted_iota(jnp.int32, (tq, tq), 1)

        def score_chunk(c, carry):
            start = pl.multiple_of(c * tq, tq)
            ka = ikbd_ref[0, pl.ds(start, tq), :]
            kb = ikbd_ref[1, pl.ds(start, tq), :]
            sc = jnp.zeros((tq, tq), f32)
            for j in range(pairs):
                rhs = iqb_ref[:, j * LANES:(j + 1) * LANES]
                d0 = lax.dot_general(ka, rhs, _NT, preferred_element_type=f32)
                d1 = lax.dot_general(kb, rhs, _NT, preferred_element_type=f32)
                sc = sc + jnp.maximum(d0, 0.0) * iwt[2 * j:2 * j + 1, :]
                sc = sc + jnp.maximum(d1, 0.0) * iwt[2 * j + 1:2 * j + 2, :]
            bits = lax.bitcast_convert_type(sc, jnp.int32)
            key = bits ^ ((bits >> 31) & jnp.int32(0x7FFFFFFF))
            key_ref[c] = jnp.where(c * tq + krow <= qcol, key, jnp.int32(INT_MIN))
            return carry

        lax.fori_loop(0, nck, score_chunk, 0)

        def bit_step(i, tau_u):
            cand_u = tau_u | lax.shift_left(jnp.int32(1), 31 - i)
            cand_s = cand_u ^ jnp.int32(INT_MIN)

            def count_chunk(c, part):
                w = jnp.where(key_ref[c] >= cand_s, 1.0, 0.0)
                return part + jnp.sum(w.reshape(tq // SUBLANES, SUBLANES, tq), axis=0)

            part = lax.fori_loop(0, nck, count_chunk, jnp.zeros((SUBLANES, tq), f32))
            cnt = jnp.sum(part, axis=0, keepdims=True)
            return jnp.where(cnt >= topk, cand_u, tau_u)

        tau_u = lax.fori_loop(0, 32, bit_step, jnp.zeros((1, tq), jnp.int32))
        tau = tau_u ^ jnp.int32(INT_MIN)

        def mask_chunk(c, carry):
            key = key_ref[c]
            sel = (key >= tau) & (key != jnp.int32(INT_MIN))
            key_ref[c] = lax.bitcast_convert_type(jnp.where(sel, 0.0, NEG_BIG).astype(f32),
                                                  jnp.int32)
            return carry

        lax.fori_loop(0, nck, mask_chunk, 0)

        bdiag_ref[...] = jnp.zeros(bdiag_ref.shape, f32)
        bprev_ref[...] = jnp.zeros(bprev_ref.shape, f32)

    nb = tq // LANES
    dtile = bias_ref[0, 0]
    ptile = bias_ref[0, 1]
    for a in range(nb):
        bdiag_ref[a * LANES:(a + 1) * LANES, a * LANES:(a + 1) * LANES] = dtile
        if a + 1 < nb:
            bdiag_ref[a * LANES:(a + 1) * LANES, (a + 1) * LANES:(a + 2) * LANES] = ptile
    bprev_ref[(nb - 1) * LANES:nb * LANES, 0:LANES] = ptile

    for c in range(nq):
        vx_ref[c, 0:hd, :] = vt_ref[0, :, c * tq:(c + 1) * tq]

    m_ref[...] = jnp.full(m_ref.shape, NEG_BIG, f32)
    acc_ref[...] = jnp.zeros(acc_ref.shape, f32)
    q = q_ref[...]

    def att_chunk(c, bias_tile_ref):
        kc = k_ref[0, pl.ds(pl.multiple_of(c * tq, tq), tq), :]
        s = lax.dot_general(kc, q, _NT, preferred_element_type=f32)
        s = s + lax.bitcast_convert_type(key_ref[c], f32)
        if bias_tile_ref is not None:
            s = s + bias_tile_ref[...]
        m_prev = m_ref[...]
        m_new = jnp.maximum(m_prev, jnp.max(s, axis=0, keepdims=True))
        alpha = jnp.exp(m_prev - m_new)
        p = jnp.exp(s - m_new).astype(bf16)
        acc_ref[...] = alpha * acc_ref[...] + jnp.dot(vx_ref[c], p, preferred_element_type=f32)
        m_ref[...] = m_new

    def far_chunk(c, carry):
        att_chunk(c, None)
        return carry

    lax.fori_loop(0, jnp.maximum(qb - 1, 0), far_chunk, 0)

    @pl.when(qb >= 1)
    def _():
        att_chunk(qb - 1, bprev_ref)

    att_chunk(qb, bdiag_ref)

    out_t = acc_ref[0:hd, :] / acc_ref[hd:hd + 1, :]
    o_ref[...] = (out_t.T * _silu(g_ref[...].astype(f32))).astype(o_ref.dtype)


def _dsa_attention(qg, k_hm, v_t, idx, bias_tiles, batch, seq, heads, tq):
    t = qg.shape[0]
    nq = seq // tq
    topk = min(TOPK_MAX, seq // 4)
    iq_w = IDX_HEADS * IDX_HEAD_DIM
    small_blk = iq_w // LANES
    hd = ATT_HEAD_DIM
    kern = functools.partial(_dsa_kernel, tq=tq, topk=topk)
    return pl.pallas_call(
        kern,
        grid=(batch, nq, heads),
        in_specs=[
            pl.BlockSpec((tq, hd), lambda b, i, h: (b * nq + i, h)),
            pl.BlockSpec((tq, hd), lambda b, i, h: (b * nq + i, heads + h)),
            pl.BlockSpec((1, seq, hd), lambda b, i, h: (h, b, 0)),
            pl.BlockSpec((1, hd, seq), lambda b, i, h: (b, h, 0)),
            pl.BlockSpec((tq, iq_w), lambda b, i, h: (b * nq + i, 0)),
            pl.BlockSpec((tq, LANES), lambda b, i, h: (b * nq + i, small_blk)),
            pl.BlockSpec((seq, LANES), lambda b, i, h: (b, small_blk)),
            pl.BlockSpec((1, 2, LANES, LANES), lambda b, i, h: (h, 0, 0, 0)),
        ],
        out_specs=pl.BlockSpec((tq, hd), lambda b, i, h: (b * nq + i, h)),
        out_shape=jax.ShapeDtypeStruct((t, heads * hd), jnp.bfloat16),
        scratch_shapes=[
            pltpu.VMEM((2, seq, LANES), jnp.bfloat16),
            pltpu.VMEM((tq, iq_w), jnp.bfloat16),
            pltpu.VMEM((nq, tq, tq), jnp.int32),
            pltpu.VMEM((nq, hd + BF16_ROWS, tq), jnp.bfloat16),
            pltpu.VMEM((tq, tq), jnp.float32),
            pltpu.VMEM((tq, tq), jnp.float32),
            pltpu.VMEM((1, tq), jnp.float32),
            pltpu.VMEM((hd + BF16_ROWS, tq), jnp.float32),
        ],
        compiler_params=_cparams(("arbitrary", "arbitrary", "arbitrary")),
    )(qg, qg, k_hm, v_t, idx, idx, idx, bias_tiles)


def _rel_bucket_of(n):
    max_exact = REL_BUCKETS // 2
    nf = jnp.maximum(n, 1).astype(jnp.float32)
    large = max_exact + (jnp.log(nf / max_exact) / math.log(REL_MAX_DIST / max_exact)
                         * (REL_BUCKETS - max_exact)).astype(jnp.int32)
    large = jnp.minimum(large, REL_BUCKETS - 1)
    return jnp.where(n < max_exact, n, large)


def _bias_tiles(rel_bias):
    assert REL_MAX_DIST <= LANES
    dist = jnp.arange(2 * LANES, dtype=jnp.int32)
    bucket = jnp.where(dist >= REL_MAX_DIST, REL_BUCKETS - 1, _rel_bucket_of(dist))
    bv = rel_bias[bucket] - rel_bias[REL_BUCKETS - 1][None, :]
    s = jnp.arange(LANES, dtype=jnp.int32)[:, None]
    q = jnp.arange(LANES, dtype=jnp.int32)[None, :]
    diag = jnp.where((q >= s)[..., None], bv[jnp.maximum(q - s, 0)], 0.0)
    nxt = bv[LANES + q - s]
    return jnp.stack([diag, nxt], axis=0).transpose(3, 0, 1, 2).astype(jnp.float32)


def _ssd_kernel(z_ref, x_ref, bc_ref, dt_ref, cw_ref, cb_ref, dtb_ref, alog_ref, dexp_ref,
                nw_ref, rexp_ref, o_ref, pad_ref, state_ref, y_ref,
                *, inner, groups):
    f32 = jnp.float32
    bf16 = jnp.bfloat16
    L = SSM_CHUNK
    P = SSM_HEAD_DIM
    N = SSM_STATE
    heads = inner // P
    hpg = heads // groups
    gw = hpg * P
    gn = groups * N
    halo = SUBLANES

    @pl.when(pl.program_id(1) == 0)
    def _():
        pad_ref[0:halo, :] = jnp.zeros((halo, pad_ref.shape[1]), f32)
        state_ref[...] = jnp.zeros(state_ref.shape, f32)

    pad_ref[halo:halo + L, 0:inner] = x_ref[...].astype(f32)
    pad_ref[halo:halo + L, inner:inner + 2 * gn] = bc_ref[...].astype(f32)
    conv = cb_ref[...] + jnp.zeros((L, inner + 2 * gn), f32)
    for j in range(SSM_CONV):
        off = halo - (SSM_CONV - 1) + j
        conv = conv + cw_ref[j:j + 1, :] * pad_ref[off:off + L, :]
    pad_ref[0:halo, :] = pad_ref[L:L + halo, :]
    xbc = _silu(conv)
    xs = xbc[:, 0:inner]
    bm = xbc[:, inner:inner + gn]
    cm = xbc[:, inner + gn:inner + 2 * gn]

    dtr = dt_ref[...] + dtb_ref[...]
    dt = jnp.maximum(dtr, 0.0) + jnp.log1p(jnp.exp(-jnp.abs(dtr)))
    adt = dt * (-jnp.exp(alog_ref[...]))
    ri = lax.broadcasted_iota(jnp.int32, (L, L), 0)
    ci = lax.broadcasted_iota(jnp.int32, (L, L), 1)
    tri = ri >= ci
    cs = _dot_exact_lhs(jnp.where(tri, 1.0, 0.0).astype(bf16), adt)
    cst = cs.T
    rexp = rexp_ref[...]
    dt_e = _dot_exact_rhs(dt, rexp)
    cs_e = _dot_exact_rhs(cs, rexp)
    csl_e = cs_e[L - 1:L, :]
    xdt = xs * dt_e
    xdt_b = xdt.astype(bf16)
    xdec_b = (xdt * jnp.exp(csl_e - cs_e)).astype(bf16)
    ecs_e = jnp.exp(cs_e)
    chunk_decay = jnp.exp(csl_e)
    lane = lax.broadcasted_iota(jnp.int32, (L, LANES), 1)

    for g in range(groups):
        cg = cm[:, g * N:(g + 1) * N].astype(bf16)
        bg = bm[:, g * N:(g + 1) * N]
        gmat = lax.dot_general(cg, bg.astype(bf16), _NT, preferred_element_type=f32)
        bgt = bg.T.astype(bf16)
        sl = slice(g * gw, (g + 1) * gw)
        st_prev = state_ref[g]
        y_off = jnp.dot(cg, st_prev.astype(bf16), preferred_element_type=f32) * ecs_e[:, sl]
        state_ref[g] = st_prev * chunk_decay[:, sl] + jnp.dot(bgt, xdec_b[:, sl],
                                                              preferred_element_type=f32)
        y_ref[:, sl] = y_off
        for pr in range(hpg * P // LANES):
            col = g * gw + pr * LANES
            xp = xdt_b[:, col:col + LANES]
            yp = []
            for sub in range(LANES // P):
                hh = (col // P) + sub
                seg = cs[:, hh:hh + 1] - cst[hh:hh + 1, :]
                lm = jnp.exp(jnp.where(tri, seg, -jnp.inf))
                yp.append(jnp.dot((gmat * lm).astype(bf16), xp, preferred_element_type=f32))
            y_ref[:, col:col + LANES] += jnp.where(lane < P, yp[0], yp[1])

    y = y_ref[...] + xs * dexp_ref[...]
    yg = y * _silu(z_ref[...].astype(f32))
    ms = jnp.mean(yg * yg, axis=-1, keepdims=True)
    o_ref[...] = (yg * lax.rsqrt(ms + NORM_EPS) * nw_ref[...]).astype(o_ref.dtype)


def _ssd(zxbc, dt, conv_w, conv_b, dt_bias, a_log, d_skip, norm_w, batch, seq, inner, groups):
    t = zxbc.shape[0]
    L = SSM_CHUNK
    heads = inner // SSM_HEAD_DIM
    gn = groups * SSM_STATE
    conv_ch = inner + 2 * gn
    assert heads <= LANES and LANES % SSM_HEAD_DIM == 0 and inner % (2 * gn) == 0
    nc = seq // L
    pad_h = LANES - heads
    dtb = jnp.pad(dt_bias, (0, pad_h)).reshape(1, LANES)
    alog = jnp.pad(a_log, (0, pad_h)).reshape(1, LANES)
    dexp = jnp.repeat(d_skip, SSM_HEAD_DIM).reshape(1, inner)
    rexp = (jnp.arange(LANES, dtype=jnp.int32)[:, None]
            == (jnp.arange(inner, dtype=jnp.int32) // SSM_HEAD_DIM)[None, :]).astype(jnp.bfloat16)
    bc_blk = inner * 2 // (2 * gn)
    kern = functools.partial(_ssd_kernel, inner=inner, groups=groups)
    const = lambda b, c: (0, 0)
    return pl.pallas_call(
        kern,
        grid=(batch, nc),
        in_specs=[
            pl.BlockSpec((L, inner), lambda b, c: (b * nc + c, 0)),
            pl.BlockSpec((L, inner), lambda b, c: (b * nc + c, 1)),
            pl.BlockSpec((L, 2 * gn), lambda b, c: (b * nc + c, bc_blk)),
            pl.BlockSpec((L, LANES), lambda b, c: (b * nc + c, 0)),
            pl.BlockSpec((SSM_CONV, conv_ch), const),
            pl.BlockSpec((1, conv_ch), const),
            pl.BlockSpec((1, LANES), const),
            pl.BlockSpec((1, LANES), const),
            pl.BlockSpec((1, inner), const),
            pl.BlockSpec((1, inner), const),
            pl.BlockSpec((LANES, inner), const),
        ],
        out_specs=pl.BlockSpec((L, inner), lambda b, c: (b * nc + c, 0)),
        out_shape=jax.ShapeDtypeStruct((t, inner), jnp.bfloat16),
        scratch_shapes=[
            pltpu.VMEM((SUBLANES + L, conv_ch), jnp.float32),
            pltpu.VMEM((groups, SSM_STATE, inner // groups), jnp.float32),
            pltpu.VMEM((L, inner), jnp.float32),
        ],
        compiler_params=_cparams(("arbitrary", "arbitrary")),
    )(zxbc, zxbc, zxbc, dt, conv_w, conv_b.reshape(1, conv_ch), dtb, alog, dexp,
      norm_w.reshape(1, inner), rexp)


def _proj_residual_final_kernel(a_ref, w_ref, x_ref, nw_ref, o_ref):
    x2 = x_ref[...] + jnp.dot(a_ref[...], w_ref[...], preferred_element_type=jnp.float32)
    ms = jnp.mean(x2 * x2, axis=-1, keepdims=True)
    o_ref[...] = (x2 * lax.rsqrt(ms + NORM_EPS) * nw_ref[...]).astype(o_ref.dtype)


def _proj_residual_final(a, w, x, nw, tm):
    t, k = a.shape
    d = w.shape[1]
    return pl.pallas_call(
        _proj_residual_final_kernel,
        grid=(t // tm,),
        in_specs=[pl.BlockSpec((tm, k), lambda i: (i, 0)),
                  pl.BlockSpec((k, d), lambda i: (0, 0)),
                  pl.BlockSpec((tm, d), lambda i: (i, 0)),
                  pl.BlockSpec((1, d), lambda i: (0, 0))],
        out_specs=pl.BlockSpec((tm, d), lambda i: (i, 0)),
        out_shape=jax.ShapeDtypeStruct((t, d), jnp.float32),
        compiler_params=_cparams(("parallel",)),
    )(a, w, x, nw.reshape(1, d))


def _row_tile(t, want):
    while t % want:
        want //= 2
    return want


def kernel(x, norm_w, a_w_in, a_w_out, rel_bias, b_w_in, b_conv_w, b_conv_b, b_dt_bias, b_a_log,
           b_d, b_norm_w, b_w_out, final_norm_w):
    batch, seq, d = x.shape
    t = batch * seq
    bf16 = jnp.bfloat16
    assert norm_w.shape[0] == 2 and a_w_in.shape[0] == 1 and b_w_in.shape[0] == 1
    xf = x.reshape(t, d)

    att_w = a_w_out.shape[1]
    heads = att_w // ATT_HEAD_DIM
    iq_w = IDX_HEADS * IDX_HEAD_DIM
    wa = a_w_in[0]
    scale = ATT_HEAD_DIM ** -0.5
    w_qg = jnp.concatenate([wa[:, :att_w] * scale, wa[:, 3 * att_w:4 * att_w]], axis=1).astype(bf16)
    w_k = wa[:, att_w:2 * att_w].astype(bf16)
    w_vt = wa[:, 2 * att_w:3 * att_w].T.astype(bf16)
    n_small = wa.shape[1] - 4 * att_w - iq_w
    assert n_small == IDX_HEAD_DIM + IDX_HEADS <= LANES
    w_idx = jnp.pad(wa[:, 4 * att_w:], ((0, 0), (0, LANES - n_small))).astype(bf16)

    tm = _row_tile(seq, 1024)
    tn = _row_tile(att_w, 512)
    h0 = _rmsnorm(xf, norm_w[0], bf16, _row_tile(t, 512))
    qg = _matmul(h0, w_qg, bf16, tm, tn)
    k_hm = _matmul_heads(h0, w_k, bf16, tm, tn)
    v_t = _matmul_t(w_vt, h0, batch, seq, bf16, tm, tn)
    idx = _matmul(h0, w_idx, jnp.float32, tm, iq_w + LANES)
    tq = _row_tile(seq, ATT_Q_BLOCK)
    att = _dsa_attention(qg, k_hm, v_t, idx, _bias_tiles(rel_bias), batch, seq, heads, tq)
    x1, h1 = _proj_residual_norm(att, a_w_out[0].astype(bf16), xf, norm_w[1], bf16,
                                 _row_tile(t, 256))

    inner = b_w_out.shape[1]
    ssm_heads = b_dt_bias.shape[1]
    conv_ch = b_conv_w.shape[2]
    groups = (conv_ch - inner) // (2 * SSM_STATE)
    wb = b_w_in[0]
    w_zxbc = wb[:, :inner + conv_ch].astype(bf16)
    w_dt = jnp.pad(wb[:, inner + conv_ch:], ((0, 0), (0, LANES - ssm_heads))).astype(bf16)
    zxbc = _matmul(h1, w_zxbc, bf16, tm, _row_tile(inner + conv_ch, 1024))
    dt = _matmul(h1, w_dt, jnp.float32, tm, LANES)
    y = _ssd(zxbc, dt, b_conv_w[0], b_conv_b[0], b_dt_bias[0], b_a_log[0], b_d[0], b_norm_w[0],
             batch, seq, inner, groups)
    out = _proj_residual_final(y, b_w_out[0].astype(bf16), x1, final_norm_w, _row_tile(t, 256))
    return out.reshape(batch, seq, d)
```

```python
import functools
import math

import jax
import jax.numpy as jnp
from jax import lax
from jax.experimental import pallas as pl
from jax.experimental.pallas import tpu as pltpu

NORM_EPS = 1e-6

ATT_HEAD_DIM = 128
IDX_HEADS = 16
IDX_HEAD_DIM = 64
TOPK_MAX = 256
REL_BUCKETS = 32
REL_MAX_DIST = 128

SSM_HEAD_DIM = 64
SSM_STATE = 128
SSM_CONV = 4
SSM_CHUNK = 128

LANES = 128
SUBLANES = 8
BF16_ROWS = 16
VMEM_LIMIT_BYTES = 56 * 1024 * 1024

INT_MIN = -2 ** 31
NEG_BIG = -1e30
LOG2E = math.log2(math.e)
ATT_Q_BLOCK = 512

_NT = (((1,), (1,)), ((), ()))


def _cparams(sem, flags=None):
    return pltpu.CompilerParams(dimension_semantics=sem, vmem_limit_bytes=VMEM_LIMIT_BYTES,
                                flags=flags)


def _silu(x):
    return x * (0.5 * jnp.tanh(0.5 * x) + 0.5)


def _split3(x):
    hi = x.astype(jnp.bfloat16)
    r1 = x - hi.astype(jnp.float32)
    mid = r1.astype(jnp.bfloat16)
    lo = (r1 - mid.astype(jnp.float32)).astype(jnp.bfloat16)
    return hi, mid, lo


def _dot_exact_lhs(a01, x):
    hi, mid, lo = _split3(x)
    f = functools.partial(jnp.dot, preferred_element_type=jnp.float32)
    return f(a01, hi) + f(a01, mid) + f(a01, lo)


def _dot_exact_rhs(x, b01):
    hi, mid, lo = _split3(x)
    f = functools.partial(jnp.dot, preferred_element_type=jnp.float32)
    return f(hi, b01) + f(mid, b01) + f(lo, b01)


def _rmsnorm_kernel(x_ref, nw_ref, o_ref):
    x = x_ref[...]
    ms = jnp.mean(x * x, axis=-1, keepdims=True)
    o_ref[...] = (x * lax.rsqrt(ms + NORM_EPS) * nw_ref[...]).astype(o_ref.dtype)


def _rmsnorm(x, nw, out_dtype, tm):
    t, d = x.shape
    return pl.pallas_call(
        _rmsnorm_kernel,
        grid=(t // tm,),
        in_specs=[pl.BlockSpec((tm, d), lambda i: (i, 0)),
                  pl.BlockSpec((1, d), lambda i: (0, 0))],
        out_specs=pl.BlockSpec((tm, d), lambda i: (i, 0)),
        out_shape=jax.ShapeDtypeStruct((t, d), out_dtype),
        compiler_params=_cparams(("parallel",)),
    )(x, nw.reshape(1, d))


def _matmul_kernel(a_ref, w_ref, o_ref):
    o_ref[...] = jnp.dot(a_ref[...], w_ref[...],
                         preferred_element_type=jnp.float32).astype(o_ref.dtype)


def _matmul(a, w, out_dtype, tm, tn):
    t, d = a.shape
    n = w.shape[1]
    return pl.pallas_call(
        _matmul_kernel,
        grid=(n // tn, t // tm),
        in_specs=[pl.BlockSpec((tm, d), lambda j, i: (i, 0)),
                  pl.BlockSpec((d, tn), lambda j, i: (0, j))],
        out_specs=pl.BlockSpec((tm, tn), lambda j, i: (i, j)),
        out_shape=jax.ShapeDtypeStruct((t, n), out_dtype),
        compiler_params=_cparams(("parallel", "parallel")),
    )(a, w)


def _proj_kernel(a_ref, w_ref, o_ref, wb_ref, *, layout, scale, tq):
    @pl.when(pl.program_id(1) == 0)
    def _():
        w = w_ref[...]
        if scale != 1.0:
            w = w * scale
        wb_ref[...] = (w.T if layout == "t" else w).astype(wb_ref.dtype)

    f32 = jnp.float32
    if layout == "t":
        acc = lax.dot_general(wb_ref[...], a_ref[...], _NT, preferred_element_type=f32)
        hd = ATT_HEAD_DIM
        for hh in range(o_ref.shape[1]):
            for cc in range(o_ref.shape[2]):
                o_ref[0, hh, cc, 0:hd, :] = acc[hh * hd:(hh + 1) * hd,
                                                cc * tq:(cc + 1) * tq].astype(o_ref.dtype)
                o_ref[0, hh, cc, hd:hd + BF16_ROWS, :] = jnp.ones((BF16_ROWS, tq), o_ref.dtype)
    else:
        acc = jnp.dot(a_ref[...], wb_ref[...], preferred_element_type=f32)
        if layout == "heads":
            for j in range(o_ref.shape[0]):
                o_ref[j] = acc[:, j * LANES:(j + 1) * LANES].astype(o_ref.dtype)
        else:
            o_ref[...] = acc.astype(o_ref.dtype)


def _proj(a, w, col_off, n, out_dtype, tm, tn, layout="rows", scale=1.0, batch=1, tq=LANES):
    t, d = a.shape
    assert col_off % tn == 0 and n % tn == 0 and t % tm == 0
    off = col_off // tn
    seq = t // batch
    mb = seq // tm
    if layout == "rows":
        out_spec = pl.BlockSpec((tm, tn), lambda j, i: (i, j))
        out_shape = (t, n)
    elif layout == "heads":
        out_spec = pl.BlockSpec((tn // LANES, tm, LANES), lambda j, i: (j, i, 0))
        out_shape = (n // LANES, t, LANES)
    else:
        rows = ATT_HEAD_DIM + BF16_ROWS
        out_spec = pl.BlockSpec((1, tn // ATT_HEAD_DIM, tm // tq, rows, tq),
                                lambda j, i: (i // mb, j, i % mb, 0, 0))
        out_shape = (batch, n // ATT_HEAD_DIM, seq // tq, rows, tq)
    wb_shape = (tn, d) if layout == "t" else (d, tn)
    return pl.pallas_call(
        functools.partial(_proj_kernel, layout=layout, scale=scale, tq=tq),
        grid=(n // tn, t // tm),
        in_specs=[pl.BlockSpec((tm, d), lambda j, i: (i, 0)),
                  pl.BlockSpec((d, tn), lambda j, i: (0, j + off))],
        out_specs=out_spec,
        out_shape=jax.ShapeDtypeStruct(out_shape, out_dtype),
        scratch_shapes=[pltpu.VMEM(wb_shape, jnp.bfloat16)],
        compiler_params=_cparams(("parallel", "arbitrary")),
    )(a, w)


def _proj_residual_norm_kernel(a_ref, w_ref, x_ref, nw_ref, xo_ref, ho_ref):
    x1 = x_ref[...] + jnp.dot(a_ref[...], w_ref[...], preferred_element_type=jnp.float32)
    xo_ref[...] = x1
    ms = jnp.mean(x1 * x1, axis=-1, keepdims=True)
    ho_ref[...] = (x1 * lax.rsqrt(ms + NORM_EPS) * nw_ref[...]).astype(ho_ref.dtype)


def _proj_residual_norm(a, w, x, nw, h_dtype, tm):
    t, k = a.shape
    d = w.shape[1]
    return pl.pallas_call(
        _proj_residual_norm_kernel,
        grid=(t // tm,),
        in_specs=[pl.BlockSpec((tm, k), lambda i: (i, 0)),
                  pl.BlockSpec((k, d), lambda i: (0, 0)),
                  pl.BlockSpec((tm, d), lambda i: (i, 0)),
                  pl.BlockSpec((1, d), lambda i: (0, 0))],
        out_specs=[pl.BlockSpec((tm, d), lambda i: (i, 0)),
                   pl.BlockSpec((tm, d), lambda i: (i, 0))],
        out_shape=[jax.ShapeDtypeStruct((t, d), jnp.float32),
                   jax.ShapeDtypeStruct((t, d), h_dtype)],
        compiler_params=_cparams(("parallel",)),
    )(a, w, x, nw.reshape(1, d))


def _dsa_kernel(q_ref, g_ref, k_ref, vt_ref, iq_ref, ikq_ref, ika_ref, bias_ref, o_ref,
                ikbd_ref, iqb_ref, madd_ref, s0_ref, s1_ref, m_ref, cmax_ref, acc_ref,
                *, tq, topk, nq):
    qb = pl.program_id(1)
    h = pl.program_id(2)
    nck = qb + 1
    nb = tq // LANES
    base_d = nq * nb
    base_p = base_d + nb
    pairs = IDX_HEADS // 2
    f32 = jnp.float32
    bf16 = jnp.bfloat16
    i32 = jnp.int32
    hd = ATT_HEAD_DIM

    @pl.when((h == 0) & (qb == 0))
    def _():
        blk = ika_ref[...]
        lane = lax.broadcasted_iota(jnp.int32, blk.shape, 1)
        a = jnp.where(lane < IDX_HEAD_DIM, blk, 0.0)
        ikbd_ref[0] = a.astype(bf16)
        ikbd_ref[1] = pltpu.roll(a, IDX_HEAD_DIM, 1).astype(bf16)

    @pl.when(h == 0)
    def _():
        iqb_ref[...] = iq_ref[...].astype(bf16)
        iwt = ikq_ref[...].T[IDX_HEAD_DIM:IDX_HEAD_DIM + IDX_HEADS, :] * (
            IDX_HEADS ** -0.5 * IDX_HEAD_DIM ** -0.5)
        krow = lax.broadcasted_iota(jnp.int32, (tq, tq), 0)
        qcol = qb * tq + lax.broadcasted_iota(jnp.int32, (tq, tq), 1)

        def score_chunk(c, carry):
            start = pl.multiple_of(c * tq, tq)
            ka = ikbd_ref[0, pl.ds(start, tq), :]
            kb = ikbd_ref[1, pl.ds(start, tq), :]
            sc = jnp.zeros((tq, tq), f32)
            for j in range(pairs):
                rhs = iqb_ref[:, j * LANES:(j + 1) * LANES]
                d0 = lax.dot_general(ka, rhs, _NT, preferred_element_type=f32)
                d1 = lax.dot_general(kb, rhs, _NT, preferred_element_type=f32)
                sc = sc + jnp.maximum(d0, 0.0) * iwt[2 * j:2 * j + 1, :]
                sc = sc + jnp.maximum(d1, 0.0) * iwt[2 * j + 1:2 * j + 2, :]
            bits = lax.bitcast_convert_type(sc, i32)
            key = bits ^ ((bits >> 31) & i32(0x7FFFFFFF))
            key = jnp.where(c * tq + krow <= qcol, key, i32(INT_MIN))
            for a in range(nb):
                madd_ref[c * nb + a] = lax.bitcast_convert_type(
                    key[a * LANES:(a + 1) * LANES, :], f32)
            return carry

        lax.fori_loop(0, nck, score_chunk, 0)

        def keys_of(c, a):
            return lax.bitcast_convert_type(madd_ref[c * nb + a], i32)

        def bit_step(i, tau_u):
            cand_u = tau_u | lax.shift_left(i32(1), 31 - i)
            cand_s = cand_u ^ i32(INT_MIN)

            def count_chunk(c, part):
                for a in range(nb):
                    w = jnp.where(keys_of(c, a) >= cand_s, 1.0, 0.0)
                    part = part + jnp.sum(w.reshape(LANES // SUBLANES, SUBLANES, tq), axis=0)
                return part

            part = lax.fori_loop(0, nck, count_chunk, jnp.zeros((SUBLANES, tq), f32))
            cnt = jnp.sum(part, axis=0, keepdims=True)
            return jnp.where(cnt >= topk, cand_u, tau_u)

        tau_u = lax.fori_loop(0, 32, bit_step, jnp.zeros((1, tq), i32))
        tau = tau_u ^ i32(INT_MIN)

        def mask_chunk(c, carry):
            for a in range(nb):
                key = keys_of(c, a)
                sel = (key >= tau) & (key != i32(INT_MIN))
                madd_ref[c * nb + a] = jnp.where(sel, 0.0, NEG_BIG).astype(f32)
            return carry

        lax.fori_loop(0, nck, mask_chunk, 0)

    dtile = bias_ref[0, 0]
    ptile = bias_ref[0, 1]
    for a in range(nb):
        madd_ref[base_d + a] = madd_ref[qb * nb + a]
        madd_ref[base_d + a, :, a * LANES:(a + 1) * LANES] += dtile
        if a + 1 < nb:
            madd_ref[base_d + a, :, (a + 1) * LANES:(a + 2) * LANES] += ptile
    madd_ref[base_p] = madd_ref[jnp.maximum(qb - 1, 0) * nb + nb - 1]
    madd_ref[base_p, :, 0:LANES] += ptile

    m_ref[...] = jnp.full(m_ref.shape, NEG_BIG, f32)
    acc_ref[...] = jnp.zeros(acc_ref.shape, f32)
    q = q_ref[...]

    s_bufs = (s0_ref, s1_ref)

    def qk(c, par):
        kc = k_ref[0, pl.ds(pl.multiple_of(c * tq, tq), tq), :]
        s = lax.dot_general(kc, q, _NT, preferred_element_type=f32)
        is_diag = c == qb
        is_prev = c == qb - 1
        cmax = None
        for a in range(nb):
            sl = jnp.where(is_diag, base_d + a, c * nb + a)
            if a == nb - 1:
                sl = jnp.where(is_prev, base_p, sl)
            sa = s[a * LANES:(a + 1) * LANES, :] + madd_ref[sl]
            s_bufs[par][a * LANES:(a + 1) * LANES, :] = sa
            sm = jnp.max(sa.reshape(LANES // SUBLANES, SUBLANES, tq), axis=0)
            cmax = sm if cmax is None else jnp.maximum(cmax, sm)
        cmax_ref[...] = jnp.max(cmax, axis=0, keepdims=True)

    def softmax_pv(c, par):
        m_prev = m_ref[...]
        m_new = jnp.maximum(m_prev, cmax_ref[...])
        m_ref[...] = m_new
        p = jnp.exp2(s_bufs[par][...] - m_new).astype(bf16)
        acc_ref[...] = jnp.exp2(m_prev - m_new) * acc_ref[...] + jnp.dot(
            vt_ref[0, 0, c], p, preferred_element_type=f32)

    def step(c, par, last):
        softmax_pv(c, par)
        if not last:
            qk(c + 1, 1 - par)

    def by_parity(c, last):
        for par in range(2):
            pl.when((c & 1) == par)(functools.partial(step, c, par, last))

    qk(0, 0)

    def body(c, carry):
        by_parity(c, False)
        return carry

    lax.fori_loop(0, nck - 1, body, 0)
    by_parity(nck - 1, True)

    out_t = acc_ref[0:hd, :] / acc_ref[hd:hd + 1, :]
    o_ref[...] = (out_t.T * _silu(g_ref[...].astype(f32))).astype(o_ref.dtype)


def _dsa_attention(q, g, k_hm, v_t, iq, iks, bias_tiles, batch, seq, heads, tq):
    t = q.shape[0]
    nq = seq // tq
    nb = tq // LANES
    topk = min(TOPK_MAX, seq // 4)
    iq_w = IDX_HEADS * IDX_HEAD_DIM
    hd = ATT_HEAD_DIM
    vrows = hd + BF16_ROWS
    kern = functools.partial(_dsa_kernel, tq=tq, topk=topk, nq=nq)
    return pl.pallas_call(
        kern,
        grid=(batch, nq, heads),
        in_specs=[
            pl.BlockSpec((tq, hd), lambda b, i, h: (b * nq + i, h)),
            pl.BlockSpec((tq, hd), lambda b, i, h: (b * nq + i, h)),
            pl.BlockSpec((1, seq, hd), lambda b, i, h: (h, b, 0)),
            pl.BlockSpec((1, 1, nq, vrows, tq), lambda b, i, h: (b, h, 0, 0, 0)),
            pl.BlockSpec((tq, iq_w), lambda b, i, h: (b * nq + i, 0)),
            pl.BlockSpec((tq, LANES), lambda b, i, h: (b * nq + i, 0)),
            pl.BlockSpec((seq, LANES), lambda b, i, h: (b, 0)),
            pl.BlockSpec((1, 2, LANES, LANES), lambda b, i, h: (h, 0, 0, 0)),
        ],
        out_specs=pl.BlockSpec((tq, hd), lambda b, i, h: (b * nq + i, h)),
        out_shape=jax.ShapeDtypeStruct((t, heads * hd), jnp.bfloat16),
        scratch_shapes=[
            pltpu.VMEM((2, seq, LANES), jnp.bfloat16),
            pltpu.VMEM((tq, iq_w), jnp.bfloat16),
            pltpu.VMEM((nq * nb + nb + 1, LANES, tq), jnp.float32),
            pltpu.VMEM((tq, tq), jnp.float32),
            pltpu.VMEM((tq, tq), jnp.float32),
            pltpu.VMEM((1, tq), jnp.float32),
            pltpu.VMEM((1, tq), jnp.float32),
            pltpu.VMEM((vrows, tq), jnp.float32),
        ],
        compiler_params=_cparams(("arbitrary", "arbitrary", "arbitrary")),
    )(q, g, k_hm, v_t, iq, iks, iks, bias_tiles)


def _rel_bucket_of(n):
    max_exact = REL_BUCKETS // 2
    nf = jnp.maximum(n, 1).astype(jnp.float32)
    large = max_exact + (jnp.log(nf / max_exact) / math.log(REL_MAX_DIST / max_exact)
                         * (REL_BUCKETS - max_exact)).astype(jnp.int32)
    large = jnp.minimum(large, REL_BUCKETS - 1)
    return jnp.where(n < max_exact, n, large)


def _bias_tiles(rel_bias):
    assert REL_MAX_DIST <= LANES
    n = LANES
    heads = rel_bias.shape[1]
    dist = jnp.arange(2 * n, dtype=jnp.int32)
    bucket = jnp.where(dist >= REL_MAX_DIST, REL_BUCKETS - 1, _rel_bucket_of(dist))
    bv = (rel_bias[bucket] - rel_bias[REL_BUCKETS - 1][None, :]) * LOG2E
    rows = jnp.concatenate([bv.T, jnp.zeros((heads, n), bv.dtype)], axis=1)
    flat = jnp.broadcast_to(rows[:, None, :], (heads, n, 3 * n)).reshape(heads, 3 * n * n)
    toep = flat[:, :n * (3 * n - 1)].reshape(heads, n, 3 * n - 1)[:, :, :2 * n]
    return jnp.stack([toep[:, :, :n], toep[:, :, n:]], axis=1).astype(jnp.float32)


def _ssd_kernel(z_ref, x_ref, bc_ref, dt_ref, cw_ref, cb_ref, dtb_ref, alog_ref, dexp_ref,
                nw_ref, rexp_ref, o_ref, pad_ref, state_ref, y_ref,
                *, inner, groups):
    f32 = jnp.float32
    bf16 = jnp.bfloat16
    L = SSM_CHUNK
    P = SSM_HEAD_DIM
    N = SSM_STATE
    heads = inner // P
    hpg = heads // groups
    gw = hpg * P
    gn = groups * N
    halo = SUBLANES

    @pl.when(pl.program_id(1) == 0)
    def _():
        pad_ref[0:halo, :] = jnp.zeros((halo, pad_ref.shape[1]), f32)
        state_ref[...] = jnp.zeros(state_ref.shape, f32)

    pad_ref[halo:halo + L, 0:inner] = x_ref[...].astype(f32)
    pad_ref[halo:halo + L, inner:inner + 2 * gn] = bc_ref[...].astype(f32)
    conv = cb_ref[...] + jnp.zeros((L, inner + 2 * gn), f32)
    for j in range(SSM_CONV):
        off = halo - (SSM_CONV - 1) + j
        conv = conv + cw_ref[j:j + 1, :] * pad_ref[off:off + L, :]
    pad_ref[0:halo, :] = pad_ref[L:L + halo, :]
    xbc = _silu(conv)
    xs = xbc[:, 0:inner]
    bm = xbc[:, inner:inner + gn]
    cm = xbc[:, inner + gn:inner + 2 * gn]

    dtr = dt_ref[...] + dtb_ref[...]
    dt = jnp.maximum(dtr, 0.0) + jnp.log1p(jnp.exp(-jnp.abs(dtr)))
    adt = dt * (-jnp.exp(alog_ref[...]))
    ri = lax.broadcasted_iota(jnp.int32, (L, L), 0)
    ci = lax.broadcasted_iota(jnp.int32, (L, L), 1)
    tri = ri >= ci
    cs = _dot_exact_lhs(jnp.where(tri, 1.0, 0.0).astype(bf16), adt)
    cst = cs.T
    rexp = rexp_ref[...]
    dt_e = _dot_exact_rhs(dt, rexp)
    cs_e = _dot_exact_rhs(cs, rexp)
    csl_e = cs_e[L - 1:L, :]
    xdt = xs * dt_e
    xdt_b = xdt.astype(bf16)
    xdec_b = (xdt * jnp.exp(csl_e - cs_e)).astype(bf16)
    ecs_e = jnp.exp(cs_e)
    chunk_decay = jnp.exp(csl_e)
    lane = lax.broadcasted_iota(jnp.int32, (L, LANES), 1)

    for g in range(groups):
        cg = cm[:, g * N:(g + 1) * N].astype(bf16)
        bg = bm[:, g * N:(g + 1) * N]
        gmat = lax.dot_general(cg, bg.astype(bf16), _NT, preferred_element_type=f32)
        bgt = bg.T.astype(bf16)
        sl = slice(g * gw, (g + 1) * gw)
        st_prev = state_ref[g]
        y_off = jnp.dot(cg, st_prev.astype(bf16), preferred_element_type=f32) * ecs_e[:, sl]
        state_ref[g] = st_prev * chunk_decay[:, sl] + jnp.dot(bgt, xdec_b[:, sl],
                                                              preferred_element_type=f32)
        y_ref[:, sl] = y_off
        for pr in range(hpg * P // LANES):
            col = g * gw + pr * LANES
            xp = xdt_b[:, col:col + LANES]
            yp = []
            for sub in range(LANES // P):
                hh = (col // P) + sub
                seg = cs[:, hh:hh + 1] - cst[hh:hh + 1, :]
                lm = jnp.exp(jnp.where(tri, seg, -jnp.inf))
                yp.append(jnp.dot((gmat * lm).astype(bf16), xp, preferred_element_type=f32))
            y_ref[:, col:col + LANES] += jnp.where(lane < P, yp[0], yp[1])

    y = y_ref[...] + xs * dexp_ref[...]
    yg = y * _silu(z_ref[...].astype(f32))
    ms = jnp.mean(yg * yg, axis=-1, keepdims=True)
    o_ref[...] = (yg * lax.rsqrt(ms + NORM_EPS) * nw_ref[...]).astype(o_ref.dtype)


def _ssd(zxbc, dt, conv_w, conv_b, dt_bias, a_log, d_skip, norm_w, batch, seq, inner, groups):
    t = zxbc.shape[0]
    L = SSM_CHUNK
    heads = inner // SSM_HEAD_DIM
    gn = groups * SSM_STATE
    conv_ch = inner + 2 * gn
    assert heads <= LANES and LANES % SSM_HEAD_DIM == 0 and inner % (2 * gn) == 0
    nc = seq // L
    pad_h = LANES - heads
    dtb = jnp.pad(dt_bias, (0, pad_h)).reshape(1, LANES)
    alog = jnp.pad(a_log, (0, pad_h)).reshape(1, LANES)
    dexp = jnp.repeat(d_skip, SSM_HEAD_DIM).reshape(1, inner)
    rexp = (jnp.arange(LANES, dtype=jnp.int32)[:, None]
            == (jnp.arange(inner, dtype=jnp.int32) // SSM_HEAD_DIM)[None, :]).astype(jnp.bfloat16)
    bc_blk = inner * 2 // (2 * gn)
    kern = functools.partial(_ssd_kernel, inner=inner, groups=groups)
    const = lambda b, c: (0, 0)
    return pl.pallas_call(
        kern,
        grid=(batch, nc),
        in_specs=[
            pl.BlockSpec((L, inner), lambda b, c: (b * nc + c, 0)),
            pl.BlockSpec((L, inner), lambda b, c: (b * nc + c, 1)),
            pl.BlockSpec((L, 2 * gn), lambda b, c: (b * nc + c, bc_blk)),
            pl.BlockSpec((L, LANES), lambda b, c: (b * nc + c, 0)),
            pl.BlockSpec((SSM_CONV, conv_ch), const),
            pl.BlockSpec((1, conv_ch), const),
            pl.BlockSpec((1, LANES), const),
            pl.BlockSpec((1, LANES), const),
            pl.BlockSpec((1, inner), const),
            pl.BlockSpec((1, inner), const),
            pl.BlockSpec((LANES, inner), const),
        ],
        out_specs=pl.BlockSpec((L, inner), lambda b, c: (b * nc + c, 0)),
        out_shape=jax.ShapeDtypeStruct((t, inner), jnp.bfloat16),
        scratch_shapes=[
            pltpu.VMEM((SUBLANES + L, conv_ch), jnp.float32),
            pltpu.VMEM((groups, SSM_STATE, inner // groups), jnp.float32),
            pltpu.VMEM((L, inner), jnp.float32),
        ],
        compiler_params=_cparams(("arbitrary", "arbitrary")),
    )(zxbc, zxbc, zxbc, dt, conv_w, conv_b.reshape(1, conv_ch), dtb, alog, dexp,
      norm_w.reshape(1, inner), rexp)


def _proj_residual_final_kernel(a_ref, w_ref, x_ref, nw_ref, o_ref):
    x2 = x_ref[...] + jnp.dot(a_ref[...], w_ref[...], preferred_element_type=jnp.float32)
    ms = jnp.mean(x2 * x2, axis=-1, keepdims=True)
    o_ref[...] = (x2 * lax.rsqrt(ms + NORM_EPS) * nw_ref[...]).astype(o_ref.dtype)


def _proj_residual_final(a, w, x, nw, tm):
    t, k = a.shape
    d = w.shape[1]
    return pl.pallas_call(
        _proj_residual_final_kernel,
        grid=(t // tm,),
        in_specs=[pl.BlockSpec((tm, k), lambda i: (i, 0)),
                  pl.BlockSpec((k, d), lambda i: (0, 0)),
                  pl.BlockSpec((tm, d), lambda i: (i, 0)),
                  pl.BlockSpec((1, d), lambda i: (0, 0))],
        out_specs=pl.BlockSpec((tm, d), lambda i: (i, 0)),
        out_shape=jax.ShapeDtypeStruct((t, d), jnp.float32),
        compiler_params=_cparams(("parallel",)),
    )(a, w, x, nw.reshape(1, d))


def _row_tile(t, want):
    while t % want:
        want //= 2
    return want


def kernel(x, norm_w, a_w_in, a_w_out, rel_bias, b_w_in, b_conv_w, b_conv_b, b_dt_bias, b_a_log,
           b_d, b_norm_w, b_w_out, final_norm_w):
    batch, seq, d = x.shape
    t = batch * seq
    bf16 = jnp.bfloat16
    assert norm_w.shape[0] == 2 and a_w_in.shape[0] == 1 and b_w_in.shape[0] == 1
    xf = x.reshape(t, d)

    att_w = a_w_out.shape[1]
    heads = att_w // ATT_HEAD_DIM
    iq_w = IDX_HEADS * IDX_HEAD_DIM
    wa = a_w_in[0]
    n_small = wa.shape[1] - 4 * att_w - iq_w
    assert n_small == IDX_HEAD_DIM + IDX_HEADS <= LANES
    w_iks = jnp.pad(wa[:, 4 * att_w + iq_w:], ((0, 0), (0, LANES - n_small))).astype(bf16)

    tm = _row_tile(seq, 1024)
    tn = _row_tile(att_w, 512)
    tq = _row_tile(seq, ATT_Q_BLOCK)
    h0 = _rmsnorm(xf, norm_w[0], bf16, _row_tile(t, 512))
    q = _proj(h0, wa, 0, att_w, bf16, tm, tn, scale=ATT_HEAD_DIM ** -0.5 * LOG2E)
    k_hm = _proj(h0, wa, att_w, att_w, bf16, tm, tn, layout="heads")
    v_t = _proj(h0, wa, 2 * att_w, att_w, bf16, tm, tn, layout="t", batch=batch, tq=tq)
    g = _proj(h0, wa, 3 * att_w, att_w, bf16, tm, tn)
    iq = _proj(h0, wa, 4 * att_w, iq_w, jnp.float32, tm, _row_tile(iq_w, 512))
    iks = _matmul(h0, w_iks, jnp.float32, tm, LANES)
    att = _dsa_attention(q, g, k_hm, v_t, iq, iks, _bias_tiles(rel_bias), batch, seq, heads, tq)
    x1, h1 = _proj_residual_norm(att, a_w_out[0].astype(bf16), xf, norm_w[1], bf16,
                                 _row_tile(t, 256))

    inner = b_w_out.shape[1]
    ssm_heads = b_dt_bias.shape[1]
    conv_ch = b_conv_w.shape[2]
    groups = (conv_ch - inner) // (2 * SSM_STATE)
    wb = b_w_in[0]
    w_dt = jnp.pad(wb[:, inner + conv_ch:], ((0, 0), (0, LANES - ssm_heads))).astype(bf16)
    zxbc = _proj(h1, wb, 0, inner + conv_ch, bf16, tm, _row_tile(inner + conv_ch, 1024))
    dt = _matmul(h1, w_dt, jnp.float32, tm, LANES)
    y = _ssd(zxbc, dt, b_conv_w[0], b_conv_b[0], b_dt_bias[0], b_a_log[0], b_d[0], b_norm_w[0],
             batch, seq, inner, groups)
    out = _proj_residual_final(y, b_w_out[0].astype(bf16), x1, final_norm_w, _row_tile(t, 256))
    return out.reshape(batch, seq, d)
```

```python
import functools
import math

import jax
import jax.numpy as jnp
from jax import lax
from jax.experimental import pallas as pl
from jax.experimental.pallas import tpu as pltpu

NORM_EPS = 1e-6

ATT_HEAD_DIM = 128
IDX_HEADS = 16
IDX_HEAD_DIM = 64
TOPK_MAX = 256
REL_BUCKETS = 32
REL_MAX_DIST = 128

SSM_HEAD_DIM = 64
SSM_STATE = 128
SSM_CONV = 4
SSM_CHUNK = 128

LANES = 128
SUBLANES = 8
BF16_ROWS = 16
VMEM_LIMIT_BYTES = 56 * 1024 * 1024

INT_MIN = -2 ** 31
NEG_BIG = -1e30
LOG2E = math.log2(math.e)
ATT_Q_BLOCK = 512

_NT = (((1,), (1,)), ((), ()))


def _cparams(sem, flags=None):
    return pltpu.CompilerParams(dimension_semantics=sem, vmem_limit_bytes=VMEM_LIMIT_BYTES,
                                flags=flags)


def _silu(x):
    h = 0.5 * x
    return h + h * jnp.tanh(h)


def _split3(x):
    hi = x.astype(jnp.bfloat16)
    r1 = x - hi.astype(jnp.float32)
    mid = r1.astype(jnp.bfloat16)
    lo = (r1 - mid.astype(jnp.float32)).astype(jnp.bfloat16)
    return hi, mid, lo


def _dot_exact_lhs(a01, x):
    hi, mid, lo = _split3(x)
    f = functools.partial(jnp.dot, preferred_element_type=jnp.float32)
    return f(a01, hi) + f(a01, mid) + f(a01, lo)


def _dot_exact_rhs(x, b01):
    hi, mid, lo = _split3(x)
    f = functools.partial(jnp.dot, preferred_element_type=jnp.float32)
    return f(hi, b01) + f(mid, b01) + f(lo, b01)


def _rmsnorm_kernel(x_ref, nw_ref, o_ref):
    x = x_ref[...]
    ms = jnp.mean(x * x, axis=-1, keepdims=True)
    o_ref[...] = (x * lax.rsqrt(ms + NORM_EPS) * nw_ref[...]).astype(o_ref.dtype)


def _rmsnorm(x, nw, out_dtype, tm):
    t, d = x.shape
    return pl.pallas_call(
        _rmsnorm_kernel,
        grid=(t // tm,),
        in_specs=[pl.BlockSpec((tm, d), lambda i: (i, 0)),
                  pl.BlockSpec((1, d), lambda i: (0, 0))],
        out_specs=pl.BlockSpec((tm, d), lambda i: (i, 0)),
        out_shape=jax.ShapeDtypeStruct((t, d), out_dtype),
        compiler_params=_cparams(("parallel",)),
    )(x, nw.reshape(1, d))


def _matmul_kernel(a_ref, w_ref, o_ref):
    o_ref[...] = jnp.dot(a_ref[...], w_ref[...],
                         preferred_element_type=jnp.float32).astype(o_ref.dtype)


def _matmul(a, w, out_dtype, tm, tn):
    t, d = a.shape
    n = w.shape[1]
    return pl.pallas_call(
        _matmul_kernel,
        grid=(n // tn, t // tm),
        in_specs=[pl.BlockSpec((tm, d), lambda j, i: (i, 0)),
                  pl.BlockSpec((d, tn), lambda j, i: (0, j))],
        out_specs=pl.BlockSpec((tm, tn), lambda j, i: (i, j)),
        out_shape=jax.ShapeDtypeStruct((t, n), out_dtype),
        compiler_params=_cparams(("parallel", "parallel")),
    )(a, w)


def _proj_kernel(a_ref, w_ref, o_ref, wb_ref, *, layout, scale, tq):
    @pl.when(pl.program_id(1) == 0)
    def _():
        w = w_ref[...]
        if scale != 1.0:
            w = w * scale
        wb_ref[...] = (w.T if layout == "t" else w).astype(wb_ref.dtype)

    f32 = jnp.float32
    if layout == "t":
        acc = lax.dot_general(wb_ref[...], a_ref[...], _NT, preferred_element_type=f32)
        hd = ATT_HEAD_DIM
        for hh in range(o_ref.shape[1]):
            for cc in range(o_ref.shape[2]):
                o_ref[0, hh, cc, 0:hd, :] = acc[hh * hd:(hh + 1) * hd,
                                                cc * tq:(cc + 1) * tq].astype(o_ref.dtype)
                o_ref[0, hh, cc, hd:hd + BF16_ROWS, :] = jnp.ones((BF16_ROWS, tq), o_ref.dtype)
    else:
        acc = jnp.dot(a_ref[...], wb_ref[...], preferred_element_type=f32)
        if layout == "heads":
            for j in range(o_ref.shape[0]):
                o_ref[j] = acc[:, j * LANES:(j + 1) * LANES].astype(o_ref.dtype)
        else:
            o_ref[...] = acc.astype(o_ref.dtype)


def _proj(a, w, col_off, n, out_dtype, tm, tn, layout="rows", scale=1.0, batch=1, tq=LANES):
    t, d = a.shape
    assert col_off % tn == 0 and n % tn == 0 and t % tm == 0
    off = col_off // tn
    seq = t // batch
    mb = seq // tm
    if layout == "rows":
        out_spec = pl.BlockSpec((tm, tn), lambda j, i: (i, j))
        out_shape = (t, n)
    elif layout == "heads":
        out_spec = pl.BlockSpec((tn // LANES, tm, LANES), lambda j, i: (j, i, 0))
        out_shape = (n // LANES, t, LANES)
    else:
        rows = ATT_HEAD_DIM + BF16_ROWS
        out_spec = pl.BlockSpec((1, tn // ATT_HEAD_DIM, tm // tq, rows, tq),
                                lambda j, i: (i // mb, j, i % mb, 0, 0))
        out_shape = (batch, n // ATT_HEAD_DIM, seq // tq, rows, tq)
    wb_shape = (tn, d) if layout == "t" else (d, tn)
    return pl.pallas_call(
        functools.partial(_proj_kernel, layout=layout, scale=scale, tq=tq),
        grid=(n // tn, t // tm),
        in_specs=[pl.BlockSpec((tm, d), lambda j, i: (i, 0)),
                  pl.BlockSpec((d, tn), lambda j, i: (0, j + off))],
        out_specs=out_spec,
        out_shape=jax.ShapeDtypeStruct(out_shape, out_dtype),
        scratch_shapes=[pltpu.VMEM(wb_shape, jnp.bfloat16)],
        compiler_params=_cparams(("parallel", "arbitrary")),
    )(a, w)


def _proj_residual_norm_kernel(a_ref, w_ref, x_ref, nw_ref, xo_ref, ho_ref):
    x1 = x_ref[...] + jnp.dot(a_ref[...], w_ref[...], preferred_element_type=jnp.float32)
    xo_ref[...] = x1
    ms = jnp.mean(x1 * x1, axis=-1, keepdims=True)
    ho_ref[...] = (x1 * lax.rsqrt(ms + NORM_EPS) * nw_ref[...]).astype(ho_ref.dtype)


def _proj_residual_norm(a, w, x, nw, h_dtype, tm):
    t, k = a.shape
    d = w.shape[1]
    return pl.pallas_call(
        _proj_residual_norm_kernel,
        grid=(t // tm,),
        in_specs=[pl.BlockSpec((tm, k), lambda i: (i, 0)),
                  pl.BlockSpec((k, d), lambda i: (0, 0)),
                  pl.BlockSpec((tm, d), lambda i: (i, 0)),
                  pl.BlockSpec((1, d), lambda i: (0, 0))],
        out_specs=[pl.BlockSpec((tm, d), lambda i: (i, 0)),
                   pl.BlockSpec((tm, d), lambda i: (i, 0))],
        out_shape=[jax.ShapeDtypeStruct((t, d), jnp.float32),
                   jax.ShapeDtypeStruct((t, d), h_dtype)],
        compiler_params=_cparams(("parallel",)),
    )(a, w, x, nw.reshape(1, d))


def _dsa_kernel(q_ref, g_ref, k_ref, vt_ref, iq_ref, ikq_ref, ika_ref, bias_ref, o_ref,
                ikbd_ref, iqb_ref, madd_ref, s0_ref, s1_ref, m_ref, cmax_ref, acc_ref,
                *, tq, topk, nq):
    qb = pl.program_id(1)
    h = pl.program_id(2)
    nck = qb + 1
    nb = tq // LANES
    base_d = nq * nb
    base_p = base_d + nb
    pairs = IDX_HEADS // 2
    f32 = jnp.float32
    bf16 = jnp.bfloat16
    i32 = jnp.int32
    hd = ATT_HEAD_DIM

    @pl.when((h == 0) & (qb == 0))
    def _():
        blk = ika_ref[...]
        lane = lax.broadcasted_iota(jnp.int32, blk.shape, 1)
        a = jnp.where(lane < IDX_HEAD_DIM, blk, 0.0)
        ikbd_ref[0] = a.astype(bf16)
        ikbd_ref[1] = pltpu.roll(a, IDX_HEAD_DIM, 1).astype(bf16)

    @pl.when(h == 0)
    def _():
        iqb_ref[...] = iq_ref[...].astype(bf16)
        iwt = ikq_ref[...].T[IDX_HEAD_DIM:IDX_HEAD_DIM + IDX_HEADS, :] * (
            IDX_HEADS ** -0.5 * IDX_HEAD_DIM ** -0.5)
        krow = lax.broadcasted_iota(jnp.int32, (tq, tq), 0)
        qcol = qb * tq + lax.broadcasted_iota(jnp.int32, (tq, tq), 1)

        def score_chunk(c, carry):
            start = pl.multiple_of(c * tq, tq)
            ka = ikbd_ref[0, pl.ds(start, tq), :]
            kb = ikbd_ref[1, pl.ds(start, tq), :]
            sc = jnp.zeros((tq, tq), f32)
            for j in range(pairs):
                rhs = iqb_ref[:, j * LANES:(j + 1) * LANES]
                d0 = lax.dot_general(ka, rhs, _NT, preferred_element_type=f32)
                d1 = lax.dot_general(kb, rhs, _NT, preferred_element_type=f32)
                sc = sc + jnp.maximum(d0, 0.0) * iwt[2 * j:2 * j + 1, :]
                sc = sc + jnp.maximum(d1, 0.0) * iwt[2 * j + 1:2 * j + 2, :]
            bits = lax.bitcast_convert_type(sc, i32)
            key = bits ^ ((bits >> 31) & i32(0x7FFFFFFF))
            key = jnp.where(c * tq + krow <= qcol, key, i32(INT_MIN))
            for a in range(nb):
                madd_ref[c * nb + a] = lax.bitcast_convert_type(
                    key[a * LANES:(a + 1) * LANES, :], f32)
            return carry

        lax.fori_loop(0, nck, score_chunk, 0)

        def keys_of(c, a):
            return lax.bitcast_convert_type(madd_ref[c * nb + a], i32)

        def bit_step(i, tau_u):
            cand_u = tau_u | lax.shift_left(i32(1), 31 - i)
            cand_s = cand_u ^ i32(INT_MIN)

            def count_chunk(c, part):
                for a in range(nb):
                    w = jnp.where(keys_of(c, a) >= cand_s, 1.0, 0.0)
                    part = part + jnp.sum(w.reshape(LANES // SUBLANES, SUBLANES, tq), axis=0)
                return part

            part = lax.fori_loop(0, nck, count_chunk, jnp.zeros((SUBLANES, tq), f32))
            cnt = jnp.sum(part, axis=0, keepdims=True)
            return jnp.where(cnt >= topk, cand_u, tau_u)

        tau_u = lax.fori_loop(0, 32, bit_step, jnp.zeros((1, tq), i32))
        tau = tau_u ^ i32(INT_MIN)

        def mask_chunk(c, carry):
            for a in range(nb):
                key = keys_of(c, a)
                sel = (key >= tau) & (key != i32(INT_MIN))
                madd_ref[c * nb + a] = jnp.where(sel, 0.0, NEG_BIG).astype(f32)
            return carry

        lax.fori_loop(0, nck, mask_chunk, 0)

    dtile = bias_ref[0, 0]
    ptile = bias_ref[0, 1]
    for a in range(nb):
        madd_ref[base_d + a] = madd_ref[qb * nb + a]
        madd_ref[base_d + a, :, a * LANES:(a + 1) * LANES] += dtile
        if a + 1 < nb:
            madd_ref[base_d + a, :, (a + 1) * LANES:(a + 2) * LANES] += ptile
    madd_ref[base_p] = madd_ref[jnp.maximum(qb - 1, 0) * nb + nb - 1]
    madd_ref[base_p, :, 0:LANES] += ptile

    m_ref[...] = jnp.full(m_ref.shape, NEG_BIG, f32)
    acc_ref[...] = jnp.zeros(acc_ref.shape, f32)
    q = q_ref[...]

    s_bufs = (s0_ref, s1_ref)

    def qk(c, par):
        kc = k_ref[0, pl.ds(pl.multiple_of(c * tq, tq), tq), :]
        s = lax.dot_general(kc, q, _NT, preferred_element_type=f32)
        is_diag = c == qb
        is_prev = c == qb - 1
        cmax = None
        for a in range(nb):
            sl = jnp.where(is_diag, base_d + a, c * nb + a)
            if a == nb - 1:
                sl = jnp.where(is_prev, base_p, sl)
            sa = s[a * LANES:(a + 1) * LANES, :] + madd_ref[sl]
            s_bufs[par][a * LANES:(a + 1) * LANES, :] = sa
            sm = jnp.max(sa.reshape(LANES // SUBLANES, SUBLANES, tq), axis=0)
            cmax = sm if cmax is None else jnp.maximum(cmax, sm)
        cmax_ref[...] = jnp.max(cmax, axis=0, keepdims=True)

    def softmax_pv(c, par):
        m_prev = m_ref[...]
        m_new = jnp.maximum(m_prev, cmax_ref[...])
        m_ref[...] = m_new
        p = jnp.exp2(s_bufs[par][...] - m_new).astype(bf16)
        acc_ref[...] = jnp.exp2(m_prev - m_new) * acc_ref[...] + jnp.dot(
            vt_ref[0, 0, c], p, preferred_element_type=f32)

    def step(c, par, last):
        softmax_pv(c, par)
        if not last:
            qk(c + 1, 1 - par)

    def by_parity(c, last):
        for par in range(2):
            pl.when((c & 1) == par)(functools.partial(step, c, par, last))

    qk(0, 0)

    def body(c, carry):
        by_parity(c, False)
        return carry

    lax.fori_loop(0, nck - 1, body, 0)
    by_parity(nck - 1, True)

    out_t = acc_ref[0:hd, :] / acc_ref[hd:hd + 1, :]
    o_ref[...] = (out_t.T * _silu(g_ref[...].astype(f32))).astype(o_ref.dtype)


def _dsa_attention(q, g, k_hm, v_t, iq, iks, bias_tiles, batch, seq, heads, tq):
    t = q.shape[0]
    nq = seq // tq
    nb = tq // LANES
    topk = min(TOPK_MAX, seq // 4)
    iq_w = IDX_HEADS * IDX_HEAD_DIM
    hd = ATT_HEAD_DIM
    vrows = hd + BF16_ROWS
    kern = functools.partial(_dsa_kernel, tq=tq, topk=topk, nq=nq)
    return pl.pallas_call(
        kern,
        grid=(batch, nq, heads),
        in_specs=[
            pl.BlockSpec((tq, hd), lambda b, i, h: (b * nq + i, h)),
            pl.BlockSpec((tq, hd), lambda b, i, h: (b * nq + i, h)),
            pl.BlockSpec((1, seq, hd), lambda b, i, h: (h, b, 0)),
            pl.BlockSpec((1, 1, nq, vrows, tq), lambda b, i, h: (b, h, 0, 0, 0)),
            pl.BlockSpec((tq, iq_w), lambda b, i, h: (b * nq + i, 0)),
            pl.BlockSpec((tq, LANES), lambda b, i, h: (b * nq + i, 0)),
            pl.BlockSpec((seq, LANES), lambda b, i, h: (b, 0)),
            pl.BlockSpec((1, 2, LANES, LANES), lambda b, i, h: (h, 0, 0, 0)),
        ],
        out_specs=pl.BlockSpec((tq, hd), lambda b, i, h: (b * nq + i, h)),
        out_shape=jax.ShapeDtypeStruct((t, heads * hd), jnp.bfloat16),
        scratch_shapes=[
            pltpu.VMEM((2, seq, LANES), jnp.bfloat16),
            pltpu.VMEM((tq, iq_w), jnp.bfloat16),
            pltpu.VMEM((nq * nb + nb + 1, LANES, tq), jnp.float32),
            pltpu.VMEM((tq, tq), jnp.float32),
            pltpu.VMEM((tq, tq), jnp.float32),
            pltpu.VMEM((1, tq), jnp.float32),
            pltpu.VMEM((1, tq), jnp.float32),
            pltpu.VMEM((vrows, tq), jnp.float32),
        ],
        compiler_params=_cparams(("arbitrary", "arbitrary", "arbitrary")),
    )(q, g, k_hm, v_t, iq, iks, iks, bias_tiles)


def _rel_bucket_of(n):
    max_exact = REL_BUCKETS // 2
    nf = jnp.maximum(n, 1).astype(jnp.float32)
    large = max_exact + (jnp.log(nf / max_exact) / math.log(REL_MAX_DIST / max_exact)
                         * (REL_BUCKETS - max_exact)).astype(jnp.int32)
    large = jnp.minimum(large, REL_BUCKETS - 1)
    return jnp.where(n < max_exact, n, large)


def _bias_tiles(rel_bias):
    assert REL_MAX_DIST <= LANES
    n = LANES
    heads = rel_bias.shape[1]
    dist = jnp.arange(2 * n, dtype=jnp.int32)
    bucket = jnp.where(dist >= REL_MAX_DIST, REL_BUCKETS - 1, _rel_bucket_of(dist))
    bv = (rel_bias[bucket] - rel_bias[REL_BUCKETS - 1][None, :]) * LOG2E
    rows = jnp.concatenate([bv.T, jnp.zeros((heads, n), bv.dtype)], axis=1)
    flat = jnp.broadcast_to(rows[:, None, :], (heads, n, 3 * n)).reshape(heads, 3 * n * n)
    toep = flat[:, :n * (3 * n - 1)].reshape(heads, n, 3 * n - 1)[:, :, :2 * n]
    return jnp.stack([toep[:, :, :n], toep[:, :, n:]], axis=1).astype(jnp.float32)


def _ssd_kernel(z_ref, x_ref, bc_ref, dt_ref, shift_ref, cw_ref, cbb_ref, dtb_ref, alog_ref,
                dexp_ref, nw_ref, rexp_ref, o_ref, xe_ref, taps_ref, state_ref, y_ref,
                *, inner, groups):
    f32 = jnp.float32
    bf16 = jnp.bfloat16
    L = SSM_CHUNK
    P = SSM_HEAD_DIM
    N = SSM_STATE
    heads = inner // P
    hpg = heads // groups
    gw = hpg * P
    gn = groups * N
    halo = BF16_ROWS
    ext = halo + L
    dot = functools.partial(jnp.dot, preferred_element_type=f32)

    @pl.when(pl.program_id(1) == 0)
    def _():
        xe_ref[0:halo, :] = jnp.zeros((halo, xe_ref.shape[1]), bf16)
        taps_ref[SSM_CONV * ext:SSM_CONV * ext + halo, :] = cbb_ref[...]
        state_ref[...] = jnp.zeros(state_ref.shape, f32)

    xe_ref[halo:ext, 0:inner] = x_ref[...]
    xe_ref[halo:ext, inner:inner + 2 * gn] = bc_ref[...]
    xe = xe_ref[...]
    for j in range(SSM_CONV):
        taps_ref[j * ext:(j + 1) * ext, :] = xe * cw_ref[j:j + 1, :]
    xe_ref[0:halo, :] = xe_ref[L:ext, :]
    xbc = _silu(dot(shift_ref[...], taps_ref[...]))
    xs = xbc[:, 0:inner]
    bm = xbc[:, inner:inner + gn]
    cm = xbc[:, inner + gn:inner + 2 * gn]

    dtr = dt_ref[...] + dtb_ref[...]
    dt = jnp.maximum(dtr, 0.0) + jnp.log1p(jnp.exp(-jnp.abs(dtr)))
    adt = dt * (-jnp.exp(alog_ref[...]))
    ri = lax.broadcasted_iota(jnp.int32, (L, L), 0)
    ci = lax.broadcasted_iota(jnp.int32, (L, L), 1)
    tri = ri >= ci
    cs = _dot_exact_lhs(jnp.where(tri, 1.0, 0.0).astype(bf16), adt)
    cst = cs.T
    rexp = rexp_ref[...]
    csl = cs[L - 1:L, :]
    dt_hi = dt.astype(bf16)
    dt_e = dot(dt_hi, rexp) + dot((dt - dt_hi.astype(f32)).astype(bf16), rexp)
    ecs_e = dot(jnp.exp(cs).astype(bf16), rexp)
    dec_e = dot(jnp.exp(csl - cs).astype(bf16), rexp)
    chunk_decay = _dot_exact_rhs(jnp.broadcast_to(jnp.exp(csl), (SUBLANES, LANES)), rexp)[0:1, :]
    xdt = xs * dt_e
    xdt_b = xdt.astype(bf16)
    xdec_b = (xdt * dec_e).astype(bf16)
    lane = lax.broadcasted_iota(jnp.int32, (L, LANES), 1)

    for g in range(groups):
        cg = cm[:, g * N:(g + 1) * N].astype(bf16)
        bg = bm[:, g * N:(g + 1) * N]
        gmat = lax.dot_general(cg, bg.astype(bf16), _NT, preferred_element_type=f32)
        bgt = bg.T.astype(bf16)
        sl = slice(g * gw, (g + 1) * gw)
        st_prev = state_ref[g]
        y_off = jnp.dot(cg, st_prev.astype(bf16), preferred_element_type=f32) * ecs_e[:, sl]
        state_ref[g] = st_prev * chunk_decay[:, sl] + jnp.dot(bgt, xdec_b[:, sl],
                                                              preferred_element_type=f32)
        y_ref[:, sl] = y_off
        for pr in range(hpg * P // LANES):
            col = g * gw + pr * LANES
            xp = xdt_b[:, col:col + LANES]
            lhs, rhs = [], []
            for sub in range(LANES // P):
                hh = (col // P) + sub
                seg = cs[:, hh:hh + 1] - cst[hh:hh + 1, :]
                lm = jnp.exp(jnp.where(tri, seg, -jnp.inf))
                lhs.append((gmat * lm).astype(bf16))
                rhs.append(jnp.where((lane >= sub * P) & (lane < (sub + 1) * P), xp,
                                     jnp.zeros_like(xp)))
            y_ref[:, col:col + LANES] += dot(jnp.concatenate(lhs, axis=1),
                                             jnp.concatenate(rhs, axis=0))

    y = y_ref[...] + xs * dexp_ref[...]
    yg = y * _silu(z_ref[...].astype(f32))
    ms = jnp.mean(yg * yg, axis=-1, keepdims=True)
    o_ref[...] = (yg * lax.rsqrt(ms + NORM_EPS) * nw_ref[...]).astype(o_ref.dtype)


def _ssd(zxbc, dt, conv_w, conv_b, dt_bias, a_log, d_skip, norm_w, batch, seq, inner, groups):
    t = zxbc.shape[0]
    L = SSM_CHUNK
    heads = inner // SSM_HEAD_DIM
    gn = groups * SSM_STATE
    conv_ch = inner + 2 * gn
    assert heads <= LANES and LANES % SSM_HEAD_DIM == 0 and inner % (2 * gn) == 0
    nc = seq // L
    pad_h = LANES - heads
    dtb = jnp.pad(dt_bias, (0, pad_h)).reshape(1, LANES)
    alog = jnp.pad(a_log, (0, pad_h)).reshape(1, LANES)
    dexp = jnp.repeat(d_skip, SSM_HEAD_DIM).reshape(1, inner)
    rexp = (jnp.arange(LANES, dtype=jnp.int32)[:, None]
            == (jnp.arange(inner, dtype=jnp.int32) // SSM_HEAD_DIM)[None, :]).astype(jnp.bfloat16)
    bc_blk = inner * 2 // (2 * gn)
    halo = BF16_ROWS
    ext = halo + L
    rows = jnp.arange(L, dtype=jnp.int32)[:, None]
    cols = jnp.arange(SSM_CONV * ext + halo, dtype=jnp.int32)[None, :]
    blk, pos = cols // ext, cols % ext
    shift = jnp.where(blk < SSM_CONV, pos == halo + rows - (SSM_CONV - 1 - blk),
                      pos < 2).astype(jnp.bfloat16)
    cb_hi = conv_b.astype(jnp.bfloat16)
    cb_lo = (conv_b - cb_hi.astype(jnp.float32)).astype(jnp.bfloat16)
    cbb = jnp.zeros((halo, conv_ch), jnp.bfloat16).at[0].set(cb_hi).at[1].set(cb_lo)
    kern = functools.partial(_ssd_kernel, inner=inner, groups=groups)
    const = lambda b, c: (0, 0)
    return pl.pallas_call(
        kern,
        grid=(batch, nc),
        in_specs=[
            pl.BlockSpec((L, inner), lambda b, c: (b * nc + c, 0)),
            pl.BlockSpec((L, inner), lambda b, c: (b * nc + c, 1)),
            pl.BlockSpec((L, 2 * gn), lambda b, c: (b * nc + c, bc_blk)),
            pl.BlockSpec((L, LANES), lambda b, c: (b * nc + c, 0)),
            pl.BlockSpec((L, SSM_CONV * ext + halo), const),
            pl.BlockSpec((SSM_CONV, conv_ch), const),
            pl.BlockSpec((halo, conv_ch), const),
            pl.BlockSpec((1, LANES), const),
            pl.BlockSpec((1, LANES), const),
            pl.BlockSpec((1, inner), const),
            pl.BlockSpec((1, inner), const),
            pl.BlockSpec((LANES, inner), const),
        ],
        out_specs=pl.BlockSpec((L, inner), lambda b, c: (b * nc + c, 0)),
        out_shape=jax.ShapeDtypeStruct((t, inner), jnp.bfloat16),
        scratch_shapes=[
            pltpu.VMEM((ext, conv_ch), jnp.bfloat16),
            pltpu.VMEM((SSM_CONV * ext + halo, conv_ch), jnp.bfloat16),
            pltpu.VMEM((groups, SSM_STATE, inner // groups), jnp.float32),
            pltpu.VMEM((L, inner), jnp.float32),
        ],
        compiler_params=_cparams(("arbitrary", "arbitrary")),
    )(zxbc, zxbc, zxbc, dt, shift, conv_w.astype(jnp.bfloat16), cbb, dtb, alog, dexp,
      norm_w.reshape(1, inner), rexp)


def _proj_residual_final_kernel(a_ref, w_ref, x_ref, nw_ref, o_ref):
    x2 = x_ref[...] + jnp.dot(a_ref[...], w_ref[...], preferred_element_type=jnp.float32)
    ms = jnp.mean(x2 * x2, axis=-1, keepdims=True)
    o_ref[...] = (x2 * lax.rsqrt(ms + NORM_EPS) * nw_ref[...]).astype(o_ref.dtype)


def _proj_residual_final(a, w, x, nw, tm):
    t, k = a.shape
    d = w.shape[1]
    return pl.pallas_call(
        _proj_residual_final_kernel,
        grid=(t // tm,),
        in_specs=[pl.BlockSpec((tm, k), lambda i: (i, 0)),
                  pl.BlockSpec((k, d), lambda i: (0, 0)),
                  pl.BlockSpec((tm, d), lambda i: (i, 0)),
                  pl.BlockSpec((1, d), lambda i: (0, 0))],
        out_specs=pl.BlockSpec((tm, d), lambda i: (i, 0)),
        out_shape=jax.ShapeDtypeStruct((t, d), jnp.float32),
        compiler_params=_cparams(("parallel",)),
    )(a, w, x, nw.reshape(1, d))


def _row_tile(t, want):
    while t % want:
        want //= 2
    return want


def kernel(x, norm_w, a_w_in, a_w_out, rel_bias, b_w_in, b_conv_w, b_conv_b, b_dt_bias, b_a_log,
           b_d, b_norm_w, b_w_out, final_norm_w):
    batch, seq, d = x.shape
    t = batch * seq
    bf16 = jnp.bfloat16
    assert norm_w.shape[0] == 2 and a_w_in.shape[0] == 1 and b_w_in.shape[0] == 1
    xf = x.reshape(t, d)

    att_w = a_w_out.shape[1]
    heads = att_w // ATT_HEAD_DIM
    iq_w = IDX_HEADS * IDX_HEAD_DIM
    wa = a_w_in.reshape(a_w_in.shape[1:])
    n_small = wa.shape[1] - 4 * att_w - iq_w
    assert n_small == IDX_HEAD_DIM + IDX_HEADS <= LANES
    w_iks = jnp.pad(wa[:, 4 * att_w + iq_w:], ((0, 0), (0, LANES - n_small))).astype(bf16)

    tm = _row_tile(seq, 1024)
    tn = _row_tile(att_w, 512)
    tq = _row_tile(seq, ATT_Q_BLOCK)
    h0 = _rmsnorm(xf, norm_w[0], bf16, _row_tile(t, 512))
    q = _proj(h0, wa, 0, att_w, bf16, tm, tn, scale=ATT_HEAD_DIM ** -0.5 * LOG2E)
    k_hm = _proj(h0, wa, att_w, att_w, bf16, tm, tn, layout="heads")
    v_t = _proj(h0, wa, 2 * att_w, att_w, bf16, tm, tn, layout="t", batch=batch, tq=tq)
    g = _proj(h0, wa, 3 * att_w, att_w, bf16, tm, tn)
    iq = _proj(h0, wa, 4 * att_w, iq_w, jnp.float32, tm, _row_tile(iq_w, 512))
    iks = _matmul(h0, w_iks, jnp.float32, tm, LANES)
    att = _dsa_attention(q, g, k_hm, v_t, iq, iks, _bias_tiles(rel_bias), batch, seq, heads, tq)
    x1, h1 = _proj_residual_norm(att, a_w_out[0].astype(bf16), xf, norm_w[1], bf16,
                                 _row_tile(t, 256))

    inner = b_w_out.shape[1]
    ssm_heads = b_dt_bias.shape[1]
    conv_ch = b_conv_w.shape[2]
    groups = (conv_ch - inner) // (2 * SSM_STATE)
    wb = b_w_in.reshape(b_w_in.shape[1:])
    w_dt = jnp.pad(wb[:, inner + conv_ch:], ((0, 0), (0, LANES - ssm_heads))).astype(bf16)
    zxbc = _proj(h1, wb, 0, inner + conv_ch, bf16, tm, _row_tile(inner + conv_ch, 1024))
    dt = _matmul(h1, w_dt, jnp.float32, tm, LANES)
    y = _ssd(zxbc, dt, b_conv_w[0], b_conv_b[0], b_dt_bias[0], b_a_log[0], b_d[0], b_norm_w[0],
             batch, seq, inner, groups)
    out = _proj_residual_final(y, b_w_out[0].astype(bf16), x1, final_norm_w, _row_tile(t, 256))
    return out.reshape(batch, seq, d)
```

```python
import functools
import math

import jax
import jax.numpy as jnp
from jax import lax
from jax.experimental import pallas as pl
from jax.experimental.pallas import tpu as pltpu

NORM_EPS = 1e-6

ATT_HEAD_DIM = 128
IDX_HEADS = 16
IDX_HEAD_DIM = 64
TOPK_MAX = 256
REL_BUCKETS = 32
REL_MAX_DIST = 128

SSM_HEAD_DIM = 64
SSM_STATE = 128
SSM_CONV = 4
SSM_CHUNK = 128

LANES = 128
SUBLANES = 8
BF16_ROWS = 16
VMEM_LIMIT_BYTES = 56 * 1024 * 1024

INT_MIN = -2 ** 31
NEG_BIG = -1e30
LOG2E = math.log2(math.e)
ATT_Q_BLOCK = 512

_NT = (((1,), (1,)), ((), ()))


def _cparams(sem, flags=None):
    return pltpu.CompilerParams(dimension_semantics=sem, vmem_limit_bytes=VMEM_LIMIT_BYTES,
                                flags=flags)


def _silu(x):
    h = 0.5 * x
    return h + h * jnp.tanh(h)


def _split3(x):
    hi = x.astype(jnp.bfloat16)
    r1 = x - hi.astype(jnp.float32)
    mid = r1.astype(jnp.bfloat16)
    lo = (r1 - mid.astype(jnp.float32)).astype(jnp.bfloat16)
    return hi, mid, lo


def _dot_exact_lhs(a01, x):
    hi, mid, lo = _split3(x)
    f = functools.partial(jnp.dot, preferred_element_type=jnp.float32)
    return f(a01, hi) + f(a01, mid) + f(a01, lo)


def _dot_exact_rhs(x, b01):
    hi, mid, lo = _split3(x)
    f = functools.partial(jnp.dot, preferred_element_type=jnp.float32)
    return f(hi, b01) + f(mid, b01) + f(lo, b01)


def _rmsnorm_kernel(x_ref, nw_ref, o_ref):
    x = x_ref[...]
    ms = jnp.mean(x * x, axis=-1, keepdims=True)
    o_ref[...] = (x * lax.rsqrt(ms + NORM_EPS) * nw_ref[...]).astype(o_ref.dtype)


def _rmsnorm(x, nw, out_dtype, tm):
    t, d = x.shape
    return pl.pallas_call(
        _rmsnorm_kernel,
        grid=(t // tm,),
        in_specs=[pl.BlockSpec((tm, d), lambda i: (i, 0)),
                  pl.BlockSpec((1, d), lambda i: (0, 0))],
        out_specs=pl.BlockSpec((tm, d), lambda i: (i, 0)),
        out_shape=jax.ShapeDtypeStruct((t, d), out_dtype),
        compiler_params=_cparams(("parallel",)),
    )(x, nw.reshape(1, d))


def _matmul_kernel(a_ref, w_ref, o_ref):
    o_ref[...] = jnp.dot(a_ref[...], w_ref[...],
                         preferred_element_type=jnp.float32).astype(o_ref.dtype)


def _matmul(a, w, out_dtype, tm, tn):
    t, d = a.shape
    n = w.shape[1]
    return pl.pallas_call(
        _matmul_kernel,
        grid=(n // tn, t // tm),
        in_specs=[pl.BlockSpec((tm, d), lambda j, i: (i, 0)),
                  pl.BlockSpec((d, tn), lambda j, i: (0, j))],
        out_specs=pl.BlockSpec((tm, tn), lambda j, i: (i, j)),
        out_shape=jax.ShapeDtypeStruct((t, n), out_dtype),
        compiler_params=_cparams(("parallel", "parallel")),
    )(a, w)


def _proj_kernel(a_ref, w_ref, o_ref, wb_ref, *, layout, scale, tq):
    @pl.when(pl.program_id(1) == 0)
    def _():
        w = w_ref[...]
        if scale != 1.0:
            w = w * scale
        wb_ref[...] = (w if layout == "t" else w.T).astype(wb_ref.dtype)

    f32 = jnp.float32
    if layout == "t":
        acc = lax.dot_general(wb_ref[...], a_ref[...], _NT, preferred_element_type=f32)
        hd = ATT_HEAD_DIM
        for hh in range(o_ref.shape[1]):
            for cc in range(o_ref.shape[2]):
                o_ref[0, hh, cc, 0:hd, :] = acc[hh * hd:(hh + 1) * hd,
                                                cc * tq:(cc + 1) * tq].astype(o_ref.dtype)
                o_ref[0, hh, cc, hd:hd + BF16_ROWS, :] = jnp.ones((BF16_ROWS, tq), o_ref.dtype)
    else:
        acc = jnp.dot(a_ref[...], wb_ref[...], preferred_element_type=f32)
        if layout == "heads":
            for j in range(o_ref.shape[0]):
                o_ref[j] = acc[:, j * LANES:(j + 1) * LANES].astype(o_ref.dtype)
        else:
            o_ref[...] = acc.astype(o_ref.dtype)


def _proj(a, wt, col_off, n, out_dtype, tm, tn, layout="rows", scale=1.0, batch=1, tq=LANES):
    t, d = a.shape
    assert col_off % tn == 0 and n % tn == 0 and t % tm == 0
    off = col_off // tn
    seq = t // batch
    mb = seq // tm
    if layout == "rows":
        out_spec = pl.BlockSpec((tm, tn), lambda j, i: (i, j))
        out_shape = (t, n)
    elif layout == "heads":
        out_spec = pl.BlockSpec((tn // LANES, tm, LANES), lambda j, i: (j, i, 0))
        out_shape = (n // LANES, t, LANES)
    else:
        rows = ATT_HEAD_DIM + BF16_ROWS
        out_spec = pl.BlockSpec((1, tn // ATT_HEAD_DIM, tm // tq, rows, tq),
                                lambda j, i: (i // mb, j, i % mb, 0, 0))
        out_shape = (batch, n // ATT_HEAD_DIM, seq // tq, rows, tq)
    wb_shape = (tn, d) if layout == "t" else (d, tn)
    return pl.pallas_call(
        functools.partial(_proj_kernel, layout=layout, scale=scale, tq=tq),
        grid=(n // tn, t // tm),
        in_specs=[pl.BlockSpec((tm, d), lambda j, i: (i, 0)),
                  pl.BlockSpec((tn, d), lambda j, i: (j + off, 0))],
        out_specs=out_spec,
        out_shape=jax.ShapeDtypeStruct(out_shape, out_dtype),
        scratch_shapes=[pltpu.VMEM(wb_shape, jnp.bfloat16)],
        compiler_params=_cparams(("parallel", "arbitrary")),
    )(a, wt)


def _proj_residual_norm_kernel(a_ref, w_ref, x_ref, nw_ref, xo_ref, ho_ref):
    x1 = x_ref[...] + jnp.dot(a_ref[...], w_ref[...], preferred_element_type=jnp.float32)
    xo_ref[...] = x1
    ms = jnp.mean(x1 * x1, axis=-1, keepdims=True)
    ho_ref[...] = (x1 * lax.rsqrt(ms + NORM_EPS) * nw_ref[...]).astype(ho_ref.dtype)


def _proj_residual_norm(a, w, x, nw, h_dtype, tm):
    t, k = a.shape
    d = w.shape[1]
    return pl.pallas_call(
        _proj_residual_norm_kernel,
        grid=(t // tm,),
        in_specs=[pl.BlockSpec((tm, k), lambda i: (i, 0)),
                  pl.BlockSpec((k, d), lambda i: (0, 0)),
                  pl.BlockSpec((tm, d), lambda i: (i, 0)),
                  pl.BlockSpec((1, d), lambda i: (0, 0))],
        out_specs=[pl.BlockSpec((tm, d), lambda i: (i, 0)),
                   pl.BlockSpec((tm, d), lambda i: (i, 0))],
        out_shape=[jax.ShapeDtypeStruct((t, d), jnp.float32),
                   jax.ShapeDtypeStruct((t, d), h_dtype)],
        compiler_params=_cparams(("parallel",)),
    )(a, w, x, nw.reshape(1, d))


def _dsa_kernel(q_ref, g_ref, k_ref, vt_ref, iq_ref, ikq_ref, ika_ref, bias_ref, o_ref,
                ikbd_ref, iqb_ref, madd_ref, s0_ref, s1_ref, m_ref, acc_ref,
                *, tq, topk, nq):
    qb = pl.program_id(1)
    h = pl.program_id(2)
    nck = qb + 1
    nb = tq // LANES
    base_d = nq * nb
    base_p = base_d + nb
    pairs = IDX_HEADS // 2
    f32 = jnp.float32
    bf16 = jnp.bfloat16
    i32 = jnp.int32
    hd = ATT_HEAD_DIM

    @pl.when((h == 0) & (qb == 0))
    def _():
        blk = ika_ref[...]
        lane = lax.broadcasted_iota(jnp.int32, blk.shape, 1)
        a = jnp.where(lane < IDX_HEAD_DIM, blk, 0.0)
        ikbd_ref[0] = a.astype(bf16)
        ikbd_ref[1] = pltpu.roll(a, IDX_HEAD_DIM, 1).astype(bf16)

    @pl.when(h == 0)
    def _():
        iqb_ref[...] = iq_ref[...].astype(bf16)
        iwt = ikq_ref[...].T[IDX_HEAD_DIM:IDX_HEAD_DIM + IDX_HEADS, :] * (
            IDX_HEADS ** -0.5 * IDX_HEAD_DIM ** -0.5)
        krow = lax.broadcasted_iota(jnp.int32, (tq, tq), 0)
        qcol = qb * tq + lax.broadcasted_iota(jnp.int32, (tq, tq), 1)

        def score_chunk(c, carry):
            start = pl.multiple_of(c * tq, tq)
            ka = ikbd_ref[0, pl.ds(start, tq), :]
            kb = ikbd_ref[1, pl.ds(start, tq), :]
            sc = jnp.zeros((tq, tq), f32)
            for j in range(pairs):
                rhs = iqb_ref[:, j * LANES:(j + 1) * LANES]
                d0 = lax.dot_general(ka, rhs, _NT, preferred_element_type=f32)
                d1 = lax.dot_general(kb, rhs, _NT, preferred_element_type=f32)
                sc = sc + jnp.maximum(d0, 0.0) * iwt[2 * j:2 * j + 1, :]
                sc = sc + jnp.maximum(d1, 0.0) * iwt[2 * j + 1:2 * j + 2, :]
            bits = lax.bitcast_convert_type(sc, i32)
            key = bits ^ ((bits >> 31) & i32(0x7FFFFFFF))
            key = jnp.where(c * tq + krow <= qcol, key, i32(INT_MIN))
            for a in range(nb):
                madd_ref[c * nb + a] = lax.bitcast_convert_type(
                    key[a * LANES:(a + 1) * LANES, :], f32)
            return carry

        lax.fori_loop(0, nck, score_chunk, 0)

        def keys_of(c, a):
            return lax.bitcast_convert_type(madd_ref[c * nb + a], i32)

        def bit_step(i, tau_u):
            cand_u = tau_u | lax.shift_left(i32(1), 31 - i)
            cand_s = cand_u ^ i32(INT_MIN)

            def count_chunk(c, part):
                for a in range(nb):
                    w = jnp.where(keys_of(c, a) >= cand_s, 1.0, 0.0)
                    part = part + jnp.sum(w.reshape(LANES // SUBLANES, SUBLANES, tq), axis=0)
                return part

            part = lax.fori_loop(0, nck, count_chunk, jnp.zeros((SUBLANES, tq), f32))
            cnt = jnp.sum(part, axis=0, keepdims=True)
            return jnp.where(cnt >= topk, cand_u, tau_u)

        tau_u = lax.fori_loop(0, 32, bit_step, jnp.zeros((1, tq), i32))
        tau = tau_u ^ i32(INT_MIN)

        def mask_chunk(c, carry):
            for a in range(nb):
                key = keys_of(c, a)
                sel = (key >= tau) & (key != i32(INT_MIN))
                madd_ref[c * nb + a] = jnp.where(sel, 0.0, NEG_BIG).astype(f32)
            return carry

        lax.fori_loop(0, nck, mask_chunk, 0)

    dtile = bias_ref[0, 0]
    ptile = bias_ref[0, 1]
    for a in range(nb):
        madd_ref[base_d + a] = madd_ref[qb * nb + a]
        madd_ref[base_d + a, :, a * LANES:(a + 1) * LANES] += dtile
        if a + 1 < nb:
            madd_ref[base_d + a, :, (a + 1) * LANES:(a + 2) * LANES] += ptile
    madd_ref[base_p] = madd_ref[jnp.maximum(qb - 1, 0) * nb + nb - 1]
    madd_ref[base_p, :, 0:LANES] += ptile

    m_ref[...] = jnp.full(m_ref.shape, NEG_BIG, f32)
    acc_ref[...] = jnp.zeros(acc_ref.shape, f32)
    q = q_ref[...]

    s_bufs = (s0_ref, s1_ref)

    def qk(c, par):
        kc = k_ref[0, pl.ds(pl.multiple_of(c * tq, tq), tq), :]
        s = lax.dot_general(kc, q, _NT, preferred_element_type=f32)
        is_diag = c == qb
        is_prev = c == qb - 1
        cmax = None
        for a in range(nb):
            sl = jnp.where(is_diag, base_d + a, c * nb + a)
            if a == nb - 1:
                sl = jnp.where(is_prev, base_p, sl)
            sa = s[a * LANES:(a + 1) * LANES, :] + madd_ref[sl]
            s_bufs[par][a * LANES:(a + 1) * LANES, :] = sa
            sm = jnp.max(sa.reshape(LANES // SUBLANES, SUBLANES, tq), axis=0)
            cmax = sm if cmax is None else jnp.maximum(cmax, sm)
        return jnp.max(cmax, axis=0, keepdims=True)

    def softmax_pv(c, par, cmax):
        m_prev = m_ref[...]
        m_new = jnp.maximum(m_prev, cmax)
        m_ref[...] = m_new
        p = jnp.exp2(s_bufs[par][...] - m_new).astype(bf16)
        acc_ref[...] = jnp.exp2(m_prev - m_new) * acc_ref[...] + jnp.dot(
            vt_ref[0, 0, c], p, preferred_element_type=f32)

    def pair(i, cm):
        c = 2 * i
        softmax_pv(c, 0, cm)
        softmax_pv(c + 1, 1, qk(c + 1, 1))
        return qk(c + 2, 0)

    npairs = (nck - 1) // 2
    cm = lax.fori_loop(0, npairs, pair, qk(0, 0))
    c0 = 2 * npairs

    @pl.when(nck - c0 == 1)
    def _():
        softmax_pv(c0, 0, cm)

    @pl.when(nck - c0 == 2)
    def _():
        softmax_pv(c0, 0, cm)
        softmax_pv(c0 + 1, 1, qk(c0 + 1, 1))

    out_t = acc_ref[0:hd, :] / acc_ref[hd:hd + 1, :]
    o_ref[...] = (out_t.T * _silu(g_ref[...].astype(f32))).astype(o_ref.dtype)


def _dsa_attention(q, g, k_hm, v_t, iq, iks, bias_tiles, batch, seq, heads, tq):
    t = q.shape[0]
    nq = seq // tq
    nb = tq // LANES
    topk = min(TOPK_MAX, seq // 4)
    iq_w = IDX_HEADS * IDX_HEAD_DIM
    hd = ATT_HEAD_DIM
    vrows = hd + BF16_ROWS
    kern = functools.partial(_dsa_kernel, tq=tq, topk=topk, nq=nq)
    return pl.pallas_call(
        kern,
        grid=(batch, nq, heads),
        in_specs=[
            pl.BlockSpec((tq, hd), lambda b, i, h: (b * nq + i, h)),
            pl.BlockSpec((tq, hd), lambda b, i, h: (b * nq + i, h)),
            pl.BlockSpec((1, seq, hd), lambda b, i, h: (h, b, 0)),
            pl.BlockSpec((1, 1, nq, vrows, tq), lambda b, i, h: (b, h, 0, 0, 0)),
            pl.BlockSpec((tq, iq_w), lambda b, i, h: (b * nq + i, 0)),
            pl.BlockSpec((tq, LANES), lambda b, i, h: (b * nq + i, 0)),
            pl.BlockSpec((seq, LANES), lambda b, i, h: (b, 0)),
            pl.BlockSpec((1, 2, LANES, LANES), lambda b, i, h: (h, 0, 0, 0)),
        ],
        out_specs=pl.BlockSpec((tq, hd), lambda b, i, h: (b * nq + i, h)),
        out_shape=jax.ShapeDtypeStruct((t, heads * hd), jnp.bfloat16),
        scratch_shapes=[
            pltpu.VMEM((2, seq, LANES), jnp.bfloat16),
            pltpu.VMEM((tq, iq_w), jnp.bfloat16),
            pltpu.VMEM((nq * nb + nb + 1, LANES, tq), jnp.float32),
            pltpu.VMEM((tq, tq), jnp.float32),
            pltpu.VMEM((tq, tq), jnp.float32),
            pltpu.VMEM((1, tq), jnp.float32),
            pltpu.VMEM((vrows, tq), jnp.float32),
        ],
        compiler_params=_cparams(("arbitrary", "arbitrary", "arbitrary")),
    )(q, g, k_hm, v_t, iq, iks, iks, bias_tiles)


def _rel_bucket_of(n):
    max_exact = REL_BUCKETS // 2
    nf = jnp.maximum(n, 1).astype(jnp.float32)
    large = max_exact + (jnp.log(nf / max_exact) / math.log(REL_MAX_DIST / max_exact)
                         * (REL_BUCKETS - max_exact)).astype(jnp.int32)
    large = jnp.minimum(large, REL_BUCKETS - 1)
    return jnp.where(n < max_exact, n, large)


def _bias_tiles(rel_bias):
    assert REL_MAX_DIST <= LANES
    n = LANES
    heads = rel_bias.shape[1]
    dist = jnp.arange(2 * n, dtype=jnp.int32)
    bucket = jnp.where(dist >= REL_MAX_DIST, REL_BUCKETS - 1, _rel_bucket_of(dist))
    bv = (rel_bias[bucket] - rel_bias[REL_BUCKETS - 1][None, :]) * LOG2E
    rows = jnp.concatenate([bv.T, jnp.zeros((heads, n), bv.dtype)], axis=1)
    flat = jnp.broadcast_to(rows[:, None, :], (heads, n, 3 * n)).reshape(heads, 3 * n * n)
    toep = flat[:, :n * (3 * n - 1)].reshape(heads, n, 3 * n - 1)[:, :, :2 * n]
    return jnp.stack([toep[:, :, :n], toep[:, :, n:]], axis=1).astype(jnp.float32)


def _ssd_kernel(z_ref, x_ref, bc_ref, dt_ref, shift_ref, cw_ref, cbb_ref, dtb_ref, alog_ref,
                dexp_ref, nw_ref, rexp_ref, o_ref, xe_ref, taps_ref, state_ref, y_ref,
                *, inner, groups):
    f32 = jnp.float32
    bf16 = jnp.bfloat16
    L = SSM_CHUNK
    P = SSM_HEAD_DIM
    N = SSM_STATE
    heads = inner // P
    hpg = heads // groups
    gw = hpg * P
    gn = groups * N
    halo = BF16_ROWS
    ext = halo + L
    dot = functools.partial(jnp.dot, preferred_element_type=f32)

    @pl.when(pl.program_id(1) == 0)
    def _():
        xe_ref[0:halo, :] = jnp.zeros((halo, xe_ref.shape[1]), bf16)
        taps_ref[SSM_CONV * ext:SSM_CONV * ext + halo, :] = cbb_ref[...]
        state_ref[...] = jnp.zeros(state_ref.shape, f32)

    xe_ref[halo:ext, 0:inner] = x_ref[...]
    xe_ref[halo:ext, inner:inner + 2 * gn] = bc_ref[...]
    xe = xe_ref[...]
    for j in range(SSM_CONV):
        taps_ref[j * ext:(j + 1) * ext, :] = xe * cw_ref[j:j + 1, :]
    xe_ref[0:halo, :] = xe_ref[L:ext, :]
    shift = shift_ref[...]
    bcm = _silu(dot(shift, taps_ref[:, inner:inner + 2 * gn]))

    dtr = dt_ref[...] + dtb_ref[...]
    dt = jnp.maximum(dtr, 0.0) + jnp.log1p(jnp.exp(-jnp.abs(dtr)))
    adt = dt * (-jnp.exp(alog_ref[...]))
    ri = lax.broadcasted_iota(jnp.int32, (L, L), 0)
    ci = lax.broadcasted_iota(jnp.int32, (L, L), 1)
    tri = ri >= ci
    cs = _dot_exact_lhs(jnp.where(tri, 1.0, 0.0).astype(bf16), adt)
    cst = cs.T
    csl = cs[L - 1:L, :]
    dt_hi = dt.astype(bf16)
    dt_lo = (dt - dt_hi.astype(f32)).astype(bf16)
    ecs_b = jnp.exp(cs).astype(bf16)
    dec_b = jnp.exp(csl - cs).astype(bf16)
    cdec = jnp.broadcast_to(jnp.exp(csl), (SUBLANES, LANES))
    lane = lax.broadcasted_iota(jnp.int32, (L, LANES), 1)

    ssq = jnp.zeros((L, 1), f32)
    for g in range(groups):
        sl = slice(g * gw, (g + 1) * gw)
        rex = rexp_ref[:, sl]
        xs = _silu(dot(shift, taps_ref[:, sl]))
        bg = bcm[:, g * N:(g + 1) * N]
        cg = bcm[:, gn + g * N:gn + (g + 1) * N].astype(bf16)
        xdt = xs * (dot(dt_hi, rex) + dot(dt_lo, rex))
        xdt_b = xdt.astype(bf16)
        xdec_b = (xdt * dot(dec_b, rex)).astype(bf16)
        gmat = lax.dot_general(cg, bg.astype(bf16), _NT, preferred_element_type=f32)
        bgt = bg.T.astype(bf16)
        st_prev = state_ref[g]
        y = dot(cg, st_prev.astype(bf16)) * dot(ecs_b, rex) + xs * dexp_ref[:, sl]
        state_ref[g] = st_prev * _dot_exact_rhs(cdec, rex)[0:1, :] + dot(bgt, xdec_b)
        diag = []
        for pr in range(gw // LANES):
            xp = xdt_b[:, pr * LANES:(pr + 1) * LANES]
            lhs, rhs = [], []
            for sub in range(LANES // P):
                hh = (g * gw + pr * LANES) // P + sub
                seg = cs[:, hh:hh + 1] - cst[hh:hh + 1, :]
                lm = jnp.exp(jnp.where(tri, seg, -jnp.inf))
                lhs.append((gmat * lm).astype(bf16))
                rhs.append(jnp.where((lane >= sub * P) & (lane < (sub + 1) * P), xp,
                                     jnp.zeros_like(xp)))
            diag.append(dot(jnp.concatenate(lhs, axis=1), jnp.concatenate(rhs, axis=0)))
        yg = (y + jnp.concatenate(diag, axis=1)) * _silu(z_ref[:, sl].astype(f32))
        ssq = ssq + jnp.sum(yg * yg, axis=-1, keepdims=True)
        y_ref[:, sl] = yg

    scale = lax.rsqrt(ssq * (1.0 / inner) + NORM_EPS)
    o_ref[...] = (y_ref[...] * scale * nw_ref[...]).astype(o_ref.dtype)


def _ssd(zxbc, dt, conv_w, conv_b, dt_bias, a_log, d_skip, norm_w, batch, seq, inner, groups):
    t = zxbc.shape[0]
    L = SSM_CHUNK
    heads = inner // SSM_HEAD_DIM
    gn = groups * SSM_STATE
    conv_ch = inner + 2 * gn
    assert heads <= LANES and LANES % SSM_HEAD_DIM == 0 and inner % (2 * gn) == 0
    nc = seq // L
    pad_h = LANES - heads
    dtb = jnp.pad(dt_bias, (0, pad_h)).reshape(1, LANES)
    alog = jnp.pad(a_log, (0, pad_h)).reshape(1, LANES)
    dexp = jnp.repeat(d_skip, SSM_HEAD_DIM).reshape(1, inner)
    rexp = (jnp.arange(LANES, dtype=jnp.int32)[:, None]
            == (jnp.arange(inner, dtype=jnp.int32) // SSM_HEAD_DIM)[None, :]).astype(jnp.bfloat16)
    bc_blk = inner * 2 // (2 * gn)
    halo = BF16_ROWS
    ext = halo + L
    rows = jnp.arange(L, dtype=jnp.int32)[:, None]
    cols = jnp.arange(SSM_CONV * ext + halo, dtype=jnp.int32)[None, :]
    blk, pos = cols // ext, cols % ext
    shift = jnp.where(blk < SSM_CONV, pos == halo + rows - (SSM_CONV - 1 - blk),
                      pos < 2).astype(jnp.bfloat16)
    cb_hi = conv_b.astype(jnp.bfloat16)
    cb_lo = (conv_b - cb_hi.astype(jnp.float32)).astype(jnp.bfloat16)
    cbb = jnp.zeros((halo, conv_ch), jnp.bfloat16).at[0].set(cb_hi).at[1].set(cb_lo)
    kern = functools.partial(_ssd_kernel, inner=inner, groups=groups)
    const = lambda b, c: (0, 0)
    return pl.pallas_call(
        kern,
        grid=(batch, nc),
        in_specs=[
            pl.BlockSpec((L, inner), lambda b, c: (b * nc + c, 0)),
            pl.BlockSpec((L, inner), lambda b, c: (b * nc + c, 1)),
            pl.BlockSpec((L, 2 * gn), lambda b, c: (b * nc + c, bc_blk)),
            pl.BlockSpec((L, LANES), lambda b, c: (b * nc + c, 0)),
            pl.BlockSpec((L, SSM_CONV * ext + halo), const),
            pl.BlockSpec((SSM_CONV, conv_ch), const),
            pl.BlockSpec((halo, conv_ch), const),
            pl.BlockSpec((1, LANES), const),
            pl.BlockSpec((1, LANES), const),
            pl.BlockSpec((1, inner), const),
            pl.BlockSpec((1, inner), const),
            pl.BlockSpec((LANES, inner), const),
        ],
        out_specs=pl.BlockSpec((L, inner), lambda b, c: (b * nc + c, 0)),
        out_shape=jax.ShapeDtypeStruct((t, inner), jnp.bfloat16),
        scratch_shapes=[
            pltpu.VMEM((ext, conv_ch), jnp.bfloat16),
            pltpu.VMEM((SSM_CONV * ext + halo, conv_ch), jnp.bfloat16),
            pltpu.VMEM((groups, SSM_STATE, inner // groups), jnp.float32),
            pltpu.VMEM((L, inner), jnp.float32),
        ],
        compiler_params=_cparams(("arbitrary", "arbitrary")),
    )(zxbc, zxbc, zxbc, dt, shift, conv_w.astype(jnp.bfloat16), cbb, dtb, alog, dexp,
      norm_w.reshape(1, inner), rexp)


def _proj_residual_final_kernel(a_ref, w_ref, x_ref, nw_ref, o_ref):
    x2 = x_ref[...] + jnp.dot(a_ref[...], w_ref[...], preferred_element_type=jnp.float32)
    ms = jnp.mean(x2 * x2, axis=-1, keepdims=True)
    o_ref[...] = (x2 * lax.rsqrt(ms + NORM_EPS) * nw_ref[...]).astype(o_ref.dtype)


def _proj_residual_final(a, w, x, nw, tm):
    t, k = a.shape
    d = w.shape[1]
    return pl.pallas_call(
        _proj_residual_final_kernel,
        grid=(t // tm,),
        in_specs=[pl.BlockSpec((tm, k), lambda i: (i, 0)),
                  pl.BlockSpec((k, d), lambda i: (0, 0)),
                  pl.BlockSpec((tm, d), lambda i: (i, 0)),
                  pl.BlockSpec((1, d), lambda i: (0, 0))],
        out_specs=pl.BlockSpec((tm, d), lambda i: (i, 0)),
        out_shape=jax.ShapeDtypeStruct((t, d), jnp.float32),
        compiler_params=_cparams(("parallel",)),
    )(a, w, x, nw.reshape(1, d))


def _row_tile(t, want):
    while t % want:
        want //= 2
    return want


def kernel(x, norm_w, a_w_in, a_w_out, rel_bias, b_w_in, b_conv_w, b_conv_b, b_dt_bias, b_a_log,
           b_d, b_norm_w, b_w_out, final_norm_w):
    batch, seq, d = x.shape
    t = batch * seq
    bf16 = jnp.bfloat16
    assert norm_w.shape[0] == 2 and a_w_in.shape[0] == 1 and b_w_in.shape[0] == 1
    xf = x.reshape(t, d)

    att_w = a_w_out.shape[1]
    heads = att_w // ATT_HEAD_DIM
    iq_w = IDX_HEADS * IDX_HEAD_DIM
    wa = jnp.swapaxes(a_w_in, 1, 2).reshape(a_w_in.shape[2], d)
    n_small = wa.shape[0] - 4 * att_w - iq_w
    assert n_small == IDX_HEAD_DIM + IDX_HEADS <= LANES
    w_iks = jnp.pad(wa[4 * att_w + iq_w:, :], ((0, LANES - n_small), (0, 0))).T.astype(bf16)

    tm = _row_tile(seq, 1024)
    tn = _row_tile(att_w, 1024)
    tq = _row_tile(seq, ATT_Q_BLOCK)
    h0 = _rmsnorm(xf, norm_w[0], bf16, _row_tile(t, 512))
    q = _proj(h0, wa, 0, att_w, bf16, tm, tn, scale=ATT_HEAD_DIM ** -0.5 * LOG2E)
    k_hm = _proj(h0, wa, att_w, att_w, bf16, tm, tn, layout="heads")
    v_t = _proj(h0, wa, 2 * att_w, att_w, bf16, tm, tn, layout="t", batch=batch, tq=tq)
    g = _proj(h0, wa, 3 * att_w, att_w, bf16, tm, tn)
    iq = _proj(h0, wa, 4 * att_w, iq_w, jnp.float32, tm, _row_tile(iq_w, 1024))
    iks = _matmul(h0, w_iks, jnp.float32, tm, LANES)
    att = _dsa_attention(q, g, k_hm, v_t, iq, iks, _bias_tiles(rel_bias), batch, seq, heads, tq)
    x1, h1 = _proj_residual_norm(att, a_w_out[0].astype(bf16), xf, norm_w[1], bf16,
                                 _row_tile(t, 256))

    inner = b_w_out.shape[1]
    ssm_heads = b_dt_bias.shape[1]
    conv_ch = b_conv_w.shape[2]
    groups = (conv_ch - inner) // (2 * SSM_STATE)
    wb = jnp.swapaxes(b_w_in, 1, 2).reshape(b_w_in.shape[2], d)
    w_dt = jnp.pad(wb[inner + conv_ch:, :], ((0, LANES - ssm_heads), (0, 0))).T.astype(bf16)
    zxbc = _proj(h1, wb, 0, inner + conv_ch, bf16, tm, _row_tile(inner + conv_ch, 1024))
    dt = _matmul(h1, w_dt, jnp.float32, tm, LANES)
    y = _ssd(zxbc, dt, b_conv_w[0], b_conv_b[0], b_dt_bias[0], b_a_log[0], b_d[0], b_norm_w[0],
             batch, seq, inner, groups)
    out = _proj_residual_final(y, b_w_out[0].astype(bf16), x1, final_norm_w, _row_tile(t, 256))
    return out.reshape(batch, seq, d)
```

```python
import functools
import math

import jax
import jax.numpy as jnp
from jax import lax
from jax.experimental import pallas as pl
from jax.experimental.pallas import tpu as pltpu

NORM_EPS = 1e-6

ATT_HEAD_DIM = 128
IDX_HEADS = 16
IDX_HEAD_DIM = 64
TOPK_MAX = 256
REL_BUCKETS = 32
REL_MAX_DIST = 128

SSM_HEAD_DIM = 64
SSM_STATE = 128
SSM_CONV = 4
SSM_CHUNK = 128

LANES = 128
SUBLANES = 8
BF16_ROWS = 16
VMEM_LIMIT_BYTES = 56 * 1024 * 1024

INT_MIN = -2 ** 31
NEG_BIG = -1e30
LOG2E = math.log2(math.e)
ATT_Q_BLOCK = 512

_NT = (((1,), (1,)), ((), ()))


def _cparams(sem, flags=None):
    return pltpu.CompilerParams(dimension_semantics=sem, vmem_limit_bytes=VMEM_LIMIT_BYTES,
                                flags=flags)


def _silu(x):
    h = 0.5 * x
    return h + h * jnp.tanh(h)


def _split3(x):
    hi = x.astype(jnp.bfloat16)
    r1 = x - hi.astype(jnp.float32)
    mid = r1.astype(jnp.bfloat16)
    lo = (r1 - mid.astype(jnp.float32)).astype(jnp.bfloat16)
    return hi, mid, lo


def _dot_exact_lhs(a01, x):
    hi, mid, lo = _split3(x)
    f = functools.partial(jnp.dot, preferred_element_type=jnp.float32)
    return f(a01, hi) + f(a01, mid) + f(a01, lo)


def _dot_exact_rhs(x, b01):
    hi, mid, lo = _split3(x)
    f = functools.partial(jnp.dot, preferred_element_type=jnp.float32)
    return f(hi, b01) + f(mid, b01) + f(lo, b01)


def _rmsnorm_kernel(x_ref, nw_ref, o_ref):
    x = x_ref[...]
    ms = jnp.mean(x * x, axis=-1, keepdims=True)
    o_ref[...] = (x * lax.rsqrt(ms + NORM_EPS) * nw_ref[...]).astype(o_ref.dtype)


def _rmsnorm(x, nw, out_dtype, tm):
    t, d = x.shape
    return pl.pallas_call(
        _rmsnorm_kernel,
        grid=(t // tm,),
        in_specs=[pl.BlockSpec((tm, d), lambda i: (i, 0)),
                  pl.BlockSpec((1, d), lambda i: (0, 0))],
        out_specs=pl.BlockSpec((tm, d), lambda i: (i, 0)),
        out_shape=jax.ShapeDtypeStruct((t, d), out_dtype),
        compiler_params=_cparams(("parallel",)),
    )(x, nw.reshape(1, d))


def _proj_kernel(a_ref, w_ref, o_ref, wb_ref, *, layout, scale, tq):
    @pl.when(pl.program_id(1) == 0)
    def _():
        w = w_ref[...]
        if scale != 1.0:
            w = w * scale
        wb_ref[...] = (w if layout == "t" else w.T).astype(wb_ref.dtype)

    f32 = jnp.float32
    if layout == "t":
        acc = lax.dot_general(wb_ref[...], a_ref[...], _NT, preferred_element_type=f32)
        hd = ATT_HEAD_DIM
        for hh in range(o_ref.shape[1]):
            for cc in range(o_ref.shape[2]):
                o_ref[0, hh, cc, 0:hd, :] = acc[hh * hd:(hh + 1) * hd,
                                                cc * tq:(cc + 1) * tq].astype(o_ref.dtype)
                o_ref[0, hh, cc, hd:hd + BF16_ROWS, :] = jnp.ones((BF16_ROWS, tq), o_ref.dtype)
    else:
        acc = jnp.dot(a_ref[...], wb_ref[...], preferred_element_type=f32)
        if layout == "heads":
            for j in range(o_ref.shape[0]):
                o_ref[j] = acc[:, j * LANES:(j + 1) * LANES].astype(o_ref.dtype)
        else:
            o_ref[...] = acc.astype(o_ref.dtype)


def _proj(a, wt, col_off, n, out_dtype, tm, tn, layout="rows", scale=1.0, batch=1, tq=LANES):
    t, d = a.shape
    assert col_off % tn == 0 and n % tn == 0 and t % tm == 0
    off = col_off // tn
    seq = t // batch
    mb = seq // tm
    if layout == "rows":
        out_spec = pl.BlockSpec((tm, tn), lambda j, i: (i, j))
        out_shape = (t, n)
    elif layout == "heads":
        out_spec = pl.BlockSpec((tn // LANES, tm, LANES), lambda j, i: (j, i, 0))
        out_shape = (n // LANES, t, LANES)
    else:
        rows = ATT_HEAD_DIM + BF16_ROWS
        out_spec = pl.BlockSpec((1, tn // ATT_HEAD_DIM, tm // tq, rows, tq),
                                lambda j, i: (i // mb, j, i % mb, 0, 0))
        out_shape = (batch, n // ATT_HEAD_DIM, seq // tq, rows, tq)
    wb_shape = (tn, d) if layout == "t" else (d, tn)
    return pl.pallas_call(
        functools.partial(_proj_kernel, layout=layout, scale=scale, tq=tq),
        grid=(n // tn, t // tm),
        in_specs=[pl.BlockSpec((tm, d), lambda j, i: (i, 0)),
                  pl.BlockSpec((tn, d), lambda j, i: (j + off, 0))],
        out_specs=out_spec,
        out_shape=jax.ShapeDtypeStruct(out_shape, out_dtype),
        scratch_shapes=[pltpu.VMEM(wb_shape, jnp.bfloat16)],
        compiler_params=_cparams(("parallel", "arbitrary")),
    )(a, wt)


def _proj_residual_norm_kernel(a_ref, w_ref, x_ref, nw_ref, xo_ref, ho_ref):
    x1 = x_ref[...] + jnp.dot(a_ref[...], w_ref[...], preferred_element_type=jnp.float32)
    xo_ref[...] = x1
    ms = jnp.mean(x1 * x1, axis=-1, keepdims=True)
    ho_ref[...] = (x1 * lax.rsqrt(ms + NORM_EPS) * nw_ref[...]).astype(ho_ref.dtype)


def _proj_residual_norm(a, w, x, nw, h_dtype, tm):
    t, k = a.shape
    d = w.shape[1]
    return pl.pallas_call(
        _proj_residual_norm_kernel,
        grid=(t // tm,),
        in_specs=[pl.BlockSpec((tm, k), lambda i: (i, 0)),
                  pl.BlockSpec((k, d), lambda i: (0, 0)),
                  pl.BlockSpec((tm, d), lambda i: (i, 0)),
                  pl.BlockSpec((1, d), lambda i: (0, 0))],
        out_specs=[pl.BlockSpec((tm, d), lambda i: (i, 0)),
                   pl.BlockSpec((tm, d), lambda i: (i, 0))],
        out_shape=[jax.ShapeDtypeStruct((t, d), jnp.float32),
                   jax.ShapeDtypeStruct((t, d), h_dtype)],
        compiler_params=_cparams(("parallel",)),
    )(a, w, x, nw.reshape(1, d))


def _dsa_kernel(q_ref, g_ref, k_ref, vt_ref, iq_ref, ikq_ref, ika_ref, bias_ref, o_ref,
                ikbd_ref, iqb_ref, madd_ref, s0_ref, s1_ref, m_ref, acc_ref,
                *, tq, topk, nq):
    qb = pl.program_id(1)
    h = pl.program_id(2)
    nck = qb + 1
    nb = tq // LANES
    base_d = nq * nb
    base_p = base_d + nb
    pairs = IDX_HEADS // 2
    f32 = jnp.float32
    bf16 = jnp.bfloat16
    i32 = jnp.int32
    hd = ATT_HEAD_DIM

    @pl.when((h == 0) & (qb == 0))
    def _():
        blk = ika_ref[...]
        lane = lax.broadcasted_iota(jnp.int32, blk.shape, 1)
        a = jnp.where(lane < IDX_HEAD_DIM, blk, 0.0)
        ikbd_ref[0] = a.astype(bf16)
        ikbd_ref[1] = pltpu.roll(a, IDX_HEAD_DIM, 1).astype(bf16)

    @pl.when(h == 0)
    def _():
        iqb_ref[...] = iq_ref[...].astype(bf16)
        iwt = ikq_ref[...].T[IDX_HEAD_DIM:IDX_HEAD_DIM + IDX_HEADS, :] * (
            IDX_HEADS ** -0.5 * IDX_HEAD_DIM ** -0.5)
        krow = lax.broadcasted_iota(jnp.int32, (tq, tq), 0)
        qcol = qb * tq + lax.broadcasted_iota(jnp.int32, (tq, tq), 1)

        def score_chunk(c, carry):
            start = pl.multiple_of(c * tq, tq)
            ka = ikbd_ref[0, pl.ds(start, tq), :]
            kb = ikbd_ref[1, pl.ds(start, tq), :]
            sc = jnp.zeros((tq, tq), f32)
            for j in range(pairs):
                rhs = iqb_ref[:, j * LANES:(j + 1) * LANES]
                d0 = lax.dot_general(ka, rhs, _NT, preferred_element_type=f32)
                d1 = lax.dot_general(kb, rhs, _NT, preferred_element_type=f32)
                sc = sc + jnp.maximum(d0, 0.0) * iwt[2 * j:2 * j + 1, :]
                sc = sc + jnp.maximum(d1, 0.0) * iwt[2 * j + 1:2 * j + 2, :]
            bits = lax.bitcast_convert_type(sc, i32)
            key = bits ^ ((bits >> 31) & i32(0x7FFFFFFF))
            key = jnp.where(c * tq + krow <= qcol, key, i32(INT_MIN))
            for a in range(nb):
                madd_ref[c * nb + a] = lax.bitcast_convert_type(
                    key[a * LANES:(a + 1) * LANES, :], f32)
            return carry

        lax.fori_loop(0, nck, score_chunk, 0)

        def keys_of(c, a):
            return lax.bitcast_convert_type(madd_ref[c * nb + a], i32)

        def bit_step(i, tau_u):
            cand_u = tau_u | lax.shift_left(i32(1), 31 - i)
            cand_s = cand_u ^ i32(INT_MIN)

            def count_chunk(c, part):
                for a in range(nb):
                    w = jnp.where(keys_of(c, a) >= cand_s, 1.0, 0.0)
                    part = part + jnp.sum(w.reshape(LANES // SUBLANES, SUBLANES, tq), axis=0)
                return part

            part = lax.fori_loop(0, nck, count_chunk, jnp.zeros((SUBLANES, tq), f32))
            cnt = jnp.sum(part, axis=0, keepdims=True)
            return jnp.where(cnt >= topk, cand_u, tau_u)

        tau_u = lax.fori_loop(0, 32, bit_step, jnp.zeros((1, tq), i32))
        tau = tau_u ^ i32(INT_MIN)

        def mask_chunk(c, carry):
            for a in range(nb):
                key = keys_of(c, a)
                sel = (key >= tau) & (key != i32(INT_MIN))
                madd_ref[c * nb + a] = jnp.where(sel, 0.0, NEG_BIG).astype(f32)
            return carry

        lax.fori_loop(0, nck, mask_chunk, 0)

    dtile = bias_ref[0, 0]
    ptile = bias_ref[0, 1]
    for a in range(nb):
        madd_ref[base_d + a] = madd_ref[qb * nb + a]
        madd_ref[base_d + a, :, a * LANES:(a + 1) * LANES] += dtile
        if a + 1 < nb:
            madd_ref[base_d + a, :, (a + 1) * LANES:(a + 2) * LANES] += ptile
    madd_ref[base_p] = madd_ref[jnp.maximum(qb - 1, 0) * nb + nb - 1]
    madd_ref[base_p, :, 0:LANES] += ptile

    m_ref[...] = jnp.full(m_ref.shape, NEG_BIG, f32)
    acc_ref[...] = jnp.zeros(acc_ref.shape, f32)
    q = q_ref[...]

    s_bufs = (s0_ref, s1_ref)

    def qk(c, par):
        kc = k_ref[0, pl.ds(pl.multiple_of(c * tq, tq), tq), :]
        s = lax.dot_general(kc, q, _NT, preferred_element_type=f32)
        is_diag = c == qb
        is_prev = c == qb - 1
        cmax = None
        for a in range(nb):
            sl = jnp.where(is_diag, base_d + a, c * nb + a)
            if a == nb - 1:
                sl = jnp.where(is_prev, base_p, sl)
            sa = s[a * LANES:(a + 1) * LANES, :] + madd_ref[sl]
            s_bufs[par][a * LANES:(a + 1) * LANES, :] = sa
            sm = jnp.max(sa.reshape(LANES // SUBLANES, SUBLANES, tq), axis=0)
            cmax = sm if cmax is None else jnp.maximum(cmax, sm)
        return jnp.max(cmax, axis=0, keepdims=True)

    def softmax_pv(c, par, cmax):
        m_prev = m_ref[...]
        m_new = jnp.maximum(m_prev, cmax)
        m_ref[...] = m_new
        p = jnp.exp2(s_bufs[par][...] - m_new).astype(bf16)
        acc_ref[...] = jnp.exp2(m_prev - m_new) * acc_ref[...] + jnp.dot(
            vt_ref[0, 0, c], p, preferred_element_type=f32)

    def pair(i, cm):
        c = 2 * i
        cm1 = qk(c + 1, 1)
        softmax_pv(c, 0, cm)
        cm2 = qk(c + 2, 0)
        softmax_pv(c + 1, 1, cm1)
        return cm2

    npairs = (nck - 1) // 2
    cm = lax.fori_loop(0, npairs, pair, qk(0, 0))
    c0 = 2 * npairs

    @pl.when(nck - c0 == 1)
    def _():
        softmax_pv(c0, 0, cm)

    @pl.when(nck - c0 == 2)
    def _():
        cm1 = qk(c0 + 1, 1)
        softmax_pv(c0, 0, cm)
        softmax_pv(c0 + 1, 1, cm1)

    out_t = acc_ref[0:hd, :] / acc_ref[hd:hd + 1, :]
    o_ref[...] = (out_t.T * _silu(g_ref[...].astype(f32))).astype(o_ref.dtype)


def _dsa_attention(q, g, k_hm, v_t, iq, iks, bias_tiles, batch, seq, heads, tq):
    t = q.shape[0]
    nq = seq // tq
    nb = tq // LANES
    topk = min(TOPK_MAX, seq // 4)
    iq_w = IDX_HEADS * IDX_HEAD_DIM
    hd = ATT_HEAD_DIM
    vrows = hd + BF16_ROWS
    kern = functools.partial(_dsa_kernel, tq=tq, topk=topk, nq=nq)
    return pl.pallas_call(
        kern,
        grid=(batch, nq, heads),
        in_specs=[
            pl.BlockSpec((tq, hd), lambda b, i, h: (b * nq + i, h)),
            pl.BlockSpec((tq, hd), lambda b, i, h: (b * nq + i, h)),
            pl.BlockSpec((1, seq, hd), lambda b, i, h: (h, b, 0)),
            pl.BlockSpec((1, 1, nq, vrows, tq), lambda b, i, h: (b, h, 0, 0, 0)),
            pl.BlockSpec((tq, iq_w), lambda b, i, h: (b * nq + i, 0)),
            pl.BlockSpec((tq, LANES), lambda b, i, h: (b * nq + i, 0)),
            pl.BlockSpec((seq, LANES), lambda b, i, h: (b, 0)),
            pl.BlockSpec((1, 2, LANES, LANES), lambda b, i, h: (h, 0, 0, 0)),
        ],
        out_specs=pl.BlockSpec((tq, hd), lambda b, i, h: (b * nq + i, h)),
        out_shape=jax.ShapeDtypeStruct((t, heads * hd), jnp.bfloat16),
        scratch_shapes=[
            pltpu.VMEM((2, seq, LANES), jnp.bfloat16),
            pltpu.VMEM((tq, iq_w), jnp.bfloat16),
            pltpu.VMEM((nq * nb + nb + 1, LANES, tq), jnp.float32),
            pltpu.VMEM((tq, tq), jnp.float32),
            pltpu.VMEM((tq, tq), jnp.float32),
            pltpu.VMEM((1, tq), jnp.float32),
            pltpu.VMEM((vrows, tq), jnp.float32),
        ],
        compiler_params=_cparams(("arbitrary", "arbitrary", "arbitrary")),
    )(q, g, k_hm, v_t, iq, iks, iks, bias_tiles)


def _rel_bucket_of(n):
    max_exact = REL_BUCKETS // 2
    nf = jnp.maximum(n, 1).astype(jnp.float32)
    large = max_exact + (jnp.log(nf / max_exact) / math.log(REL_MAX_DIST / max_exact)
                         * (REL_BUCKETS - max_exact)).astype(jnp.int32)
    large = jnp.minimum(large, REL_BUCKETS - 1)
    return jnp.where(n < max_exact, n, large)


def _bias_tiles(rel_bias):
    assert REL_MAX_DIST <= LANES
    n = LANES
    heads = rel_bias.shape[1]
    dist = jnp.arange(2 * n, dtype=jnp.int32)
    bucket = jnp.where(dist >= REL_MAX_DIST, REL_BUCKETS - 1, _rel_bucket_of(dist))
    bv = (rel_bias[bucket] - rel_bias[REL_BUCKETS - 1][None, :]) * LOG2E
    rows = jnp.concatenate([bv.T, jnp.zeros((heads, n), bv.dtype)], axis=1)
    flat = jnp.broadcast_to(rows[:, None, :], (heads, n, 3 * n)).reshape(heads, 3 * n * n)
    toep = flat[:, :n * (3 * n - 1)].reshape(heads, n, 3 * n - 1)[:, :, :2 * n]
    return jnp.stack([toep[:, :, :n], toep[:, :, n:]], axis=1).astype(jnp.float32)


def _ssd_kernel(z_ref, x_ref, bc_ref, dt_ref, shift_ref, cw_ref, cbb_ref, dtb_ref, alog_ref,
                dexp_ref, nw_ref, rexp_ref, o_ref, xe_ref, taps_ref, state_ref, y_ref,
                *, inner, groups):
    f32 = jnp.float32
    bf16 = jnp.bfloat16
    L = SSM_CHUNK
    P = SSM_HEAD_DIM
    N = SSM_STATE
    heads = inner // P
    hpg = heads // groups
    gw = hpg * P
    gn = groups * N
    halo = BF16_ROWS
    ext = halo + L
    dot = functools.partial(jnp.dot, preferred_element_type=f32)

    @pl.when(pl.program_id(1) == 0)
    def _():
        xe_ref[0:halo, :] = jnp.zeros((halo, xe_ref.shape[1]), bf16)
        taps_ref[SSM_CONV * ext:SSM_CONV * ext + halo, :] = cbb_ref[...]
        state_ref[...] = jnp.zeros(state_ref.shape, f32)

    xe_ref[halo:ext, 0:inner] = x_ref[...]
    xe_ref[halo:ext, inner:inner + 2 * gn] = bc_ref[...]
    xe = xe_ref[...]
    for j in range(SSM_CONV):
        taps_ref[j * ext:(j + 1) * ext, :] = xe * cw_ref[j:j + 1, :]
    xe_ref[0:halo, :] = xe_ref[L:ext, :]
    shift = shift_ref[...]
    bcm = _silu(dot(shift, taps_ref[:, inner:inner + 2 * gn]))

    dtr = dt_ref[...] + dtb_ref[...]
    dt = jnp.maximum(dtr, 0.0) + jnp.log1p(jnp.exp(-jnp.abs(dtr)))
    adt = dt * (-jnp.exp(alog_ref[...]))
    ri = lax.broadcasted_iota(jnp.int32, (L, L), 0)
    ci = lax.broadcasted_iota(jnp.int32, (L, L), 1)
    tri = ri >= ci
    cs = _dot_exact_lhs(jnp.where(tri, 1.0, 0.0).astype(bf16), adt)
    cst = cs.T
    csl = cs[L - 1:L, :]
    dt_hi = dt.astype(bf16)
    dt_lo = (dt - dt_hi.astype(f32)).astype(bf16)
    ecs_b = jnp.exp(cs).astype(bf16)
    dec_b = jnp.exp(csl - cs).astype(bf16)
    cdec = jnp.broadcast_to(jnp.exp(csl), (SUBLANES, LANES))
    lane = lax.broadcasted_iota(jnp.int32, (L, LANES), 1)

    ssq = jnp.zeros((L, 1), f32)
    for g in range(groups):
        sl = slice(g * gw, (g + 1) * gw)
        rex = rexp_ref[:, sl]
        xs = _silu(dot(shift, taps_ref[:, sl]))
        bg = bcm[:, g * N:(g + 1) * N]
        cg = bcm[:, gn + g * N:gn + (g + 1) * N].astype(bf16)
        xdt = xs * (dot(dt_hi, rex) + dot(dt_lo, rex))
        xdt_b = xdt.astype(bf16)
        xdec_b = (xdt * dot(dec_b, rex)).astype(bf16)
        gmat = lax.dot_general(cg, bg.astype(bf16), _NT, preferred_element_type=f32)
        bgt = bg.T.astype(bf16)
        st_prev = state_ref[g]
        y = dot(cg, st_prev.astype(bf16)) * dot(ecs_b, rex) + xs * dexp_ref[:, sl]
        state_ref[g] = st_prev * _dot_exact_rhs(cdec, rex)[0:1, :] + dot(bgt, xdec_b)
        diag = []
        for pr in range(gw // LANES):
            xp = xdt_b[:, pr * LANES:(pr + 1) * LANES]
            lhs, rhs = [], []
            for sub in range(LANES // P):
                hh = (g * gw + pr * LANES) // P + sub
                seg = cs[:, hh:hh + 1] - cst[hh:hh + 1, :]
                lm = jnp.exp(jnp.where(tri, seg, -jnp.inf))
                lhs.append((gmat * lm).astype(bf16))
                rhs.append(jnp.where((lane >= sub * P) & (lane < (sub + 1) * P), xp,
                                     jnp.zeros_like(xp)))
            diag.append(dot(jnp.concatenate(lhs, axis=1), jnp.concatenate(rhs, axis=0)))
        yg = (y + jnp.concatenate(diag, axis=1)) * _silu(z_ref[:, sl].astype(f32))
        ssq = ssq + jnp.sum(yg * yg, axis=-1, keepdims=True)
        y_ref[:, sl] = yg

    scale = lax.rsqrt(ssq * (1.0 / inner) + NORM_EPS)
    o_ref[...] = (y_ref[...] * scale * nw_ref[...]).astype(o_ref.dtype)


def _ssd(zxbc, dt, conv_w, conv_b, dt_bias, a_log, d_skip, norm_w, batch, seq, inner, groups):
    t = zxbc.shape[0]
    L = SSM_CHUNK
    heads = inner // SSM_HEAD_DIM
    gn = groups * SSM_STATE
    conv_ch = inner + 2 * gn
    assert heads <= LANES and LANES % SSM_HEAD_DIM == 0 and inner % (2 * gn) == 0
    nc = seq // L
    pad_h = LANES - heads
    dtb = jnp.pad(dt_bias, (0, pad_h)).reshape(1, LANES)
    alog = jnp.pad(a_log, (0, pad_h)).reshape(1, LANES)
    dexp = jnp.repeat(d_skip, SSM_HEAD_DIM).reshape(1, inner)
    rexp = (jnp.arange(LANES, dtype=jnp.int32)[:, None]
            == (jnp.arange(inner, dtype=jnp.int32) // SSM_HEAD_DIM)[None, :]).astype(jnp.bfloat16)
    bc_blk = inner * 2 // (2 * gn)
    halo = BF16_ROWS
    ext = halo + L
    rows = jnp.arange(L, dtype=jnp.int32)[:, None]
    cols = jnp.arange(SSM_CONV * ext + halo, dtype=jnp.int32)[None, :]
    blk, pos = cols // ext, cols % ext
    shift = jnp.where(blk < SSM_CONV, pos == halo + rows - (SSM_CONV - 1 - blk),
                      pos < 2).astype(jnp.bfloat16)
    cb_hi = conv_b.astype(jnp.bfloat16)
    cb_lo = (conv_b - cb_hi.astype(jnp.float32)).astype(jnp.bfloat16)
    cbb = jnp.zeros((halo, conv_ch), jnp.bfloat16).at[0].set(cb_hi).at[1].set(cb_lo)
    kern = functools.partial(_ssd_kernel, inner=inner, groups=groups)
    const = lambda b, c: (0, 0)
    return pl.pallas_call(
        kern,
        grid=(batch, nc),
        in_specs=[
            pl.BlockSpec((L, inner), lambda b, c: (b * nc + c, 0)),
            pl.BlockSpec((L, inner), lambda b, c: (b * nc + c, 1)),
            pl.BlockSpec((L, 2 * gn), lambda b, c: (b * nc + c, bc_blk)),
            pl.BlockSpec((L, LANES), lambda b, c: (b * nc + c, 0)),
            pl.BlockSpec((L, SSM_CONV * ext + halo), const),
            pl.BlockSpec((SSM_CONV, conv_ch), const),
            pl.BlockSpec((halo, conv_ch), const),
            pl.BlockSpec((1, LANES), const),
            pl.BlockSpec((1, LANES), const),
            pl.BlockSpec((1, inner), const),
            pl.BlockSpec((1, inner), const),
            pl.BlockSpec((LANES, inner), const),
        ],
        out_specs=pl.BlockSpec((L, inner), lambda b, c: (b * nc + c, 0)),
        out_shape=jax.ShapeDtypeStruct((t, inner), jnp.bfloat16),
        scratch_shapes=[
            pltpu.VMEM((ext, conv_ch), jnp.bfloat16),
            pltpu.VMEM((SSM_CONV * ext + halo, conv_ch), jnp.bfloat16),
            pltpu.VMEM((groups, SSM_STATE, inner // groups), jnp.float32),
            pltpu.VMEM((L, inner), jnp.float32),
        ],
        compiler_params=_cparams(("arbitrary", "arbitrary")),
    )(zxbc, zxbc, zxbc, dt, shift, conv_w.astype(jnp.bfloat16), cbb, dtb, alog, dexp,
      norm_w.reshape(1, inner), rexp)


def _proj_residual_final_kernel(a_ref, w_ref, x_ref, nw_ref, o_ref):
    x2 = x_ref[...] + jnp.dot(a_ref[...], w_ref[...], preferred_element_type=jnp.float32)
    ms = jnp.mean(x2 * x2, axis=-1, keepdims=True)
    o_ref[...] = (x2 * lax.rsqrt(ms + NORM_EPS) * nw_ref[...]).astype(o_ref.dtype)


def _proj_residual_final(a, w, x, nw, tm):
    t, k = a.shape
    d = w.shape[1]
    return pl.pallas_call(
        _proj_residual_final_kernel,
        grid=(t // tm,),
        in_specs=[pl.BlockSpec((tm, k), lambda i: (i, 0)),
                  pl.BlockSpec((k, d), lambda i: (0, 0)),
                  pl.BlockSpec((tm, d), lambda i: (i, 0)),
                  pl.BlockSpec((1, d), lambda i: (0, 0))],
        out_specs=pl.BlockSpec((tm, d), lambda i: (i, 0)),
        out_shape=jax.ShapeDtypeStruct((t, d), jnp.float32),
        compiler_params=_cparams(("parallel",)),
    )(a, w, x, nw.reshape(1, d))


def _row_tile(t, want):
    while t % want:
        want //= 2
    return want


def kernel(x, norm_w, a_w_in, a_w_out, rel_bias, b_w_in, b_conv_w, b_conv_b, b_dt_bias, b_a_log,
           b_d, b_norm_w, b_w_out, final_norm_w):
    batch, seq, d = x.shape
    t = batch * seq
    bf16 = jnp.bfloat16
    assert norm_w.shape[0] == 2 and a_w_in.shape[0] == 1 and b_w_in.shape[0] == 1
    xf = x.reshape(t, d)

    att_w = a_w_out.shape[1]
    heads = att_w // ATT_HEAD_DIM
    iq_w = IDX_HEADS * IDX_HEAD_DIM
    wa = jnp.swapaxes(a_w_in, 1, 2).reshape(a_w_in.shape[2], d)
    n_small = wa.shape[0] - 4 * att_w - iq_w
    assert n_small == IDX_HEAD_DIM + IDX_HEADS <= LANES
    w_iks = jnp.pad(wa[4 * att_w + iq_w:, :], ((0, LANES - n_small), (0, 0)))

    tm = _row_tile(seq, 1024)
    tn = _row_tile(att_w, 1024)
    tq = _row_tile(seq, ATT_Q_BLOCK)
    h0 = _rmsnorm(xf, norm_w[0], bf16, _row_tile(t, 512))
    q = _proj(h0, wa, 0, att_w, bf16, tm, tn, scale=ATT_HEAD_DIM ** -0.5 * LOG2E)
    k_hm = _proj(h0, wa, att_w, att_w, bf16, tm, tn, layout="heads")
    v_t = _proj(h0, wa, 2 * att_w, att_w, bf16, tm, tn, layout="t", batch=batch, tq=tq)
    g = _proj(h0, wa, 3 * att_w, att_w, bf16, tm, tn)
    iq = _proj(h0, wa, 4 * att_w, iq_w, jnp.float32, tm, _row_tile(iq_w, 1024))
    iks = _proj(h0, w_iks, 0, LANES, jnp.float32, tm, LANES)
    att = _dsa_attention(q, g, k_hm, v_t, iq, iks, _bias_tiles(rel_bias), batch, seq, heads, tq)
    x1, h1 = _proj_residual_norm(att, a_w_out[0].astype(bf16), xf, norm_w[1], bf16,
                                 _row_tile(t, 256))

    inner = b_w_out.shape[1]
    ssm_heads = b_dt_bias.shape[1]
    conv_ch = b_conv_w.shape[2]
    groups = (conv_ch - inner) // (2 * SSM_STATE)
    wb = jnp.swapaxes(b_w_in, 1, 2).reshape(b_w_in.shape[2], d)
    w_dt = jnp.pad(wb[inner + conv_ch:, :], ((0, LANES - ssm_heads), (0, 0)))
    zxbc = _proj(h1, wb, 0, inner + conv_ch, bf16, tm, _row_tile(inner + conv_ch, 1024))
    dt = _proj(h1, w_dt, 0, LANES, jnp.float32, tm, LANES)
    y = _ssd(zxbc, dt, b_conv_w[0], b_conv_b[0], b_dt_bias[0], b_a_log[0], b_d[0], b_norm_w[0],
             batch, seq, inner, groups)
    out = _proj_residual_final(y, b_w_out[0].astype(bf16), x1, final_norm_w, _row_tile(t, 256))
    return out.reshape(batch, seq, d)
```

```python
import functools
import math

import jax
import jax.numpy as jnp
from jax import lax
from jax.experimental import pallas as pl
from jax.experimental.pallas import tpu as pltpu

NORM_EPS = 1e-6

ATT_HEAD_DIM = 128
IDX_HEADS = 16
IDX_HEAD_DIM = 64
TOPK_MAX = 256
REL_BUCKETS = 32
REL_MAX_DIST = 128

SSM_HEAD_DIM = 64
SSM_STATE = 128
SSM_CONV = 4
SSM_CHUNK = 128

LANES = 128
SUBLANES = 8
BF16_ROWS = 16
VMEM_LIMIT_BYTES = 56 * 1024 * 1024

INT_MIN = -2 ** 31
NEG_BIG = -1e30
LOG2E = math.log2(math.e)
ATT_Q_BLOCK = 512
ATT_LOOP_CHUNKS = 4

_NT = (((1,), (1,)), ((), ()))


def _cparams(sem, flags=None):
    return pltpu.CompilerParams(dimension_semantics=sem, vmem_limit_bytes=VMEM_LIMIT_BYTES,
                                flags=flags)


def _silu(x):
    h = 0.5 * x
    return h + h * jnp.tanh(h)


def _split3(x):
    hi = x.astype(jnp.bfloat16)
    r1 = x - hi.astype(jnp.float32)
    mid = r1.astype(jnp.bfloat16)
    lo = (r1 - mid.astype(jnp.float32)).astype(jnp.bfloat16)
    return hi, mid, lo


def _dot_exact_lhs(a01, x):
    hi, mid, lo = _split3(x)
    f = functools.partial(jnp.dot, preferred_element_type=jnp.float32)
    return f(a01, hi) + f(a01, mid) + f(a01, lo)


def _dot_exact_rhs(x, b01):
    hi, mid, lo = _split3(x)
    f = functools.partial(jnp.dot, preferred_element_type=jnp.float32)
    return f(hi, b01) + f(mid, b01) + f(lo, b01)


def _rmsnorm_kernel(x_ref, nw_ref, o_ref):
    x = x_ref[...]
    ms = jnp.mean(x * x, axis=-1, keepdims=True)
    o_ref[...] = (x * lax.rsqrt(ms + NORM_EPS) * nw_ref[...]).astype(o_ref.dtype)


def _rmsnorm(x, nw, out_dtype, tm):
    t, d = x.shape
    return pl.pallas_call(
        _rmsnorm_kernel,
        grid=(t // tm,),
        in_specs=[pl.BlockSpec((tm, d), lambda i: (i, 0)),
                  pl.BlockSpec((1, d), lambda i: (0, 0))],
        out_specs=pl.BlockSpec((tm, d), lambda i: (i, 0)),
        out_shape=jax.ShapeDtypeStruct((t, d), out_dtype),
        compiler_params=_cparams(("parallel",)),
    )(x, nw.reshape(1, d))


def _proj_kernel(a_ref, w_ref, o_ref, wb_ref, *, layout, scale, tq):
    @pl.when(pl.program_id(1) == 0)
    def _():
        w = w_ref[...]
        if scale != 1.0:
            w = w * scale
        wb_ref[...] = (w if layout == "t" else w.T).astype(wb_ref.dtype)

    f32 = jnp.float32
    if layout == "t":
        acc = lax.dot_general(wb_ref[...], a_ref[...], _NT, preferred_element_type=f32)
        hd = ATT_HEAD_DIM
        for hh in range(o_ref.shape[1]):
            for cc in range(o_ref.shape[2]):
                o_ref[0, hh, cc, 0:hd, :] = acc[hh * hd:(hh + 1) * hd,
                                                cc * tq:(cc + 1) * tq].astype(o_ref.dtype)
                o_ref[0, hh, cc, hd:hd + BF16_ROWS, :] = jnp.ones((BF16_ROWS, tq), o_ref.dtype)
    else:
        acc = jnp.dot(a_ref[...], wb_ref[...], preferred_element_type=f32)
        if layout == "heads":
            for j in range(o_ref.shape[0]):
                o_ref[j] = acc[:, j * LANES:(j + 1) * LANES].astype(o_ref.dtype)
        else:
            o_ref[...] = acc.astype(o_ref.dtype)


def _proj(a, wt, col_off, n, out_dtype, tm, tn, layout="rows", scale=1.0, batch=1, tq=LANES):
    t, d = a.shape
    assert col_off % tn == 0 and n % tn == 0 and t % tm == 0
    off = col_off // tn
    seq = t // batch
    mb = seq // tm
    if layout == "rows":
        out_spec = pl.BlockSpec((tm, tn), lambda j, i: (i, j))
        out_shape = (t, n)
    elif layout == "heads":
        out_spec = pl.BlockSpec((tn // LANES, tm, LANES), lambda j, i: (j, i, 0))
        out_shape = (n // LANES, t, LANES)
    else:
        rows = ATT_HEAD_DIM + BF16_ROWS
        out_spec = pl.BlockSpec((1, tn // ATT_HEAD_DIM, tm // tq, rows, tq),
                                lambda j, i: (i // mb, j, i % mb, 0, 0))
        out_shape = (batch, n // ATT_HEAD_DIM, seq // tq, rows, tq)
    wb_shape = (tn, d) if layout == "t" else (d, tn)
    return pl.pallas_call(
        functools.partial(_proj_kernel, layout=layout, scale=scale, tq=tq),
        grid=(n // tn, t // tm),
        in_specs=[pl.BlockSpec((tm, d), lambda j, i: (i, 0)),
                  pl.BlockSpec((tn, d), lambda j, i: (j + off, 0))],
        out_specs=out_spec,
        out_shape=jax.ShapeDtypeStruct(out_shape, out_dtype),
        scratch_shapes=[pltpu.VMEM(wb_shape, jnp.bfloat16)],
        compiler_params=_cparams(("parallel", "arbitrary")),
    )(a, wt)


def _proj_residual_norm_kernel(a_ref, w_ref, x_ref, nw_ref, xo_ref, ho_ref):
    x1 = x_ref[...] + jnp.dot(a_ref[...], w_ref[...], preferred_element_type=jnp.float32)
    xo_ref[...] = x1
    ms = jnp.mean(x1 * x1, axis=-1, keepdims=True)
    ho_ref[...] = (x1 * lax.rsqrt(ms + NORM_EPS) * nw_ref[...]).astype(ho_ref.dtype)


def _proj_residual_norm(a, w, x, nw, h_dtype, tm):
    t, k = a.shape
    d = w.shape[1]
    return pl.pallas_call(
        _proj_residual_norm_kernel,
        grid=(t // tm,),
        in_specs=[pl.BlockSpec((tm, k), lambda i: (i, 0)),
                  pl.BlockSpec((k, d), lambda i: (0, 0)),
                  pl.BlockSpec((tm, d), lambda i: (i, 0)),
                  pl.BlockSpec((1, d), lambda i: (0, 0))],
        out_specs=[pl.BlockSpec((tm, d), lambda i: (i, 0)),
                   pl.BlockSpec((tm, d), lambda i: (i, 0))],
        out_shape=[jax.ShapeDtypeStruct((t, d), jnp.float32),
                   jax.ShapeDtypeStruct((t, d), h_dtype)],
        compiler_params=_cparams(("parallel",)),
    )(a, w, x, nw.reshape(1, d))


def _dsa_kernel(q_ref, g_ref, k_ref, vt_ref, iq_ref, ikq_ref, ika_ref, bias_ref, o_ref,
                ikbd_ref, iqb_ref, madd_ref, s0_ref, s1_ref, m_ref, acc_ref,
                *, tq, topk, nq):
    qb = pl.program_id(1)
    h = pl.program_id(2)
    nck = qb + 1
    nb = tq // LANES
    base_d = nq * nb
    base_p = base_d + nb
    pairs = IDX_HEADS // 2
    f32 = jnp.float32
    bf16 = jnp.bfloat16
    i32 = jnp.int32
    hd = ATT_HEAD_DIM

    @pl.when((h == 0) & (qb == 0))
    def _():
        blk = ika_ref[...]
        lane = lax.broadcasted_iota(jnp.int32, blk.shape, 1)
        a = jnp.where(lane < IDX_HEAD_DIM, blk, 0.0)
        ikbd_ref[0] = a.astype(bf16)
        ikbd_ref[1] = pltpu.roll(a, IDX_HEAD_DIM, 1).astype(bf16)

    @pl.when(h == 0)
    def _():
        iqb_ref[...] = iq_ref[...].astype(bf16)
        iwt = ikq_ref[...].T[IDX_HEAD_DIM:IDX_HEAD_DIM + IDX_HEADS, :] * (
            IDX_HEADS ** -0.5 * IDX_HEAD_DIM ** -0.5)
        krow = lax.broadcasted_iota(jnp.int32, (tq, tq), 0)
        qcol = qb * tq + lax.broadcasted_iota(jnp.int32, (tq, tq), 1)

        def score_chunk(c, carry):
            start = pl.multiple_of(c * tq, tq)
            ka = ikbd_ref[0, pl.ds(start, tq), :]
            kb = ikbd_ref[1, pl.ds(start, tq), :]
            sc = jnp.zeros((tq, tq), f32)
            for j in range(pairs):
                rhs = iqb_ref[:, j * LANES:(j + 1) * LANES]
                d0 = lax.dot_general(ka, rhs, _NT, preferred_element_type=f32)
                d1 = lax.dot_general(kb, rhs, _NT, preferred_element_type=f32)
                sc = sc + jnp.maximum(d0, 0.0) * iwt[2 * j:2 * j + 1, :]
                sc = sc + jnp.maximum(d1, 0.0) * iwt[2 * j + 1:2 * j + 2, :]
            bits = lax.bitcast_convert_type(sc, i32)
            key = bits ^ ((bits >> 31) & i32(0x7FFFFFFF))
            key = jnp.where(c * tq + krow <= qcol, key, i32(INT_MIN))
            for a in range(nb):
                madd_ref[c * nb + a] = lax.bitcast_convert_type(
                    key[a * LANES:(a + 1) * LANES, :], f32)
            return carry

        lax.fori_loop(0, nck, score_chunk, 0)

        def keys_of(c, a):
            return lax.bitcast_convert_type(madd_ref[c * nb + a], i32)

        def bit_step(i, tau_u):
            cand_u = tau_u | lax.shift_left(i32(1), 31 - i)
            cand_s = cand_u ^ i32(INT_MIN)

            def count_chunk(c, part):
                for a in range(nb):
                    w = jnp.where(keys_of(c, a) >= cand_s, 1.0, 0.0)
                    part = part + jnp.sum(w.reshape(LANES // SUBLANES, SUBLANES, tq), axis=0)
                return part

            part = lax.fori_loop(0, nck, count_chunk, jnp.zeros((SUBLANES, tq), f32))
            cnt = jnp.sum(part, axis=0, keepdims=True)
            return jnp.where(cnt >= topk, cand_u, tau_u)

        tau_u = lax.fori_loop(0, 32, bit_step, jnp.zeros((1, tq), i32))
        tau = tau_u ^ i32(INT_MIN)

        def mask_chunk(c, carry):
            for a in range(nb):
                key = keys_of(c, a)
                sel = (key >= tau) & (key != i32(INT_MIN))
                madd_ref[c * nb + a] = jnp.where(sel, 0.0, NEG_BIG).astype(f32)
            return carry

        lax.fori_loop(0, nck, mask_chunk, 0)

    dtile = bias_ref[0, 0]
    ptile = bias_ref[0, 1]
    for a in range(nb):
        madd_ref[base_d + a] = madd_ref[qb * nb + a]
        madd_ref[base_d + a, :, a * LANES:(a + 1) * LANES] += dtile
        if a + 1 < nb:
            madd_ref[base_d + a, :, (a + 1) * LANES:(a + 2) * LANES] += ptile
    madd_ref[base_p] = madd_ref[jnp.maximum(qb - 1, 0) * nb + nb - 1]
    madd_ref[base_p, :, 0:LANES] += ptile

    m_ref[...] = jnp.full(m_ref.shape, NEG_BIG, f32)
    acc_ref[...] = jnp.zeros(acc_ref.shape, f32)
    q = q_ref[...]

    s_bufs = (s0_ref, s1_ref)

    def qk(c, par):
        kc = k_ref[0, pl.ds(pl.multiple_of(c * tq, tq), tq), :]
        s = lax.dot_general(kc, q, _NT, preferred_element_type=f32)
        is_diag = c == qb
        is_prev = c == qb - 1
        cmax = None
        for a in range(nb):
            sl = jnp.where(is_diag, base_d + a, c * nb + a)
            if a == nb - 1:
                sl = jnp.where(is_prev, base_p, sl)
            sa = s[a * LANES:(a + 1) * LANES, :] + madd_ref[sl]
            s_bufs[par][a * LANES:(a + 1) * LANES, :] = sa
            sm = jnp.max(sa.reshape(LANES // SUBLANES, SUBLANES, tq), axis=0)
            cmax = sm if cmax is None else jnp.maximum(cmax, sm)
        return jnp.max(cmax, axis=0, keepdims=True)

    def softmax_pv(c, par, cmax):
        m_prev = m_ref[...]
        m_new = jnp.maximum(m_prev, cmax)
        m_ref[...] = m_new
        p = jnp.exp2(s_bufs[par][...] - m_new).astype(bf16)
        acc_ref[...] = jnp.exp2(m_prev - m_new) * acc_ref[...] + jnp.dot(
            vt_ref[0, 0, c], p, preferred_element_type=f32)

    def run(c, n, cm, feed_next):
        for j in range(n):
            cm_next = qk(c + j + 1, (j + 1) & 1) if (j + 1 < n or feed_next) else None
            softmax_pv(c + j, j & 1, cm)
            cm = cm_next
        return cm

    unroll = ATT_LOOP_CHUNKS
    nloops = (nck - 1) // unroll
    cm = lax.fori_loop(0, nloops, lambda i, cm: run(unroll * i, unroll, cm, True), qk(0, 0))
    c0 = unroll * nloops
    for rem in range(1, unroll + 1):
        pl.when(nck - c0 == rem)(functools.partial(run, c0, rem, cm, False))

    out_t = acc_ref[0:hd, :] / acc_ref[hd:hd + 1, :]
    o_ref[...] = (out_t.T * _silu(g_ref[...].astype(f32))).astype(o_ref.dtype)


def _dsa_attention(q, g, k_hm, v_t, idx, bias_tiles, batch, seq, heads, tq):
    t = q.shape[0]
    nq = seq // tq
    nb = tq // LANES
    topk = min(TOPK_MAX, seq // 4)
    iq_w = IDX_HEADS * IDX_HEAD_DIM
    small_blk = iq_w // LANES
    hd = ATT_HEAD_DIM
    vrows = hd + BF16_ROWS
    kern = functools.partial(_dsa_kernel, tq=tq, topk=topk, nq=nq)
    return pl.pallas_call(
        kern,
        grid=(batch, nq, heads),
        in_specs=[
            pl.BlockSpec((tq, hd), lambda b, i, h: (b * nq + i, h)),
            pl.BlockSpec((tq, hd), lambda b, i, h: (b * nq + i, h)),
            pl.BlockSpec((1, seq, hd), lambda b, i, h: (h, b, 0)),
            pl.BlockSpec((1, 1, nq, vrows, tq), lambda b, i, h: (b, h, 0, 0, 0)),
            pl.BlockSpec((tq, iq_w), lambda b, i, h: (b * nq + i, 0)),
            pl.BlockSpec((tq, LANES), lambda b, i, h: (b * nq + i, small_blk)),
            pl.BlockSpec((seq, LANES), lambda b, i, h: (b, small_blk)),
            pl.BlockSpec((1, 2, LANES, LANES), lambda b, i, h: (h, 0, 0, 0)),
        ],
        out_specs=pl.BlockSpec((tq, hd), lambda b, i, h: (b * nq + i, h)),
        out_shape=jax.ShapeDtypeStruct((t, heads * hd), jnp.bfloat16),
        scratch_shapes=[
            pltpu.VMEM((2, seq, LANES), jnp.bfloat16),
            pltpu.VMEM((tq, iq_w), jnp.bfloat16),
            pltpu.VMEM((nq * nb + nb + 1, LANES, tq), jnp.float32),
            pltpu.VMEM((tq, tq), jnp.float32),
            pltpu.VMEM((tq, tq), jnp.float32),
            pltpu.VMEM((1, tq), jnp.float32),
            pltpu.VMEM((vrows, tq), jnp.float32),
        ],
        compiler_params=_cparams(("arbitrary", "arbitrary", "arbitrary")),
    )(q, g, k_hm, v_t, idx, idx, idx, bias_tiles)


def _rel_bucket_of(n):
    max_exact = REL_BUCKETS // 2
    nf = jnp.maximum(n, 1).astype(jnp.float32)
    large = max_exact + (jnp.log(nf / max_exact) / math.log(REL_MAX_DIST / max_exact)
                         * (REL_BUCKETS - max_exact)).astype(jnp.int32)
    large = jnp.minimum(large, REL_BUCKETS - 1)
    return jnp.where(n < max_exact, n, large)


def _bias_tiles(rel_bias):
    assert REL_MAX_DIST <= LANES
    n = LANES
    heads = rel_bias.shape[1]
    dist = jnp.arange(2 * n, dtype=jnp.int32)
    bucket = jnp.where(dist >= REL_MAX_DIST, REL_BUCKETS - 1, _rel_bucket_of(dist))
    bv = (rel_bias[bucket] - rel_bias[REL_BUCKETS - 1][None, :]) * LOG2E
    rows = jnp.concatenate([bv.T, jnp.zeros((heads, n), bv.dtype)], axis=1)
    flat = jnp.broadcast_to(rows[:, None, :], (heads, n, 3 * n)).reshape(heads, 3 * n * n)
    toep = flat[:, :n * (3 * n - 1)].reshape(heads, n, 3 * n - 1)[:, :, :2 * n]
    return jnp.stack([toep[:, :, :n], toep[:, :, n:]], axis=1).astype(jnp.float32)


def _ssd_kernel(z_ref, x_ref, bc_ref, dt_ref, shift_ref, cw_ref, cbb_ref, dtb_ref, alog_ref,
                dexp_ref, nw_ref, rexp_ref, o_ref, xe_ref, taps_ref, state_ref, y_ref,
                *, inner, groups):
    f32 = jnp.float32
    bf16 = jnp.bfloat16
    L = SSM_CHUNK
    P = SSM_HEAD_DIM
    N = SSM_STATE
    heads = inner // P
    hpg = heads // groups
    gw = hpg * P
    gn = groups * N
    halo = BF16_ROWS
    ext = halo + L
    dot = functools.partial(jnp.dot, preferred_element_type=f32)

    @pl.when(pl.program_id(1) == 0)
    def _():
        xe_ref[0:halo, :] = jnp.zeros((halo, xe_ref.shape[1]), bf16)
        taps_ref[SSM_CONV * ext:SSM_CONV * ext + halo, :] = cbb_ref[...]
        state_ref[...] = jnp.zeros(state_ref.shape, f32)

    xe_ref[halo:ext, 0:inner] = x_ref[...]
    xe_ref[halo:ext, inner:inner + 2 * gn] = bc_ref[...]
    xe = xe_ref[...]
    for j in range(SSM_CONV):
        taps_ref[j * ext:(j + 1) * ext, :] = xe * cw_ref[j:j + 1, :]
    xe_ref[0:halo, :] = xe_ref[L:ext, :]
    shift = shift_ref[...]
    bcm = _silu(dot(shift, taps_ref[:, inner:inner + 2 * gn]))

    dtr = dt_ref[...] + dtb_ref[...]
    dt = jnp.maximum(dtr, 0.0) + jnp.log1p(jnp.exp(-jnp.abs(dtr)))
    adt = dt * (-jnp.exp(alog_ref[...]))
    ri = lax.broadcasted_iota(jnp.int32, (L, L), 0)
    ci = lax.broadcasted_iota(jnp.int32, (L, L), 1)
    tri = ri >= ci
    cs = _dot_exact_lhs(jnp.where(tri, 1.0, 0.0).astype(bf16), adt)
    cst = cs.T
    csl = cs[L - 1:L, :]
    dt_hi = dt.astype(bf16)
    dt_lo = (dt - dt_hi.astype(f32)).astype(bf16)
    ecs_b = jnp.exp(cs).astype(bf16)
    dec_b = jnp.exp(csl - cs).astype(bf16)
    cdec = jnp.broadcast_to(jnp.exp(csl), (SUBLANES, LANES))
    lane = lax.broadcasted_iota(jnp.int32, (L, LANES), 1)

    ssq = jnp.zeros((L, 1), f32)
    for g in range(groups):
        sl = slice(g * gw, (g + 1) * gw)
        rex = rexp_ref[:, sl]
        xs = _silu(dot(shift, taps_ref[:, sl]))
        bg = bcm[:, g * N:(g + 1) * N]
        cg = bcm[:, gn + g * N:gn + (g + 1) * N].astype(bf16)
        xdt = xs * (dot(dt_hi, rex) + dot(dt_lo, rex))
        xdt_b = xdt.astype(bf16)
        xdec_b = (xdt * dot(dec_b, rex)).astype(bf16)
        gmat = lax.dot_general(cg, bg.astype(bf16), _NT, preferred_element_type=f32)
        bgt = bg.T.astype(bf16)
        st_prev = state_ref[g]
        y = dot(cg, st_prev.astype(bf16)) * dot(ecs_b, rex) + xs * dexp_ref[:, sl]
        state_ref[g] = st_prev * _dot_exact_rhs(cdec, rex)[0:1, :] + dot(bgt, xdec_b)
        diag = []
        for pr in range(gw // LANES):
            xp = xdt_b[:, pr * LANES:(pr + 1) * LANES]
            lhs, rhs = [], []
            for sub in range(LANES // P):
                hh = (g * gw + pr * LANES) // P + sub
                seg = cs[:, hh:hh + 1] - cst[hh:hh + 1, :]
                lm = jnp.exp(jnp.where(tri, seg, -jnp.inf))
                lhs.append((gmat * lm).astype(bf16))
                rhs.append(jnp.where((lane >= sub * P) & (lane < (sub + 1) * P), xp,
                                     jnp.zeros_like(xp)))
            diag.append(dot(jnp.concatenate(lhs, axis=1), jnp.concatenate(rhs, axis=0)))
        yg = (y + jnp.concatenate(diag, axis=1)) * _silu(z_ref[:, sl].astype(f32))
        ssq = ssq + jnp.sum(yg * yg, axis=-1, keepdims=True)
        y_ref[:, sl] = yg

    scale = lax.rsqrt(ssq * (1.0 / inner) + NORM_EPS)
    o_ref[...] = (y_ref[...] * scale * nw_ref[...]).astype(o_ref.dtype)


def _ssd(zxbc, dt, conv_w, conv_b, dt_bias, a_log, d_skip, norm_w, batch, seq, inner, groups):
    t = zxbc.shape[0]
    L = SSM_CHUNK
    heads = inner // SSM_HEAD_DIM
    gn = groups * SSM_STATE
    conv_ch = inner + 2 * gn
    assert heads <= LANES and LANES % SSM_HEAD_DIM == 0 and inner % (2 * gn) == 0
    nc = seq // L
    pad_h = LANES - heads
    dtb = jnp.pad(dt_bias, (0, pad_h)).reshape(1, LANES)
    alog = jnp.pad(a_log, (0, pad_h)).reshape(1, LANES)
    dexp = jnp.repeat(d_skip, SSM_HEAD_DIM).reshape(1, inner)
    rexp = (jnp.arange(LANES, dtype=jnp.int32)[:, None]
            == (jnp.arange(inner, dtype=jnp.int32) // SSM_HEAD_DIM)[None, :]).astype(jnp.bfloat16)
    bc_blk = inner * 2 // (2 * gn)
    halo = BF16_ROWS
    ext = halo + L
    rows = jnp.arange(L, dtype=jnp.int32)[:, None]
    cols = jnp.arange(SSM_CONV * ext + halo, dtype=jnp.int32)[None, :]
    blk, pos = cols // ext, cols % ext
    shift = jnp.where(blk < SSM_CONV, pos == halo + rows - (SSM_CONV - 1 - blk),
                      pos < 2).astype(jnp.bfloat16)
    cb_hi = conv_b.astype(jnp.bfloat16)
    cb_lo = (conv_b - cb_hi.astype(jnp.float32)).astype(jnp.bfloat16)
    cbb = jnp.zeros((halo, conv_ch), jnp.bfloat16).at[0].set(cb_hi).at[1].set(cb_lo)
    kern = functools.partial(_ssd_kernel, inner=inner, groups=groups)
    const = lambda b, c: (0, 0)
    return pl.pallas_call(
        kern,
        grid=(batch, nc),
        in_specs=[
            pl.BlockSpec((L, inner), lambda b, c: (b * nc + c, 0)),
            pl.BlockSpec((L, inner), lambda b, c: (b * nc + c, 1)),
            pl.BlockSpec((L, 2 * gn), lambda b, c: (b * nc + c, bc_blk)),
            pl.BlockSpec((L, LANES), lambda b, c: (b * nc + c, 0)),
            pl.BlockSpec((L, SSM_CONV * ext + halo), const),
            pl.BlockSpec((SSM_CONV, conv_ch), const),
            pl.BlockSpec((halo, conv_ch), const),
            pl.BlockSpec((1, LANES), const),
            pl.BlockSpec((1, LANES), const),
            pl.BlockSpec((1, inner), const),
            pl.BlockSpec((1, inner), const),
            pl.BlockSpec((LANES, inner), const),
        ],
        out_specs=pl.BlockSpec((L, inner), lambda b, c: (b * nc + c, 0)),
        out_shape=jax.ShapeDtypeStruct((t, inner), jnp.bfloat16),
        scratch_shapes=[
            pltpu.VMEM((ext, conv_ch), jnp.bfloat16),
            pltpu.VMEM((SSM_CONV * ext + halo, conv_ch), jnp.bfloat16),
            pltpu.VMEM((groups, SSM_STATE, inner // groups), jnp.float32),
            pltpu.VMEM((L, inner), jnp.float32),
        ],
        compiler_params=_cparams(("arbitrary", "arbitrary")),
    )(zxbc, zxbc, zxbc, dt, shift, conv_w.astype(jnp.bfloat16), cbb, dtb, alog, dexp,
      norm_w.reshape(1, inner), rexp)


def _proj_residual_final_kernel(a_ref, w_ref, x_ref, nw_ref, o_ref):
    x2 = x_ref[...] + jnp.dot(a_ref[...], w_ref[...], preferred_element_type=jnp.float32)
    ms = jnp.mean(x2 * x2, axis=-1, keepdims=True)
    o_ref[...] = (x2 * lax.rsqrt(ms + NORM_EPS) * nw_ref[...]).astype(o_ref.dtype)


def _proj_residual_final(a, w, x, nw, tm):
    t, k = a.shape
    d = w.shape[1]
    return pl.pallas_call(
        _proj_residual_final_kernel,
        grid=(t // tm,),
        in_specs=[pl.BlockSpec((tm, k), lambda i: (i, 0)),
                  pl.BlockSpec((k, d), lambda i: (0, 0)),
                  pl.BlockSpec((tm, d), lambda i: (i, 0)),
                  pl.BlockSpec((1, d), lambda i: (0, 0))],
        out_specs=pl.BlockSpec((tm, d), lambda i: (i, 0)),
        out_shape=jax.ShapeDtypeStruct((t, d), jnp.float32),
        compiler_params=_cparams(("parallel",)),
    )(a, w, x, nw.reshape(1, d))


def _row_tile(t, want):
    while t % want:
        want //= 2
    return want


def kernel(x, norm_w, a_w_in, a_w_out, rel_bias, b_w_in, b_conv_w, b_conv_b, b_dt_bias, b_a_log,
           b_d, b_norm_w, b_w_out, final_norm_w):
    batch, seq, d = x.shape
    t = batch * seq
    bf16 = jnp.bfloat16
    assert norm_w.shape[0] == 2 and a_w_in.shape[0] == 1 and b_w_in.shape[0] == 1
    xf = x.reshape(t, d)

    att_w = a_w_out.shape[1]
    heads = att_w // ATT_HEAD_DIM
    iq_w = IDX_HEADS * IDX_HEAD_DIM
    wa = jnp.swapaxes(a_w_in, 1, 2).reshape(a_w_in.shape[2], d)
    n_small = wa.shape[0] - 4 * att_w - iq_w
    assert n_small == IDX_HEAD_DIM + IDX_HEADS <= LANES
    w_idx = jnp.pad(wa[4 * att_w:, :], ((0, LANES - n_small), (0, 0)))

    tm = _row_tile(seq, 2048)
    tn = _row_tile(att_w, 1024)
    tq = _row_tile(seq, ATT_Q_BLOCK)
    h0 = _rmsnorm(xf, norm_w[0], bf16, _row_tile(t, 512))
    q = _proj(h0, wa, 0, att_w, bf16, tm, tn, scale=ATT_HEAD_DIM ** -0.5 * LOG2E)
    k_hm = _proj(h0, wa, att_w, att_w, bf16, tm, tn, layout="heads")
    v_t = _proj(h0, wa, 2 * att_w, att_w, bf16, _row_tile(seq, 1024), tn, layout="t",
                batch=batch, tq=tq)
    g = _proj(h0, wa, 3 * att_w, att_w, bf16, tm, tn)
    idx = _proj(h0, w_idx, 0, iq_w + LANES, jnp.float32, _row_tile(seq, 1024), iq_w + LANES)
    att = _dsa_attention(q, g, k_hm, v_t, idx, _bias_tiles(rel_bias), batch, seq, heads, tq)
    x1, h1 = _proj_residual_norm(att, a_w_out[0].astype(bf16), xf, norm_w[1], bf16,
                                 _row_tile(t, 512))

    inner = b_w_out.shape[1]
    ssm_heads = b_dt_bias.shape[1]
    conv_ch = b_conv_w.shape[2]
    groups = (conv_ch - inner) // (2 * SSM_STATE)
    wb = jnp.swapaxes(b_w_in, 1, 2).reshape(b_w_in.shape[2], d)
    w_dt = jnp.pad(wb[inner + conv_ch:, :], ((0, LANES - ssm_heads), (0, 0)))
    zxbc = _proj(h1, wb, 0, inner + conv_ch, bf16, tm, _row_tile(inner + conv_ch, 1024))
    dt = _proj(h1, w_dt, 0, LANES, jnp.float32, tm, LANES)
    y = _ssd(zxbc, dt, b_conv_w[0], b_conv_b[0], b_dt_bias[0], b_a_log[0], b_d[0], b_norm_w[0],
             batch, seq, inner, groups)
    out = _proj_residual_final(y, b_w_out[0].astype(bf16), x1, final_norm_w, _row_tile(t, 256))
    return out.reshape(batch, seq, d)
```

```python
import functools
import math

import jax
import jax.numpy as jnp
from jax import lax
from jax.experimental import pallas as pl
from jax.experimental.pallas import tpu as pltpu

NORM_EPS = 1e-6

ATT_HEAD_DIM = 128
IDX_HEADS = 16
IDX_HEAD_DIM = 64
TOPK_MAX = 256
REL_BUCKETS = 32
REL_MAX_DIST = 128

SSM_HEAD_DIM = 64
SSM_STATE = 128
SSM_CONV = 4
SSM_CHUNK = 128

LANES = 128
SUBLANES = 8
BF16_ROWS = 16
VMEM_LIMIT_BYTES = 56 * 1024 * 1024

INT_MIN = -2 ** 31
NEG_BIG = -1e30
LOG2E = math.log2(math.e)
ATT_Q_BLOCK = 512
ATT_LOOP_CHUNKS = 4

_NT = (((1,), (1,)), ((), ()))


def _cparams(sem, flags=None):
    return pltpu.CompilerParams(dimension_semantics=sem, vmem_limit_bytes=VMEM_LIMIT_BYTES,
                                flags=flags)


def _silu(x):
    h = 0.5 * x
    return h + h * jnp.tanh(h)


def _split3(x):
    hi = x.astype(jnp.bfloat16)
    r1 = x - hi.astype(jnp.float32)
    mid = r1.astype(jnp.bfloat16)
    lo = (r1 - mid.astype(jnp.float32)).astype(jnp.bfloat16)
    return hi, mid, lo


def _dot_exact_lhs(a01, x):
    hi, mid, lo = _split3(x)
    f = functools.partial(jnp.dot, preferred_element_type=jnp.float32)
    return f(a01, hi) + f(a01, mid) + f(a01, lo)


def _dot_exact_rhs(x, b01):
    hi, mid, lo = _split3(x)
    f = functools.partial(jnp.dot, preferred_element_type=jnp.float32)
    return f(hi, b01) + f(mid, b01) + f(lo, b01)


def _rmsnorm_kernel(x_ref, nw_ref, o_ref):
    x = x_ref[...]
    ms = jnp.mean(x * x, axis=-1, keepdims=True)
    o_ref[...] = (x * lax.rsqrt(ms + NORM_EPS) * nw_ref[...]).astype(o_ref.dtype)


def _rmsnorm(x, nw, out_dtype, tm):
    t, d = x.shape
    return pl.pallas_call(
        _rmsnorm_kernel,
        grid=(t // tm,),
        in_specs=[pl.BlockSpec((tm, d), lambda i: (i, 0)),
                  pl.BlockSpec((1, d), lambda i: (0, 0))],
        out_specs=pl.BlockSpec((tm, d), lambda i: (i, 0)),
        out_shape=jax.ShapeDtypeStruct((t, d), out_dtype),
        compiler_params=_cparams(("parallel",)),
    )(x, nw.reshape(1, d))


def _proj_kernel(a_ref, w_ref, o_ref, wb_ref, *, layout, scale, tq):
    @pl.when(pl.program_id(1) == 0)
    def _():
        w = w_ref[...]
        if scale != 1.0:
            w = w * scale
        wb_ref[...] = (w if layout == "t" else w.T).astype(wb_ref.dtype)

    f32 = jnp.float32
    if layout == "t":
        acc = lax.dot_general(wb_ref[...], a_ref[...], _NT, preferred_element_type=f32)
        hd = ATT_HEAD_DIM
        for hh in range(o_ref.shape[1]):
            for cc in range(o_ref.shape[2]):
                o_ref[0, hh, cc, 0:hd, :] = acc[hh * hd:(hh + 1) * hd,
                                                cc * tq:(cc + 1) * tq].astype(o_ref.dtype)
                o_ref[0, hh, cc, hd:hd + BF16_ROWS, :] = jnp.ones((BF16_ROWS, tq), o_ref.dtype)
    else:
        acc = jnp.dot(a_ref[...], wb_ref[...], preferred_element_type=f32)
        if layout == "heads":
            for j in range(o_ref.shape[0]):
                o_ref[j] = acc[:, j * LANES:(j + 1) * LANES].astype(o_ref.dtype)
        else:
            o_ref[...] = acc.astype(o_ref.dtype)


def _proj(a, wt, col_off, n, out_dtype, tm, tn, layout="rows", scale=1.0, batch=1, tq=LANES):
    t, d = a.shape
    assert col_off % tn == 0 and n % tn == 0 and t % tm == 0
    off = col_off // tn
    seq = t // batch
    mb = seq // tm
    if layout == "rows":
        out_spec = pl.BlockSpec((tm, tn), lambda j, i: (i, j))
        out_shape = (t, n)
    elif layout == "heads":
        out_spec = pl.BlockSpec((tn // LANES, tm, LANES), lambda j, i: (j, i, 0))
        out_shape = (n // LANES, t, LANES)
    else:
        rows = ATT_HEAD_DIM + BF16_ROWS
        out_spec = pl.BlockSpec((1, tn // ATT_HEAD_DIM, tm // tq, rows, tq),
                                lambda j, i: (i // mb, j, i % mb, 0, 0))
        out_shape = (batch, n // ATT_HEAD_DIM, seq // tq, rows, tq)
    wb_shape = (tn, d) if layout == "t" else (d, tn)
    return pl.pallas_call(
        functools.partial(_proj_kernel, layout=layout, scale=scale, tq=tq),
        grid=(n // tn, t // tm),
        in_specs=[pl.BlockSpec((tm, d), lambda j, i: (i, 0)),
                  pl.BlockSpec((tn, d), lambda j, i: (j + off, 0))],
        out_specs=out_spec,
        out_shape=jax.ShapeDtypeStruct(out_shape, out_dtype),
        scratch_shapes=[pltpu.VMEM(wb_shape, jnp.bfloat16)],
        compiler_params=_cparams(("parallel", "arbitrary")),
    )(a, wt)


def _proj_residual_norm_kernel(a_ref, w_ref, x_ref, nw_ref, xo_ref, ho_ref):
    x1 = x_ref[...] + jnp.dot(a_ref[...], w_ref[...], preferred_element_type=jnp.float32)
    xo_ref[...] = x1
    ms = jnp.mean(x1 * x1, axis=-1, keepdims=True)
    ho_ref[...] = (x1 * lax.rsqrt(ms + NORM_EPS) * nw_ref[...]).astype(ho_ref.dtype)


def _proj_residual_norm(a, w, x, nw, h_dtype, tm):
    t, k = a.shape
    d = w.shape[1]
    return pl.pallas_call(
        _proj_residual_norm_kernel,
        grid=(t // tm,),
        in_specs=[pl.BlockSpec((tm, k), lambda i: (i, 0)),
                  pl.BlockSpec((k, d), lambda i: (0, 0)),
                  pl.BlockSpec((tm, d), lambda i: (i, 0)),
                  pl.BlockSpec((1, d), lambda i: (0, 0))],
        out_specs=[pl.BlockSpec((tm, d), lambda i: (i, 0)),
                   pl.BlockSpec((tm, d), lambda i: (i, 0))],
        out_shape=[jax.ShapeDtypeStruct((t, d), jnp.float32),
                   jax.ShapeDtypeStruct((t, d), h_dtype)],
        compiler_params=_cparams(("parallel",)),
    )(a, w, x, nw.reshape(1, d))


def _dsa_kernel(q_ref, g_ref, k_ref, vt_ref, iq_ref, ikq_ref, ika_ref, bias_ref, o_ref,
                ikbd_ref, iqb_ref, madd_ref, s0_ref, s1_ref, m_ref, tied_ref, acc_ref,
                *, tq, topk, nq):
    qb = pl.program_id(1)
    h = pl.program_id(2)
    nck = qb + 1
    nb = tq // LANES
    base_d = nq * nb
    base_p = base_d + nb
    pairs = IDX_HEADS // 2
    f32 = jnp.float32
    bf16 = jnp.bfloat16
    i32 = jnp.int32
    hd = ATT_HEAD_DIM

    @pl.when((h == 0) & (qb == 0))
    def _():
        blk = ika_ref[...]
        lane = lax.broadcasted_iota(jnp.int32, blk.shape, 1)
        a = jnp.where(lane < IDX_HEAD_DIM, blk, 0.0)
        ikbd_ref[0] = a.astype(bf16)
        ikbd_ref[1] = pltpu.roll(a, IDX_HEAD_DIM, 1).astype(bf16)

    @pl.when(h == 0)
    def _():
        iqb_ref[...] = iq_ref[...].astype(bf16)
        iwt = ikq_ref[...].T[IDX_HEAD_DIM:IDX_HEAD_DIM + IDX_HEADS, :] * (
            IDX_HEADS ** -0.5 * IDX_HEAD_DIM ** -0.5)
        krow = lax.broadcasted_iota(jnp.int32, (tq, tq), 0)
        qcol = qb * tq + lax.broadcasted_iota(jnp.int32, (tq, tq), 1)

        def score_chunk(c, carry):
            start = pl.multiple_of(c * tq, tq)
            ka = ikbd_ref[0, pl.ds(start, tq), :]
            kb = ikbd_ref[1, pl.ds(start, tq), :]
            sc = jnp.zeros((tq, tq), f32)
            for j in range(pairs):
                rhs = iqb_ref[:, j * LANES:(j + 1) * LANES]
                d0 = lax.dot_general(ka, rhs, _NT, preferred_element_type=f32)
                d1 = lax.dot_general(kb, rhs, _NT, preferred_element_type=f32)
                sc = sc + jnp.maximum(d0, 0.0) * iwt[2 * j:2 * j + 1, :]
                sc = sc + jnp.maximum(d1, 0.0) * iwt[2 * j + 1:2 * j + 2, :]
            bits = lax.bitcast_convert_type(sc, i32)
            key = bits ^ ((bits >> 31) & i32(0x7FFFFFFF))
            key = jnp.where(c * tq + krow <= qcol, key, i32(INT_MIN))
            for a in range(nb):
                madd_ref[c * nb + a] = lax.bitcast_convert_type(
                    key[a * LANES:(a + 1) * LANES, :], f32)
            return carry

        lax.fori_loop(0, nck, score_chunk, 0)

        def keys_of(c, a):
            return lax.bitcast_convert_type(madd_ref[c * nb + a], i32)

        srow = lax.broadcasted_iota(i32, (LANES, tq), 0)

        def count(pred):
            def count_chunk(c, part):
                for a in range(nb):
                    pos = c * tq + a * LANES + srow
                    w = jnp.where(pred(keys_of(c, a), pos), 1.0, 0.0)
                    part = part + jnp.sum(w.reshape(LANES // SUBLANES, SUBLANES, tq), axis=0)
                return part

            part = lax.fori_loop(0, nck, count_chunk, jnp.zeros((SUBLANES, tq), f32))
            return jnp.sum(part, axis=0, keepdims=True)

        def bit_step(i, carry):
            tau_u, cnt_tau = carry
            cand_u = tau_u | lax.shift_left(i32(1), 31 - i)
            cand_s = cand_u ^ i32(INT_MIN)
            cnt = count(lambda key, pos: key >= cand_s)
            ok = cnt >= topk
            return jnp.where(ok, cand_u, tau_u), jnp.where(ok, cnt, cnt_tau)

        tau_u, cnt_tau = lax.fori_loop(0, 32, bit_step,
                                       (jnp.zeros((1, tq), i32), jnp.zeros((1, tq), f32)))
        tau = tau_u ^ i32(INT_MIN)

        def write_mask(sel_fn):
            def mask_chunk(c, carry):
                for a in range(nb):
                    key = keys_of(c, a)
                    pos = c * tq + a * LANES + srow
                    sel = sel_fn(key, pos) & (key != i32(INT_MIN))
                    madd_ref[c * nb + a] = jnp.where(sel, 0.0, NEG_BIG).astype(f32)
                return carry

            lax.fori_loop(0, nck, mask_chunk, 0)

        tied_ref[...] = jnp.broadcast_to(jnp.where(cnt_tau > topk, 1.0, 0.0), tied_ref.shape)
        n_tied_rows = jnp.sum(tied_ref[...])

        @pl.when(n_tied_rows == 0.0)
        def _():
            write_mask(lambda key, pos: key >= tau)

        @pl.when(n_tied_rows > 0.0)
        def _():
            need = topk - count(lambda key, pos: key > tau)
            nbits = max(1, (nq * tq - 1).bit_length())

            def pos_step(i, v):
                cand = v | lax.shift_left(i32(1), nbits - 1 - i)
                below = count(lambda key, pos: (key == tau) & (pos < cand))
                return jnp.where(below < need, cand, v)

            v = lax.fori_loop(0, nbits, pos_step, jnp.zeros((1, tq), i32))
            write_mask(lambda key, pos: (key > tau) | ((key == tau) & (pos <= v)))

    dtile = bias_ref[0, 0]
    ptile = bias_ref[0, 1]
    for a in range(nb):
        madd_ref[base_d + a] = madd_ref[qb * nb + a]
        madd_ref[base_d + a, :, a * LANES:(a + 1) * LANES] += dtile
        if a + 1 < nb:
            madd_ref[base_d + a, :, (a + 1) * LANES:(a + 2) * LANES] += ptile
    madd_ref[base_p] = madd_ref[jnp.maximum(qb - 1, 0) * nb + nb - 1]
    madd_ref[base_p, :, 0:LANES] += ptile

    m_ref[...] = jnp.full(m_ref.shape, NEG_BIG, f32)
    acc_ref[...] = jnp.zeros(acc_ref.shape, f32)
    q = q_ref[...]

    s_bufs = (s0_ref, s1_ref)

    def qk(c, par):
        kc = k_ref[0, pl.ds(pl.multiple_of(c * tq, tq), tq), :]
        s = lax.dot_general(kc, q, _NT, preferred_element_type=f32)
        is_diag = c == qb
        is_prev = c == qb - 1
        cmax = None
        for a in range(nb):
            sl = jnp.where(is_diag, base_d + a, c * nb + a)
            if a == nb - 1:
                sl = jnp.where(is_prev, base_p, sl)
            sa = s[a * LANES:(a + 1) * LANES, :] + madd_ref[sl]
            s_bufs[par][a * LANES:(a + 1) * LANES, :] = sa
            sm = jnp.max(sa.reshape(LANES // SUBLANES, SUBLANES, tq), axis=0)
            cmax = sm if cmax is None else jnp.maximum(cmax, sm)
        return jnp.max(cmax, axis=0, keepdims=True)

    def softmax_pv(c, par, cmax):
        m_prev = m_ref[...]
        m_new = jnp.maximum(m_prev, cmax)
        m_ref[...] = m_new
        p = jnp.exp2(s_bufs[par][...] - m_new).astype(bf16)
        acc_ref[...] = jnp.exp2(m_prev - m_new) * acc_ref[...] + jnp.dot(
            vt_ref[0, 0, c], p, preferred_element_type=f32)

    def run(c, n, cm, feed_next):
        for j in range(n):
            cm_next = qk(c + j + 1, (j + 1) & 1) if (j + 1 < n or feed_next) else None
            softmax_pv(c + j, j & 1, cm)
            cm = cm_next
        return cm

    unroll = ATT_LOOP_CHUNKS
    nloops = (nck - 1) // unroll
    cm = lax.fori_loop(0, nloops, lambda i, cm: run(unroll * i, unroll, cm, True), qk(0, 0))
    c0 = unroll * nloops
    for rem in range(1, unroll + 1):
        pl.when(nck - c0 == rem)(functools.partial(run, c0, rem, cm, False))

    out_t = acc_ref[0:hd, :] / acc_ref[hd:hd + 1, :]
    o_ref[...] = (out_t.T * _silu(g_ref[...].astype(f32))).astype(o_ref.dtype)


def _dsa_attention(q, g, k_hm, v_t, idx, bias_tiles, batch, seq, heads, tq):
    t = q.shape[0]
    nq = seq // tq
    nb = tq // LANES
    topk = min(TOPK_MAX, seq // 4)
    iq_w = IDX_HEADS * IDX_HEAD_DIM
    small_blk = iq_w // LANES
    hd = ATT_HEAD_DIM
    vrows = hd + BF16_ROWS
    kern = functools.partial(_dsa_kernel, tq=tq, topk=topk, nq=nq)
    return pl.pallas_call(
        kern,
        grid=(batch, nq, heads),
        in_specs=[
            pl.BlockSpec((tq, hd), lambda b, i, h: (b * nq + i, h)),
            pl.BlockSpec((tq, hd), lambda b, i, h: (b * nq + i, h)),
            pl.BlockSpec((1, seq, hd), lambda b, i, h: (h, b, 0)),
            pl.BlockSpec((1, 1, nq, vrows, tq), lambda b, i, h: (b, h, 0, 0, 0)),
            pl.BlockSpec((tq, iq_w), lambda b, i, h: (b * nq + i, 0)),
            pl.BlockSpec((tq, LANES), lambda b, i, h: (b * nq + i, small_blk)),
            pl.BlockSpec((seq, LANES), lambda b, i, h: (b, small_blk)),
            pl.BlockSpec((1, 2, LANES, LANES), lambda b, i, h: (h, 0, 0, 0)),
        ],
        out_specs=pl.BlockSpec((tq, hd), lambda b, i, h: (b * nq + i, h)),
        out_shape=jax.ShapeDtypeStruct((t, heads * hd), jnp.bfloat16),
        scratch_shapes=[
            pltpu.VMEM((2, seq, LANES), jnp.bfloat16),
            pltpu.VMEM((tq, iq_w), jnp.bfloat16),
            pltpu.VMEM((nq * nb + nb + 1, LANES, tq), jnp.float32),
            pltpu.VMEM((tq, tq), jnp.float32),
            pltpu.VMEM((tq, tq), jnp.float32),
            pltpu.VMEM((1, tq), jnp.float32),
            pltpu.VMEM((SUBLANES, tq), jnp.float32),
            pltpu.VMEM((vrows, tq), jnp.float32),
        ],
        compiler_params=_cparams(("arbitrary", "arbitrary", "arbitrary")),
    )(q, g, k_hm, v_t, idx, idx, idx, bias_tiles)


def _rel_bucket_of(n):
    max_exact = REL_BUCKETS // 2
    nf = jnp.maximum(n, 1).astype(jnp.float32)
    large = max_exact + (jnp.log(nf / max_exact) / math.log(REL_MAX_DIST / max_exact)
                         * (REL_BUCKETS - max_exact)).astype(jnp.int32)
    large = jnp.minimum(large, REL_BUCKETS - 1)
    return jnp.where(n < max_exact, n, large)


def _bias_tiles(rel_bias):
    assert REL_MAX_DIST <= LANES
    n = LANES
    heads = rel_bias.shape[1]
    dist = jnp.arange(2 * n, dtype=jnp.int32)
    bucket = jnp.where(dist >= REL_MAX_DIST, REL_BUCKETS - 1, _rel_bucket_of(dist))
    bv = (rel_bias[bucket] - rel_bias[REL_BUCKETS - 1][None, :]) * LOG2E
    rows = jnp.concatenate([bv.T, jnp.zeros((heads, n), bv.dtype)], axis=1)
    flat = jnp.broadcast_to(rows[:, None, :], (heads, n, 3 * n)).reshape(heads, 3 * n * n)
    toep = flat[:, :n * (3 * n - 1)].reshape(heads, n, 3 * n - 1)[:, :, :2 * n]
    return jnp.stack([toep[:, :, :n], toep[:, :, n:]], axis=1).astype(jnp.float32)


def _ssd_kernel(z_ref, x_ref, bc_ref, dt_ref, shift_ref, cw_ref, cbb_ref, dtb_ref, alog_ref,
                dexp_ref, nw_ref, rexp_ref, o_ref, xe_ref, taps_ref, state_ref, y_ref,
                *, inner, groups):
    f32 = jnp.float32
    bf16 = jnp.bfloat16
    L = SSM_CHUNK
    P = SSM_HEAD_DIM
    N = SSM_STATE
    heads = inner // P
    hpg = heads // groups
    gw = hpg * P
    gn = groups * N
    halo = BF16_ROWS
    ext = halo + L
    dot = functools.partial(jnp.dot, preferred_element_type=f32)

    @pl.when(pl.program_id(1) == 0)
    def _():
        xe_ref[0:halo, :] = jnp.zeros((halo, xe_ref.shape[1]), bf16)
        taps_ref[SSM_CONV * ext:SSM_CONV * ext + halo, :] = cbb_ref[...]
        state_ref[...] = jnp.zeros(state_ref.shape, f32)

    xe_ref[halo:ext, 0:inner] = x_ref[...]
    xe_ref[halo:ext, inner:inner + 2 * gn] = bc_ref[...]
    xe = xe_ref[...]
    for j in range(SSM_CONV):
        taps_ref[j * ext:(j + 1) * ext, :] = xe * cw_ref[j:j + 1, :]
    xe_ref[0:halo, :] = xe_ref[L:ext, :]
    shift = shift_ref[...]
    bcm = _silu(dot(shift, taps_ref[:, inner:inner + 2 * gn]))

    dtr = dt_ref[...] + dtb_ref[...]
    dt = jnp.maximum(dtr, 0.0) + jnp.log1p(jnp.exp(-jnp.abs(dtr)))
    adt = dt * (-jnp.exp(alog_ref[...]))
    ri = lax.broadcasted_iota(jnp.int32, (L, L), 0)
    ci = lax.broadcasted_iota(jnp.int32, (L, L), 1)
    tri = ri >= ci
    cs = _dot_exact_lhs(jnp.where(tri, 1.0, 0.0).astype(bf16), adt)
    cst = cs.T
    csl = cs[L - 1:L, :]
    dt_hi = dt.astype(bf16)
    dt_lo = (dt - dt_hi.astype(f32)).astype(bf16)
    ecs_b = jnp.exp(cs).astype(bf16)
    dec_b = jnp.exp(csl - cs).astype(bf16)
    cdec = jnp.broadcast_to(jnp.exp(csl), (SUBLANES, LANES))
    lane = lax.broadcasted_iota(jnp.int32, (L, LANES), 1)

    ssq = jnp.zeros((L, 1), f32)
    for g in range(groups):
        sl = slice(g * gw, (g + 1) * gw)
        rex = rexp_ref[:, sl]
        xs = _silu(dot(shift, taps_ref[:, sl]))
        bg = bcm[:, g * N:(g + 1) * N]
        cg = bcm[:, gn + g * N:gn + (g + 1) * N].astype(bf16)
        xdt = xs * (dot(dt_hi, rex) + dot(dt_lo, rex))
        xdt_b = xdt.astype(bf16)
        xdec_b = (xdt * dot(dec_b, rex)).astype(bf16)
        gmat = lax.dot_general(cg, bg.astype(bf16), _NT, preferred_element_type=f32)
        bgt = bg.T.astype(bf16)
        st_prev = state_ref[g]
        y = dot(cg, st_prev.astype(bf16)) * dot(ecs_b, rex) + xs * dexp_ref[:, sl]
        state_ref[g] = st_prev * _dot_exact_rhs(cdec, rex)[0:1, :] + dot(bgt, xdec_b)
        diag = []
        for pr in range(gw // LANES):
            xp = xdt_b[:, pr * LANES:(pr + 1) * LANES]
            lhs, rhs = [], []
            for sub in range(LANES // P):
                hh = (g * gw + pr * LANES) // P + sub
                seg = cs[:, hh:hh + 1] - cst[hh:hh + 1, :]
                lm = jnp.exp(jnp.where(tri, seg, -jnp.inf))
                lhs.append((gmat * lm).astype(bf16))
                rhs.append(jnp.where((lane >= sub * P) & (lane < (sub + 1) * P), xp,
                                     jnp.zeros_like(xp)))
            diag.append(dot(jnp.concatenate(lhs, axis=1), jnp.concatenate(rhs, axis=0)))
        yg = (y + jnp.concatenate(diag, axis=1)) * _silu(z_ref[:, sl].astype(f32))
        ssq = ssq + jnp.sum(yg * yg, axis=-1, keepdims=True)
        y_ref[:, sl] = yg

    scale = lax.rsqrt(ssq * (1.0 / inner) + NORM_EPS)
    o_ref[...] = (y_ref[...] * scale * nw_ref[...]).astype(o_ref.dtype)


def _ssd(zxbc, dt, conv_w, conv_b, dt_bias, a_log, d_skip, norm_w, batch, seq, inner, groups):
    t = zxbc.shape[0]
    L = SSM_CHUNK
    heads = inner // SSM_HEAD_DIM
    gn = groups * SSM_STATE
    conv_ch = inner + 2 * gn
    assert heads <= LANES and LANES % SSM_HEAD_DIM == 0 and inner % (2 * gn) == 0
    nc = seq // L
    pad_h = LANES - heads
    dtb = jnp.pad(dt_bias, (0, pad_h)).reshape(1, LANES)
    alog = jnp.pad(a_log, (0, pad_h)).reshape(1, LANES)
    dexp = jnp.repeat(d_skip, SSM_HEAD_DIM).reshape(1, inner)
    rexp = (jnp.arange(LANES, dtype=jnp.int32)[:, None]
            == (jnp.arange(inner, dtype=jnp.int32) // SSM_HEAD_DIM)[None, :]).astype(jnp.bfloat16)
    bc_blk = inner * 2 // (2 * gn)
    halo = BF16_ROWS
    ext = halo + L
    rows = jnp.arange(L, dtype=jnp.int32)[:, None]
    cols = jnp.arange(SSM_CONV * ext + halo, dtype=jnp.int32)[None, :]
    blk, pos = cols // ext, cols % ext
    shift = jnp.where(blk < SSM_CONV, pos == halo + rows - (SSM_CONV - 1 - blk),
                      pos < 2).astype(jnp.bfloat16)
    cb_hi = conv_b.astype(jnp.bfloat16)
    cb_lo = (conv_b - cb_hi.astype(jnp.float32)).astype(jnp.bfloat16)
    cbb = jnp.zeros((halo, conv_ch), jnp.bfloat16).at[0].set(cb_hi).at[1].set(cb_lo)
    kern = functools.partial(_ssd_kernel, inner=inner, groups=groups)
    const = lambda b, c: (0, 0)
    return pl.pallas_call(
        kern,
        grid=(batch, nc),
        in_specs=[
            pl.BlockSpec((L, inner), lambda b, c: (b * nc + c, 0)),
            pl.BlockSpec((L, inner), lambda b, c: (b * nc + c, 1)),
            pl.BlockSpec((L, 2 * gn), lambda b, c: (b * nc + c, bc_blk)),
            pl.BlockSpec((L, LANES), lambda b, c: (b * nc + c, 0)),
            pl.BlockSpec((L, SSM_CONV * ext + halo), const),
            pl.BlockSpec((SSM_CONV, conv_ch), const),
            pl.BlockSpec((halo, conv_ch), const),
            pl.BlockSpec((1, LANES), const),
            pl.BlockSpec((1, LANES), const),
            pl.BlockSpec((1, inner), const),
            pl.BlockSpec((1, inner), const),
            pl.BlockSpec((LANES, inner), const),
        ],
        out_specs=pl.BlockSpec((L, inner), lambda b, c: (b * nc + c, 0)),
        out_shape=jax.ShapeDtypeStruct((t, inner), jnp.bfloat16),
        scratch_shapes=[
            pltpu.VMEM((ext, conv_ch), jnp.bfloat16),
            pltpu.VMEM((SSM_CONV * ext + halo, conv_ch), jnp.bfloat16),
            pltpu.VMEM((groups, SSM_STATE, inner // groups), jnp.float32),
            pltpu.VMEM((L, inner), jnp.float32),
        ],
        compiler_params=_cparams(("arbitrary", "arbitrary")),
    )(zxbc, zxbc, zxbc, dt, shift, conv_w.astype(jnp.bfloat16), cbb, dtb, alog, dexp,
      norm_w.reshape(1, inner), rexp)


def _proj_residual_final_kernel(a_ref, w_ref, x_ref, nw_ref, o_ref):
    x2 = x_ref[...] + jnp.dot(a_ref[...], w_ref[...], preferred_element_type=jnp.float32)
    ms = jnp.mean(x2 * x2, axis=-1, keepdims=True)
    o_ref[...] = (x2 * lax.rsqrt(ms + NORM_EPS) * nw_ref[...]).astype(o_ref.dtype)


def _proj_residual_final(a, w, x, nw, tm):
    t, k = a.shape
    d = w.shape[1]
    return pl.pallas_call(
        _proj_residual_final_kernel,
        grid=(t // tm,),
        in_specs=[pl.BlockSpec((tm, k), lambda i: (i, 0)),
                  pl.BlockSpec((k, d), lambda i: (0, 0)),
                  pl.BlockSpec((tm, d), lambda i: (i, 0)),
                  pl.BlockSpec((1, d), lambda i: (0, 0))],
        out_specs=pl.BlockSpec((tm, d), lambda i: (i, 0)),
        out_shape=jax.ShapeDtypeStruct((t, d), jnp.float32),
        compiler_params=_cparams(("parallel",)),
    )(a, w, x, nw.reshape(1, d))


def _row_tile(t, want):
    while t % want:
        want //= 2
    return want


def kernel(x, norm_w, a_w_in, a_w_out, rel_bias, b_w_in, b_conv_w, b_conv_b, b_dt_bias, b_a_log,
           b_d, b_norm_w, b_w_out, final_norm_w):
    batch, seq, d = x.shape
    t = batch * seq
    bf16 = jnp.bfloat16
    assert norm_w.shape[0] == 2 and a_w_in.shape[0] == 1 and b_w_in.shape[0] == 1
    xf = x.reshape(t, d)

    att_w = a_w_out.shape[1]
    heads = att_w // ATT_HEAD_DIM
    iq_w = IDX_HEADS * IDX_HEAD_DIM
    wa = jnp.swapaxes(a_w_in, 1, 2).reshape(a_w_in.shape[2], d)
    n_small = wa.shape[0] - 4 * att_w - iq_w
    assert n_small == IDX_HEAD_DIM + IDX_HEADS <= LANES
    w_idx = jnp.pad(wa[4 * att_w:, :], ((0, LANES - n_small), (0, 0)))

    tm = _row_tile(seq, 2048)
    tn = _row_tile(att_w, 1024)
    tq = _row_tile(seq, ATT_Q_BLOCK)
    h0 = _rmsnorm(xf, norm_w[0], bf16, _row_tile(t, 512))
    q = _proj(h0, wa, 0, att_w, bf16, tm, tn, scale=ATT_HEAD_DIM ** -0.5 * LOG2E)
    k_hm = _proj(h0, wa, att_w, att_w, bf16, tm, tn, layout="heads")
    v_t = _proj(h0, wa, 2 * att_w, att_w, bf16, _row_tile(seq, 1024), tn, layout="t",
                batch=batch, tq=tq)
    g = _proj(h0, wa, 3 * att_w, att_w, bf16, tm, tn)
    idx = _proj(h0, w_idx, 0, iq_w + LANES, jnp.float32, _row_tile(seq, 1024), iq_w + LANES)
    att = _dsa_attention(q, g, k_hm, v_t, idx, _bias_tiles(rel_bias), batch, seq, heads, tq)
    x1, h1 = _proj_residual_norm(att, a_w_out[0].astype(bf16), xf, norm_w[1], bf16,
                                 _row_tile(t, 512))

    inner = b_w_out.shape[1]
    ssm_heads = b_dt_bias.shape[1]
    conv_ch = b_conv_w.shape[2]
    groups = (conv_ch - inner) // (2 * SSM_STATE)
    wb = jnp.swapaxes(b_w_in, 1, 2).reshape(b_w_in.shape[2], d)
    w_dt = jnp.pad(wb[inner + conv_ch:, :], ((0, LANES - ssm_heads), (0, 0)))
    zxbc = _proj(h1, wb, 0, inner + conv_ch, bf16, tm, _row_tile(inner + conv_ch, 1024))
    dt = _proj(h1, w_dt, 0, LANES, jnp.float32, tm, LANES)
    y = _ssd(zxbc, dt, b_conv_w[0], b_conv_b[0], b_dt_bias[0], b_a_log[0], b_d[0], b_norm_w[0],
             batch, seq, inner, groups)
    out = _proj_residual_final(y, b_w_out[0].astype(bf16), x1, final_norm_w, _row_tile(t, 256))
    return out.reshape(batch, seq, d)
```

```python
import functools
import math

import jax
import jax.numpy as jnp
from jax import lax
from jax.experimental import pallas as pl
from jax.experimental.pallas import tpu as pltpu

NORM_EPS = 1e-6

ATT_HEAD_DIM = 128
IDX_HEADS = 16
IDX_HEAD_DIM = 64
TOPK_MAX = 256
REL_BUCKETS = 32
REL_MAX_DIST = 128

SSM_HEAD_DIM = 64
SSM_STATE = 128
SSM_CONV = 4
SSM_CHUNK = 128

LANES = 128
SUBLANES = 8
BF16_ROWS = 16
VMEM_LIMIT_BYTES = 56 * 1024 * 1024

INT_MIN = -2 ** 31
NEG_BIG = -1e30
LOG2E = math.log2(math.e)
ATT_Q_BLOCK = 512
ATT_LOOP_CHUNKS = 4
ATT_HEADS_PER_STEP = 2

_NT = (((1,), (1,)), ((), ()))


def _cparams(sem, flags=None):
    return pltpu.CompilerParams(dimension_semantics=sem, vmem_limit_bytes=VMEM_LIMIT_BYTES,
                                flags=flags)


def _silu(x):
    h = 0.5 * x
    return h + h * jnp.tanh(h)


def _split3(x):
    hi = x.astype(jnp.bfloat16)
    r1 = x - hi.astype(jnp.float32)
    mid = r1.astype(jnp.bfloat16)
    lo = (r1 - mid.astype(jnp.float32)).astype(jnp.bfloat16)
    return hi, mid, lo


def _dot_exact_lhs(a01, x):
    hi, mid, lo = _split3(x)
    f = functools.partial(jnp.dot, preferred_element_type=jnp.float32)
    return f(a01, hi) + f(a01, mid) + f(a01, lo)


def _dot_exact_rhs(x, b01):
    hi, mid, lo = _split3(x)
    f = functools.partial(jnp.dot, preferred_element_type=jnp.float32)
    return f(hi, b01) + f(mid, b01) + f(lo, b01)


def _rmsnorm_kernel(x_ref, nw_ref, o_ref):
    x = x_ref[...]
    ms = jnp.mean(x * x, axis=-1, keepdims=True)
    o_ref[...] = (x * lax.rsqrt(ms + NORM_EPS) * nw_ref[...]).astype(o_ref.dtype)


def _rmsnorm(x, nw, out_dtype, tm):
    t, d = x.shape
    return pl.pallas_call(
        _rmsnorm_kernel,
        grid=(t // tm,),
        in_specs=[pl.BlockSpec((tm, d), lambda i: (i, 0)),
                  pl.BlockSpec((1, d), lambda i: (0, 0))],
        out_specs=pl.BlockSpec((tm, d), lambda i: (i, 0)),
        out_shape=jax.ShapeDtypeStruct((t, d), out_dtype),
        compiler_params=_cparams(("parallel",)),
    )(x, nw.reshape(1, d))


def _proj_kernel(a_ref, w_ref, o_ref, wb_ref, *, layout, scale, tq):
    @pl.when(pl.program_id(1) == 0)
    def _():
        w = w_ref[...]
        if scale != 1.0:
            w = w * scale
        wb_ref[...] = (w if layout == "t" else w.T).astype(wb_ref.dtype)

    f32 = jnp.float32
    if layout == "t":
        acc = lax.dot_general(wb_ref[...], a_ref[...], _NT, preferred_element_type=f32)
        hd = ATT_HEAD_DIM
        for hh in range(o_ref.shape[1]):
            for cc in range(o_ref.shape[2]):
                o_ref[0, hh, cc, 0:hd, :] = acc[hh * hd:(hh + 1) * hd,
                                                cc * tq:(cc + 1) * tq].astype(o_ref.dtype)
                o_ref[0, hh, cc, hd:hd + BF16_ROWS, :] = jnp.ones((BF16_ROWS, tq), o_ref.dtype)
    else:
        acc = jnp.dot(a_ref[...], wb_ref[...], preferred_element_type=f32)
        if layout == "heads":
            for j in range(o_ref.shape[0]):
                o_ref[j] = acc[:, j * LANES:(j + 1) * LANES].astype(o_ref.dtype)
        else:
            o_ref[...] = acc.astype(o_ref.dtype)


def _proj(a, wt, col_off, n, out_dtype, tm, tn, layout="rows", scale=1.0, batch=1, tq=LANES):
    t, d = a.shape
    assert col_off % tn == 0 and n % tn == 0 and t % tm == 0
    off = col_off // tn
    seq = t // batch
    mb = seq // tm
    if layout == "rows":
        out_spec = pl.BlockSpec((tm, tn), lambda j, i: (i, j))
        out_shape = (t, n)
    elif layout == "heads":
        out_spec = pl.BlockSpec((tn // LANES, tm, LANES), lambda j, i: (j, i, 0))
        out_shape = (n // LANES, t, LANES)
    else:
        rows = ATT_HEAD_DIM + BF16_ROWS
        out_spec = pl.BlockSpec((1, tn // ATT_HEAD_DIM, tm // tq, rows, tq),
                                lambda j, i: (i // mb, j, i % mb, 0, 0))
        out_shape = (batch, n // ATT_HEAD_DIM, seq // tq, rows, tq)
    wb_shape = (tn, d) if layout == "t" else (d, tn)
    return pl.pallas_call(
        functools.partial(_proj_kernel, layout=layout, scale=scale, tq=tq),
        grid=(n // tn, t // tm),
        in_specs=[pl.BlockSpec((tm, d), lambda j, i: (i, 0)),
                  pl.BlockSpec((tn, d), lambda j, i: (j + off, 0))],
        out_specs=out_spec,
        out_shape=jax.ShapeDtypeStruct(out_shape, out_dtype),
        scratch_shapes=[pltpu.VMEM(wb_shape, jnp.bfloat16)],
        compiler_params=_cparams(("parallel", "arbitrary")),
    )(a, wt)


def _proj_residual_norm_kernel(a_ref, w_ref, x_ref, nw_ref, xo_ref, ho_ref):
    x1 = x_ref[...] + jnp.dot(a_ref[...], w_ref[...], preferred_element_type=jnp.float32)
    xo_ref[...] = x1
    ms = jnp.mean(x1 * x1, axis=-1, keepdims=True)
    ho_ref[...] = (x1 * lax.rsqrt(ms + NORM_EPS) * nw_ref[...]).astype(ho_ref.dtype)


def _proj_residual_norm(a, w, x, nw, h_dtype, tm):
    t, k = a.shape
    d = w.shape[1]
    return pl.pallas_call(
        _proj_residual_norm_kernel,
        grid=(t // tm,),
        in_specs=[pl.BlockSpec((tm, k), lambda i: (i, 0)),
                  pl.BlockSpec((k, d), lambda i: (0, 0)),
                  pl.BlockSpec((tm, d), lambda i: (i, 0)),
                  pl.BlockSpec((1, d), lambda i: (0, 0))],
        out_specs=[pl.BlockSpec((tm, d), lambda i: (i, 0)),
                   pl.BlockSpec((tm, d), lambda i: (i, 0))],
        out_shape=[jax.ShapeDtypeStruct((t, d), jnp.float32),
                   jax.ShapeDtypeStruct((t, d), h_dtype)],
        compiler_params=_cparams(("parallel",)),
    )(a, w, x, nw.reshape(1, d))


def _dsa_kernel(q_ref, g_ref, k_ref, vt_ref, iq_ref, ikq_ref, ika_ref, bias_ref, o_ref,
                ikbd_ref, iqb_ref, madd_ref, s0_ref, s1_ref, m_ref, tied_ref, acc_ref,
                *, tq, topk, nq):
    qb = pl.program_id(1)
    h = pl.program_id(2)
    nck = qb + 1
    nb = tq // LANES
    base_d = nq * nb
    base_p = base_d + nb
    pairs = IDX_HEADS // 2
    f32 = jnp.float32
    bf16 = jnp.bfloat16
    i32 = jnp.int32
    hd = ATT_HEAD_DIM

    @pl.when((h == 0) & (qb == 0))
    def _():
        blk = ika_ref[...]
        lane = lax.broadcasted_iota(jnp.int32, blk.shape, 1)
        a = jnp.where(lane < IDX_HEAD_DIM, blk, 0.0)
        ikbd_ref[0] = a.astype(bf16)
        ikbd_ref[1] = pltpu.roll(a, IDX_HEAD_DIM, 1).astype(bf16)

    @pl.when(h == 0)
    def _():
        iqb_ref[...] = iq_ref[...].astype(bf16)
        iwt = ikq_ref[...].T[IDX_HEAD_DIM:IDX_HEAD_DIM + IDX_HEADS, :] * (
            IDX_HEADS ** -0.5 * IDX_HEAD_DIM ** -0.5)
        krow = lax.broadcasted_iota(jnp.int32, (tq, tq), 0)
        qcol = qb * tq + lax.broadcasted_iota(jnp.int32, (tq, tq), 1)

        def score_chunk(c, carry):
            start = pl.multiple_of(c * tq, tq)
            ka = ikbd_ref[0, pl.ds(start, tq), :]
            kb = ikbd_ref[1, pl.ds(start, tq), :]
            sc = jnp.zeros((tq, tq), f32)
            for j in range(pairs):
                rhs = iqb_ref[:, j * LANES:(j + 1) * LANES]
                d0 = lax.dot_general(ka, rhs, _NT, preferred_element_type=f32)
                d1 = lax.dot_general(kb, rhs, _NT, preferred_element_type=f32)
                sc = sc + jnp.maximum(d0, 0.0) * iwt[2 * j:2 * j + 1, :]
                sc = sc + jnp.maximum(d1, 0.0) * iwt[2 * j + 1:2 * j + 2, :]
            bits = lax.bitcast_convert_type(sc, i32)
            key = bits ^ ((bits >> 31) & i32(0x7FFFFFFF))
            key = jnp.where(c * tq + krow <= qcol, key, i32(INT_MIN))
            for a in range(nb):
                madd_ref[c * nb + a] = lax.bitcast_convert_type(
                    key[a * LANES:(a + 1) * LANES, :], f32)
            return carry

        lax.fori_loop(0, nck, score_chunk, 0)

        def keys_of(c, a):
            return lax.bitcast_convert_type(madd_ref[c * nb + a], i32)

        srow = lax.broadcasted_iota(i32, (LANES, tq), 0)

        def count(pred):
            def count_chunk(c, part):
                for a in range(nb):
                    pos = c * tq + a * LANES + srow
                    w = jnp.where(pred(keys_of(c, a), pos), 1.0, 0.0)
                    part = part + jnp.sum(w.reshape(LANES // SUBLANES, SUBLANES, tq), axis=0)
                return part

            part = lax.fori_loop(0, nck, count_chunk, jnp.zeros((SUBLANES, tq), f32))
            return jnp.sum(part, axis=0, keepdims=True)

        def bit_step(i, carry):
            tau_u, cnt_tau = carry
            cand_u = tau_u | lax.shift_left(i32(1), 31 - i)
            cand_s = cand_u ^ i32(INT_MIN)
            cnt = count(lambda key, pos: key >= cand_s)
            ok = cnt >= topk
            return jnp.where(ok, cand_u, tau_u), jnp.where(ok, cnt, cnt_tau)

        tau_u, cnt_tau = lax.fori_loop(0, 32, bit_step,
                                       (jnp.zeros((1, tq), i32), jnp.zeros((1, tq), f32)))
        tau = tau_u ^ i32(INT_MIN)

        def write_mask(sel_fn):
            def mask_chunk(c, carry):
                for a in range(nb):
                    key = keys_of(c, a)
                    pos = c * tq + a * LANES + srow
                    sel = sel_fn(key, pos) & (key != i32(INT_MIN))
                    madd_ref[c * nb + a] = jnp.where(sel, 0.0, NEG_BIG).astype(f32)
                return carry

            lax.fori_loop(0, nck, mask_chunk, 0)

        tied_ref[...] = jnp.broadcast_to(jnp.where(cnt_tau > topk, 1.0, 0.0), tied_ref.shape)
        n_tied_rows = jnp.sum(tied_ref[...])

        @pl.when(n_tied_rows == 0.0)
        def _():
            write_mask(lambda key, pos: key >= tau)

        @pl.when(n_tied_rows > 0.0)
        def _():
            need = topk - count(lambda key, pos: key > tau)
            nbits = max(1, (nq * tq - 1).bit_length())

            def pos_step(i, v):
                cand = v | lax.shift_left(i32(1), nbits - 1 - i)
                below = count(lambda key, pos: (key == tau) & (pos < cand))
                return jnp.where(below < need, cand, v)

            v = lax.fori_loop(0, nbits, pos_step, jnp.zeros((1, tq), i32))
            write_mask(lambda key, pos: (key > tau) | ((key == tau) & (pos <= v)))

    def attend(hh):
        dtile = bias_ref[hh, 0]
        ptile = bias_ref[hh, 1]
        for a in range(nb):
            madd_ref[base_d + a] = madd_ref[qb * nb + a]
            madd_ref[base_d + a, :, a * LANES:(a + 1) * LANES] += dtile
            if a + 1 < nb:
                madd_ref[base_d + a, :, (a + 1) * LANES:(a + 2) * LANES] += ptile
        madd_ref[base_p] = madd_ref[jnp.maximum(qb - 1, 0) * nb + nb - 1]
        madd_ref[base_p, :, 0:LANES] += ptile

        m_ref[...] = jnp.full(m_ref.shape, NEG_BIG, f32)
        acc_ref[...] = jnp.zeros(acc_ref.shape, f32)
        q = q_ref[:, hh * hd:(hh + 1) * hd]

        s_bufs = (s0_ref, s1_ref)

        def qk(c, par):
            kc = k_ref[hh, pl.ds(pl.multiple_of(c * tq, tq), tq), :]
            s = lax.dot_general(kc, q, _NT, preferred_element_type=f32)
            is_diag = c == qb
            is_prev = c == qb - 1
            cmax = None
            for a in range(nb):
                sl = jnp.where(is_diag, base_d + a, c * nb + a)
                if a == nb - 1:
                    sl = jnp.where(is_prev, base_p, sl)
                sa = s[a * LANES:(a + 1) * LANES, :] + madd_ref[sl]
                s_bufs[par][a * LANES:(a + 1) * LANES, :] = sa
                sm = jnp.max(sa.reshape(LANES // SUBLANES, SUBLANES, tq), axis=0)
                cmax = sm if cmax is None else jnp.maximum(cmax, sm)
            return jnp.max(cmax, axis=0, keepdims=True)

        def softmax_pv(c, par, cmax):
            m_prev = m_ref[...]
            m_new = jnp.maximum(m_prev, cmax)
            m_ref[...] = m_new
            p = jnp.exp2(s_bufs[par][...] - m_new).astype(bf16)
            acc_ref[...] = jnp.exp2(m_prev - m_new) * acc_ref[...] + jnp.dot(
                vt_ref[0, hh, c], p, preferred_element_type=f32)

        def run(c, n, cm, feed_next):
            for j in range(n):
                cm_next = qk(c + j + 1, (j + 1) & 1) if (j + 1 < n or feed_next) else None
                softmax_pv(c + j, j & 1, cm)
                cm = cm_next
            return cm

        unroll = ATT_LOOP_CHUNKS
        nloops = (nck - 1) // unroll
        cm = lax.fori_loop(0, nloops, lambda i, cm: run(unroll * i, unroll, cm, True), qk(0, 0))
        c0 = unroll * nloops
        for rem in range(1, unroll + 1):
            pl.when(nck - c0 == rem)(functools.partial(run, c0, rem, cm, False))

        out_t = acc_ref[0:hd, :] / acc_ref[hd:hd + 1, :]
        o_ref[:, hh * hd:(hh + 1) * hd] = (
            out_t.T * _silu(g_ref[:, hh * hd:(hh + 1) * hd].astype(f32))).astype(o_ref.dtype)

    for hh in range(k_ref.shape[0]):
        attend(hh)


def _dsa_attention(q, g, k_hm, v_t, idx, bias_tiles, batch, seq, heads, tq):
    t = q.shape[0]
    nq = seq // tq
    nb = tq // LANES
    topk = min(TOPK_MAX, seq // 4)
    iq_w = IDX_HEADS * IDX_HEAD_DIM
    small_blk = iq_w // LANES
    hd = ATT_HEAD_DIM
    vrows = hd + BF16_ROWS
    hps = ATT_HEADS_PER_STEP if heads % ATT_HEADS_PER_STEP == 0 else 1
    kern = functools.partial(_dsa_kernel, tq=tq, topk=topk, nq=nq)
    return pl.pallas_call(
        kern,
        grid=(batch, nq, heads // hps),
        in_specs=[
            pl.BlockSpec((tq, hps * hd), lambda b, i, h: (b * nq + i, h)),
            pl.BlockSpec((tq, hps * hd), lambda b, i, h: (b * nq + i, h)),
            pl.BlockSpec((hps, seq, hd), lambda b, i, h: (h, b, 0)),
            pl.BlockSpec((1, hps, nq, vrows, tq), lambda b, i, h: (b, h, 0, 0, 0)),
            pl.BlockSpec((tq, iq_w), lambda b, i, h: (b * nq + i, 0)),
            pl.BlockSpec((tq, LANES), lambda b, i, h: (b * nq + i, small_blk)),
            pl.BlockSpec((seq, LANES), lambda b, i, h: (b, small_blk)),
            pl.BlockSpec((hps, 2, LANES, LANES), lambda b, i, h: (h, 0, 0, 0)),
        ],
        out_specs=pl.BlockSpec((tq, hps * hd), lambda b, i, h: (b * nq + i, h)),
        out_shape=jax.ShapeDtypeStruct((t, heads * hd), jnp.bfloat16),
        scratch_shapes=[
            pltpu.VMEM((2, seq, LANES), jnp.bfloat16),
            pltpu.VMEM((tq, iq_w), jnp.bfloat16),
            pltpu.VMEM((nq * nb + nb + 1, LANES, tq), jnp.float32),
            pltpu.VMEM((tq, tq), jnp.float32),
            pltpu.VMEM((tq, tq), jnp.float32),
            pltpu.VMEM((1, tq), jnp.float32),
            pltpu.VMEM((SUBLANES, tq), jnp.float32),
            pltpu.VMEM((vrows, tq), jnp.float32),
        ],
        compiler_params=_cparams(("arbitrary", "arbitrary", "arbitrary")),
    )(q, g, k_hm, v_t, idx, idx, idx, bias_tiles)


def _rel_bucket_of(n):
    max_exact = REL_BUCKETS // 2
    nf = jnp.maximum(n, 1).astype(jnp.float32)
    large = max_exact + (jnp.log(nf / max_exact) / math.log(REL_MAX_DIST / max_exact)
                         * (REL_BUCKETS - max_exact)).astype(jnp.int32)
    large = jnp.minimum(large, REL_BUCKETS - 1)
    return jnp.where(n < max_exact, n, large)


def _bias_tiles(rel_bias):
    assert REL_MAX_DIST <= LANES
    n = LANES
    heads = rel_bias.shape[1]
    dist = jnp.arange(2 * n, dtype=jnp.int32)
    bucket = jnp.where(dist >= REL_MAX_DIST, REL_BUCKETS - 1, _rel_bucket_of(dist))
    bv = (rel_bias[bucket] - rel_bias[REL_BUCKETS - 1][None, :]) * LOG2E
    rows = jnp.concatenate([bv.T, jnp.zeros((heads, n), bv.dtype)], axis=1)
    flat = jnp.broadcast_to(rows[:, None, :], (heads, n, 3 * n)).reshape(heads, 3 * n * n)
    toep = flat[:, :n * (3 * n - 1)].reshape(heads, n, 3 * n - 1)[:, :, :2 * n]
    return jnp.stack([toep[:, :, :n], toep[:, :, n:]], axis=1).astype(jnp.float32)


def _ssd_kernel(z_ref, x_ref, bc_ref, dt_ref, shift_ref, cw_ref, cbb_ref, dtb_ref, alog_ref,
                dexp_ref, nw_ref, rexp_ref, o_ref, xe_ref, taps_ref, state_ref, y_ref,
                *, inner, groups):
    f32 = jnp.float32
    bf16 = jnp.bfloat16
    L = SSM_CHUNK
    P = SSM_HEAD_DIM
    N = SSM_STATE
    heads = inner // P
    hpg = heads // groups
    gw = hpg * P
    gn = groups * N
    halo = BF16_ROWS
    ext = halo + L
    dot = functools.partial(jnp.dot, preferred_element_type=f32)

    @pl.when(pl.program_id(1) == 0)
    def _():
        xe_ref[0:halo, :] = jnp.zeros((halo, xe_ref.shape[1]), bf16)
        taps_ref[SSM_CONV * ext:SSM_CONV * ext + halo, :] = cbb_ref[...]
        state_ref[...] = jnp.zeros(state_ref.shape, f32)

    xe_ref[halo:ext, 0:inner] = x_ref[...]
    xe_ref[halo:ext, inner:inner + 2 * gn] = bc_ref[...]
    xe = xe_ref[...]
    for j in range(SSM_CONV):
        taps_ref[j * ext:(j + 1) * ext, :] = xe * cw_ref[j:j + 1, :]
    xe_ref[0:halo, :] = xe_ref[L:ext, :]
    shift = shift_ref[...]
    bcm = _silu(dot(shift, taps_ref[:, inner:inner + 2 * gn]))

    dtr = dt_ref[...] + dtb_ref[...]
    dt = jnp.maximum(dtr, 0.0) + jnp.log1p(jnp.exp(-jnp.abs(dtr)))
    adt = dt * (-jnp.exp(alog_ref[...]))
    ri = lax.broadcasted_iota(jnp.int32, (L, L), 0)
    ci = lax.broadcasted_iota(jnp.int32, (L, L), 1)
    tri = ri >= ci
    cs = _dot_exact_lhs(jnp.where(tri, 1.0, 0.0).astype(bf16), adt)
    cst = cs.T
    csl = cs[L - 1:L, :]
    dt_hi = dt.astype(bf16)
    dt_lo = (dt - dt_hi.astype(f32)).astype(bf16)
    ecs_b = jnp.exp(cs).astype(bf16)
    dec_b = jnp.exp(csl - cs).astype(bf16)
    cdec = jnp.broadcast_to(jnp.exp(csl), (SUBLANES, LANES))
    lane = lax.broadcasted_iota(jnp.int32, (L, LANES), 1)

    ssq = jnp.zeros((L, 1), f32)
    for g in range(groups):
        sl = slice(g * gw, (g + 1) * gw)
        rex = rexp_ref[:, sl]
        xs = _silu(dot(shift, taps_ref[:, sl]))
        bg = bcm[:, g * N:(g + 1) * N]
        cg = bcm[:, gn + g * N:gn + (g + 1) * N].astype(bf16)
        xdt = xs * (dot(dt_hi, rex) + dot(dt_lo, rex))
        xdt_b = xdt.astype(bf16)
        xdec_b = (xdt * dot(dec_b, rex)).astype(bf16)
        gmat = lax.dot_general(cg, bg.astype(bf16), _NT, preferred_element_type=f32)
        bgt = bg.T.astype(bf16)
        st_prev = state_ref[g]
        y = dot(cg, st_prev.astype(bf16)) * dot(ecs_b, rex) + xs * dexp_ref[:, sl]
        state_ref[g] = st_prev * _dot_exact_rhs(cdec, rex)[0:1, :] + dot(bgt, xdec_b)
        diag = []
        for pr in range(gw // LANES):
            xp = xdt_b[:, pr * LANES:(pr + 1) * LANES]
            lhs, rhs = [], []
            for sub in range(LANES // P):
                hh = (g * gw + pr * LANES) // P + sub
                seg = cs[:, hh:hh + 1] - cst[hh:hh + 1, :]
                lm = jnp.exp(jnp.where(tri, seg, -jnp.inf))
                lhs.append((gmat * lm).astype(bf16))
                rhs.append(jnp.where((lane >= sub * P) & (lane < (sub + 1) * P), xp,
                                     jnp.zeros_like(xp)))
            diag.append(dot(jnp.concatenate(lhs, axis=1), jnp.concatenate(rhs, axis=0)))
        yg = (y + jnp.concatenate(diag, axis=1)) * _silu(z_ref[:, sl].astype(f32))
        ssq = ssq + jnp.sum(yg * yg, axis=-1, keepdims=True)
        y_ref[:, sl] = yg

    scale = lax.rsqrt(ssq * (1.0 / inner) + NORM_EPS)
    o_ref[...] = (y_ref[...] * scale * nw_ref[...]).astype(o_ref.dtype)


def _ssd(zxbc, dt, conv_w, conv_b, dt_bias, a_log, d_skip, norm_w, batch, seq, inner, groups):
    t = zxbc.shape[0]
    L = SSM_CHUNK
    heads = inner // SSM_HEAD_DIM
    gn = groups * SSM_STATE
    conv_ch = inner + 2 * gn
    assert heads <= LANES and LANES % SSM_HEAD_DIM == 0 and inner % (2 * gn) == 0
    nc = seq // L
    pad_h = LANES - heads
    dtb = jnp.pad(dt_bias, (0, pad_h)).reshape(1, LANES)
    alog = jnp.pad(a_log, (0, pad_h)).reshape(1, LANES)
    dexp = jnp.repeat(d_skip, SSM_HEAD_DIM).reshape(1, inner)
    rexp = (jnp.arange(LANES, dtype=jnp.int32)[:, None]
            == (jnp.arange(inner, dtype=jnp.int32) // SSM_HEAD_DIM)[None, :]).astype(jnp.bfloat16)
    bc_blk = inner * 2 // (2 * gn)
    halo = BF16_ROWS
    ext = halo + L
    rows = jnp.arange(L, dtype=jnp.int32)[:, None]
    cols = jnp.arange(SSM_CONV * ext + halo, dtype=jnp.int32)[None, :]
    blk, pos = cols // ext, cols % ext
    shift = jnp.where(blk < SSM_CONV, pos == halo + rows - (SSM_CONV - 1 - blk),
                      pos < 2).astype(jnp.bfloat16)
    cb_hi = conv_b.astype(jnp.bfloat16)
    cb_lo = (conv_b - cb_hi.astype(jnp.float32)).astype(jnp.bfloat16)
    cbb = jnp.zeros((halo, conv_ch), jnp.bfloat16).at[0].set(cb_hi).at[1].set(cb_lo)
    kern = functools.partial(_ssd_kernel, inner=inner, groups=groups)
    const = lambda b, c: (0, 0)
    return pl.pallas_call(
        kern,
        grid=(batch, nc),
        in_specs=[
            pl.BlockSpec((L, inner), lambda b, c: (b * nc + c, 0)),
            pl.BlockSpec((L, inner), lambda b, c: (b * nc + c, 1)),
            pl.BlockSpec((L, 2 * gn), lambda b, c: (b * nc + c, bc_blk)),
            pl.BlockSpec((L, LANES), lambda b, c: (b * nc + c, 0)),
            pl.BlockSpec((L, SSM_CONV * ext + halo), const),
            pl.BlockSpec((SSM_CONV, conv_ch), const),
            pl.BlockSpec((halo, conv_ch), const),
            pl.BlockSpec((1, LANES), const),
            pl.BlockSpec((1, LANES), const),
            pl.BlockSpec((1, inner), const),
            pl.BlockSpec((1, inner), const),
            pl.BlockSpec((LANES, inner), const),
        ],
        out_specs=pl.BlockSpec((L, inner), lambda b, c: (b * nc + c, 0)),
        out_shape=jax.ShapeDtypeStruct((t, inner), jnp.bfloat16),
        scratch_shapes=[
            pltpu.VMEM((ext, conv_ch), jnp.bfloat16),
            pltpu.VMEM((SSM_CONV * ext + halo, conv_ch), jnp.bfloat16),
            pltpu.VMEM((groups, SSM_STATE, inner // groups), jnp.float32),
            pltpu.VMEM((L, inner), jnp.float32),
        ],
        compiler_params=_cparams(("arbitrary", "arbitrary")),
    )(zxbc, zxbc, zxbc, dt, shift, conv_w.astype(jnp.bfloat16), cbb, dtb, alog, dexp,
      norm_w.reshape(1, inner), rexp)


def _proj_residual_final_kernel(a_ref, w_ref, x_ref, nw_ref, o_ref):
    x2 = x_ref[...] + jnp.dot(a_ref[...], w_ref[...], preferred_element_type=jnp.float32)
    ms = jnp.mean(x2 * x2, axis=-1, keepdims=True)
    o_ref[...] = (x2 * lax.rsqrt(ms + NORM_EPS) * nw_ref[...]).astype(o_ref.dtype)


def _proj_residual_final(a, w, x, nw, tm):
    t, k = a.shape
    d = w.shape[1]
    return pl.pallas_call(
        _proj_residual_final_kernel,
        grid=(t // tm,),
        in_specs=[pl.BlockSpec((tm, k), lambda i: (i, 0)),
                  pl.BlockSpec((k, d), lambda i: (0, 0)),
                  pl.BlockSpec((tm, d), lambda i: (i, 0)),
                  pl.BlockSpec((1, d), lambda i: (0, 0))],
        out_specs=pl.BlockSpec((tm, d), lambda i: (i, 0)),
        out_shape=jax.ShapeDtypeStruct((t, d), jnp.float32),
        compiler_params=_cparams(("parallel",)),
    )(a, w, x, nw.reshape(1, d))


def _row_tile(t, want):
    while t % want:
        want //= 2
    return want


def kernel(x, norm_w, a_w_in, a_w_out, rel_bias, b_w_in, b_conv_w, b_conv_b, b_dt_bias, b_a_log,
           b_d, b_norm_w, b_w_out, final_norm_w):
    batch, seq, d = x.shape
    t = batch * seq
    bf16 = jnp.bfloat16
    assert norm_w.shape[0] == 2 and a_w_in.shape[0] == 1 and b_w_in.shape[0] == 1
    xf = x.reshape(t, d)

    att_w = a_w_out.shape[1]
    heads = att_w // ATT_HEAD_DIM
    iq_w = IDX_HEADS * IDX_HEAD_DIM
    wa = jnp.swapaxes(a_w_in, 1, 2).reshape(a_w_in.shape[2], d)
    n_small = wa.shape[0] - 4 * att_w - iq_w
    assert n_small == IDX_HEAD_DIM + IDX_HEADS <= LANES
    w_idx = jnp.pad(wa[4 * att_w:, :], ((0, LANES - n_small), (0, 0)))

    tm = _row_tile(seq, 2048)
    tn = _row_tile(att_w, 1024)
    tq = _row_tile(seq, ATT_Q_BLOCK)
    h0 = _rmsnorm(xf, norm_w[0], bf16, _row_tile(t, 512))
    q = _proj(h0, wa, 0, att_w, bf16, tm, tn, scale=ATT_HEAD_DIM ** -0.5 * LOG2E)
    k_hm = _proj(h0, wa, att_w, att_w, bf16, tm, tn, layout="heads")
    v_t = _proj(h0, wa, 2 * att_w, att_w, bf16, _row_tile(seq, 1024), tn, layout="t",
                batch=batch, tq=tq)
    g = _proj(h0, wa, 3 * att_w, att_w, bf16, tm, tn)
    idx = _proj(h0, w_idx, 0, iq_w + LANES, jnp.float32, _row_tile(seq, 1024), iq_w + LANES)
    att = _dsa_attention(q, g, k_hm, v_t, idx, _bias_tiles(rel_bias), batch, seq, heads, tq)
    x1, h1 = _proj_residual_norm(att, a_w_out[0].astype(bf16), xf, norm_w[1], bf16,
                                 _row_tile(t, 512))

    inner = b_w_out.shape[1]
    ssm_heads = b_dt_bias.shape[1]
    conv_ch = b_conv_w.shape[2]
    groups = (conv_ch - inner) // (2 * SSM_STATE)
    wb = jnp.swapaxes(b_w_in, 1, 2).reshape(b_w_in.shape[2], d)
    w_dt = jnp.pad(wb[inner + conv_ch:, :], ((0, LANES - ssm_heads), (0, 0)))
    zxbc = _proj(h1, wb, 0, inner + conv_ch, bf16, tm, _row_tile(inner + conv_ch, 1024))
    dt = _proj(h1, w_dt, 0, LANES, jnp.float32, tm, LANES)
    y = _ssd(zxbc, dt, b_conv_w[0], b_conv_b[0], b_dt_bias[0], b_a_log[0], b_d[0], b_norm_w[0],
             batch, seq, inner, groups)
    out = _proj_residual_final(y, b_w_out[0].astype(bf16), x1, final_norm_w, _row_tile(t, 256))
    return out.reshape(batch, seq, d)
```

```python
import functools
import math

import jax
import jax.numpy as jnp
from jax import lax
from jax.experimental import pallas as pl
from jax.experimental.pallas import tpu as pltpu

NORM_EPS = 1e-6

ATT_HEAD_DIM = 128
IDX_HEADS = 16
IDX_HEAD_DIM = 64
TOPK_MAX = 256
REL_BUCKETS = 32
REL_MAX_DIST = 128

SSM_HEAD_DIM = 64
SSM_STATE = 128
SSM_CONV = 4
SSM_CHUNK = 128

LANES = 128
SUBLANES = 8
BF16_ROWS = 16
VMEM_LIMIT_BYTES = 56 * 1024 * 1024

INT_MIN = -2 ** 31
NEG_BIG = -1e30
LOG2E = math.log2(math.e)
ATT_Q_BLOCK = 512
ATT_LOOP_CHUNKS = 4
ATT_HEADS_PER_STEP = 4

_NT = (((1,), (1,)), ((), ()))


def _cparams(sem, flags=None):
    return pltpu.CompilerParams(dimension_semantics=sem, vmem_limit_bytes=VMEM_LIMIT_BYTES,
                                flags=flags)


def _silu(x):
    h = 0.5 * x
    return h + h * jnp.tanh(h)


def _split3(x):
    hi = x.astype(jnp.bfloat16)
    r1 = x - hi.astype(jnp.float32)
    mid = r1.astype(jnp.bfloat16)
    lo = (r1 - mid.astype(jnp.float32)).astype(jnp.bfloat16)
    return hi, mid, lo


def _dot_exact_lhs(a01, x):
    hi, mid, lo = _split3(x)
    f = functools.partial(jnp.dot, preferred_element_type=jnp.float32)
    return f(a01, hi) + f(a01, mid) + f(a01, lo)


def _dot_exact_rhs(x, b01):
    hi, mid, lo = _split3(x)
    f = functools.partial(jnp.dot, preferred_element_type=jnp.float32)
    return f(hi, b01) + f(mid, b01) + f(lo, b01)


def _rmsnorm_kernel(x_ref, nw_ref, o_ref):
    x = x_ref[...]
    ms = jnp.mean(x * x, axis=-1, keepdims=True)
    o_ref[...] = (x * lax.rsqrt(ms + NORM_EPS) * nw_ref[...]).astype(o_ref.dtype)


def _rmsnorm(x, nw, out_dtype, tm):
    t, d = x.shape
    return pl.pallas_call(
        _rmsnorm_kernel,
        grid=(t // tm,),
        in_specs=[pl.BlockSpec((tm, d), lambda i: (i, 0)),
                  pl.BlockSpec((1, d), lambda i: (0, 0))],
        out_specs=pl.BlockSpec((tm, d), lambda i: (i, 0)),
        out_shape=jax.ShapeDtypeStruct((t, d), out_dtype),
        compiler_params=_cparams(("parallel",)),
    )(x, nw.reshape(1, d))


def _proj_kernel(a_ref, w_ref, o_ref, wb_ref, *, layout, scale, tq):
    @pl.when(pl.program_id(1) == 0)
    def _():
        w = w_ref[...]
        if scale != 1.0:
            w = w * scale
        wb_ref[...] = (w if layout == "t" else w.T).astype(wb_ref.dtype)

    f32 = jnp.float32
    if layout == "t":
        acc = lax.dot_general(wb_ref[...], a_ref[...], _NT, preferred_element_type=f32)
        hd = ATT_HEAD_DIM
        for hh in range(o_ref.shape[1]):
            for cc in range(o_ref.shape[2]):
                o_ref[0, hh, cc, 0:hd, :] = acc[hh * hd:(hh + 1) * hd,
                                                cc * tq:(cc + 1) * tq].astype(o_ref.dtype)
                o_ref[0, hh, cc, hd:hd + BF16_ROWS, :] = jnp.ones((BF16_ROWS, tq), o_ref.dtype)
    else:
        acc = jnp.dot(a_ref[...], wb_ref[...], preferred_element_type=f32)
        if layout == "heads":
            for j in range(o_ref.shape[0]):
                o_ref[j] = acc[:, j * LANES:(j + 1) * LANES].astype(o_ref.dtype)
        else:
            o_ref[...] = acc.astype(o_ref.dtype)


def _proj(a, wt, col_off, n, out_dtype, tm, tn, layout="rows", scale=1.0, batch=1, tq=LANES):
    t, d = a.shape
    assert col_off % tn == 0 and n % tn == 0 and t % tm == 0
    off = col_off // tn
    seq = t // batch
    mb = seq // tm
    if layout == "rows":
        out_spec = pl.BlockSpec((tm, tn), lambda j, i: (i, j))
        out_shape = (t, n)
    elif layout == "heads":
        out_spec = pl.BlockSpec((tn // LANES, tm, LANES), lambda j, i: (j, i, 0))
        out_shape = (n // LANES, t, LANES)
    else:
        rows = ATT_HEAD_DIM + BF16_ROWS
        out_spec = pl.BlockSpec((1, tn // ATT_HEAD_DIM, tm // tq, rows, tq),
                                lambda j, i: (i // mb, j, i % mb, 0, 0))
        out_shape = (batch, n // ATT_HEAD_DIM, seq // tq, rows, tq)
    wb_shape = (tn, d) if layout == "t" else (d, tn)
    return pl.pallas_call(
        functools.partial(_proj_kernel, layout=layout, scale=scale, tq=tq),
        grid=(n // tn, t // tm),
        in_specs=[pl.BlockSpec((tm, d), lambda j, i: (i, 0)),
                  pl.BlockSpec((tn, d), lambda j, i: (j + off, 0))],
        out_specs=out_spec,
        out_shape=jax.ShapeDtypeStruct(out_shape, out_dtype),
        scratch_shapes=[pltpu.VMEM(wb_shape, jnp.bfloat16)],
        compiler_params=_cparams(("parallel", "arbitrary")),
    )(a, wt)


def _proj_residual_norm_kernel(a_ref, w_ref, x_ref, nw_ref, xo_ref, ho_ref):
    x1 = x_ref[...] + jnp.dot(a_ref[...], w_ref[...], preferred_element_type=jnp.float32)
    xo_ref[...] = x1
    ms = jnp.mean(x1 * x1, axis=-1, keepdims=True)
    ho_ref[...] = (x1 * lax.rsqrt(ms + NORM_EPS) * nw_ref[...]).astype(ho_ref.dtype)


def _proj_residual_norm(a, w, x, nw, h_dtype, tm):
    t, k = a.shape
    d = w.shape[1]
    return pl.pallas_call(
        _proj_residual_norm_kernel,
        grid=(t // tm,),
        in_specs=[pl.BlockSpec((tm, k), lambda i: (i, 0)),
                  pl.BlockSpec((k, d), lambda i: (0, 0), pipeline_mode=pl.Buffered(1)),
                  pl.BlockSpec((tm, d), lambda i: (i, 0)),
                  pl.BlockSpec((1, d), lambda i: (0, 0))],
        out_specs=[pl.BlockSpec((tm, d), lambda i: (i, 0)),
                   pl.BlockSpec((tm, d), lambda i: (i, 0))],
        out_shape=[jax.ShapeDtypeStruct((t, d), jnp.float32),
                   jax.ShapeDtypeStruct((t, d), h_dtype)],
        compiler_params=_cparams(("parallel",)),
    )(a, w, x, nw.reshape(1, d))


def _dsa_kernel(q_ref, g_ref, k_ref, vt_ref, iq_ref, ikq_ref, ika_ref, bias_ref, o_ref,
                ikbd_ref, iqb_ref, madd_ref, s0_ref, s1_ref, m_ref, tied_ref, acc_ref,
                *, tq, topk, nq):
    qb = pl.program_id(1)
    h = pl.program_id(2)
    nck = qb + 1
    nb = tq // LANES
    base_d = nq * nb
    base_p = base_d + nb
    pairs = IDX_HEADS // 2
    f32 = jnp.float32
    bf16 = jnp.bfloat16
    i32 = jnp.int32
    hd = ATT_HEAD_DIM

    @pl.when((h == 0) & (qb == 0))
    def _():
        blk = ika_ref[...]
        lane = lax.broadcasted_iota(jnp.int32, blk.shape, 1)
        a = jnp.where(lane < IDX_HEAD_DIM, blk, 0.0)
        ikbd_ref[0] = a.astype(bf16)
        ikbd_ref[1] = pltpu.roll(a, IDX_HEAD_DIM, 1).astype(bf16)

    @pl.when(h == 0)
    def _():
        iqb_ref[...] = iq_ref[...].astype(bf16)
        iwt = ikq_ref[...].T[IDX_HEAD_DIM:IDX_HEAD_DIM + IDX_HEADS, :] * (
            IDX_HEADS ** -0.5 * IDX_HEAD_DIM ** -0.5)
        krow = lax.broadcasted_iota(jnp.int32, (tq, tq), 0)
        qcol = qb * tq + lax.broadcasted_iota(jnp.int32, (tq, tq), 1)

        def score_chunk(c, carry):
            start = pl.multiple_of(c * tq, tq)
            ka = ikbd_ref[0, pl.ds(start, tq), :]
            kb = ikbd_ref[1, pl.ds(start, tq), :]
            sc = jnp.zeros((tq, tq), f32)
            for j in range(pairs):
                rhs = iqb_ref[:, j * LANES:(j + 1) * LANES]
                d0 = lax.dot_general(ka, rhs, _NT, preferred_element_type=f32)
                d1 = lax.dot_general(kb, rhs, _NT, preferred_element_type=f32)
                sc = sc + jnp.maximum(d0, 0.0) * iwt[2 * j:2 * j + 1, :]
                sc = sc + jnp.maximum(d1, 0.0) * iwt[2 * j + 1:2 * j + 2, :]
            bits = lax.bitcast_convert_type(sc, i32)
            key = bits ^ ((bits >> 31) & i32(0x7FFFFFFF))
            key = jnp.where(c * tq + krow <= qcol, key, i32(INT_MIN))
            for a in range(nb):
                madd_ref[c * nb + a] = lax.bitcast_convert_type(
                    key[a * LANES:(a + 1) * LANES, :], f32)
            return carry

        lax.fori_loop(0, nck, score_chunk, 0)

        def keys_of(c, a):
            return lax.bitcast_convert_type(madd_ref[c * nb + a], i32)

        srow = lax.broadcasted_iota(i32, (LANES, tq), 0)

        def count(pred):
            def count_chunk(c, part):
                for a in range(nb):
                    pos = c * tq + a * LANES + srow
                    w = jnp.where(pred(keys_of(c, a), pos), 1.0, 0.0)
                    part = part + jnp.sum(w.reshape(LANES // SUBLANES, SUBLANES, tq), axis=0)
                return part

            part = lax.fori_loop(0, nck, count_chunk, jnp.zeros((SUBLANES, tq), f32))
            return jnp.sum(part, axis=0, keepdims=True)

        def bit_step(i, carry):
            tau_u, cnt_tau = carry
            cand_u = tau_u | lax.shift_left(i32(1), 31 - i)
            cand_s = cand_u ^ i32(INT_MIN)
            cnt = count(lambda key, pos: key >= cand_s)
            ok = cnt >= topk
            return jnp.where(ok, cand_u, tau_u), jnp.where(ok, cnt, cnt_tau)

        tau_u, cnt_tau = lax.fori_loop(0, 32, bit_step,
                                       (jnp.zeros((1, tq), i32), jnp.zeros((1, tq), f32)))
        tau = tau_u ^ i32(INT_MIN)

        def write_mask(sel_fn):
            def mask_chunk(c, carry):
                for a in range(nb):
                    key = keys_of(c, a)
                    pos = c * tq + a * LANES + srow
                    sel = sel_fn(key, pos) & (key != i32(INT_MIN))
                    madd_ref[c * nb + a] = jnp.where(sel, 0.0, NEG_BIG).astype(f32)
                return carry

            lax.fori_loop(0, nck, mask_chunk, 0)

        tied_ref[...] = jnp.broadcast_to(jnp.where(cnt_tau > topk, 1.0, 0.0), tied_ref.shape)
        n_tied_rows = jnp.sum(tied_ref[...])

        @pl.when(n_tied_rows == 0.0)
        def _():
            write_mask(lambda key, pos: key >= tau)

        @pl.when(n_tied_rows > 0.0)
        def _():
            need = topk - count(lambda key, pos: key > tau)
            nbits = max(1, (nq * tq - 1).bit_length())

            def pos_step(i, v):
                cand = v | lax.shift_left(i32(1), nbits - 1 - i)
                below = count(lambda key, pos: (key == tau) & (pos < cand))
                return jnp.where(below < need, cand, v)

            v = lax.fori_loop(0, nbits, pos_step, jnp.zeros((1, tq), i32))
            write_mask(lambda key, pos: (key > tau) | ((key == tau) & (pos <= v)))

    def attend(hh):
        dtile = bias_ref[hh, 0]
        ptile = bias_ref[hh, 1]
        for a in range(nb):
            madd_ref[base_d + a] = madd_ref[qb * nb + a]
            madd_ref[base_d + a, :, a * LANES:(a + 1) * LANES] += dtile
            if a + 1 < nb:
                madd_ref[base_d + a, :, (a + 1) * LANES:(a + 2) * LANES] += ptile
        madd_ref[base_p] = madd_ref[jnp.maximum(qb - 1, 0) * nb + nb - 1]
        madd_ref[base_p, :, 0:LANES] += ptile

        m_ref[...] = jnp.full(m_ref.shape, NEG_BIG, f32)
        acc_ref[...] = jnp.zeros(acc_ref.shape, f32)
        q = q_ref[:, hh * hd:(hh + 1) * hd]

        s_bufs = (s0_ref, s1_ref)

        def qk(c, par):
            kc = k_ref[hh, pl.ds(pl.multiple_of(c * tq, tq), tq), :]
            s = lax.dot_general(kc, q, _NT, preferred_element_type=f32)
            is_diag = c == qb
            is_prev = c == qb - 1
            cmax = None
            for a in range(nb):
                sl = jnp.where(is_diag, base_d + a, c * nb + a)
                if a == nb - 1:
                    sl = jnp.where(is_prev, base_p, sl)
                sa = s[a * LANES:(a + 1) * LANES, :] + madd_ref[sl]
                s_bufs[par][a * LANES:(a + 1) * LANES, :] = sa
                sm = jnp.max(sa.reshape(LANES // SUBLANES, SUBLANES, tq), axis=0)
                cmax = sm if cmax is None else jnp.maximum(cmax, sm)
            return jnp.max(cmax, axis=0, keepdims=True)

        def softmax_pv(c, par, cmax):
            m_prev = m_ref[...]
            m_new = jnp.maximum(m_prev, cmax)
            m_ref[...] = m_new
            p = jnp.exp2(s_bufs[par][...] - m_new).astype(bf16)
            acc_ref[...] = jnp.exp2(m_prev - m_new) * acc_ref[...] + jnp.dot(
                vt_ref[0, hh, c], p, preferred_element_type=f32)

        def run(c, n, cm, feed_next):
            for j in range(n):
                cm_next = qk(c + j + 1, (j + 1) & 1) if (j + 1 < n or feed_next) else None
                softmax_pv(c + j, j & 1, cm)
                cm = cm_next
            return cm

        unroll = ATT_LOOP_CHUNKS
        nloops = (nck - 1) // unroll
        cm = lax.fori_loop(0, nloops, lambda i, cm: run(unroll * i, unroll, cm, True), qk(0, 0))
        c0 = unroll * nloops
        for rem in range(1, unroll + 1):
            pl.when(nck - c0 == rem)(functools.partial(run, c0, rem, cm, False))

        out_t = acc_ref[0:hd, :] / acc_ref[hd:hd + 1, :]
        o_ref[:, hh * hd:(hh + 1) * hd] = (
            out_t.T * _silu(g_ref[:, hh * hd:(hh + 1) * hd].astype(f32))).astype(o_ref.dtype)

    for hh in range(k_ref.shape[0]):
        attend(hh)


def _dsa_attention(q, g, k_hm, v_t, idx, bias_tiles, batch, seq, heads, tq):
    t = q.shape[0]
    nq = seq // tq
    nb = tq // LANES
    topk = min(TOPK_MAX, seq // 4)
    iq_w = IDX_HEADS * IDX_HEAD_DIM
    small_blk = iq_w // LANES
    hd = ATT_HEAD_DIM
    vrows = hd + BF16_ROWS
    hps = ATT_HEADS_PER_STEP if heads % ATT_HEADS_PER_STEP == 0 else 1
    kern = functools.partial(_dsa_kernel, tq=tq, topk=topk, nq=nq)
    return pl.pallas_call(
        kern,
        grid=(batch, nq, heads // hps),
        in_specs=[
            pl.BlockSpec((tq, hps * hd), lambda b, i, h: (b * nq + i, h)),
            pl.BlockSpec((tq, hps * hd), lambda b, i, h: (b * nq + i, h)),
            pl.BlockSpec((hps, seq, hd), lambda b, i, h: (h, b, 0)),
            pl.BlockSpec((1, hps, nq, vrows, tq), lambda b, i, h: (b, h, 0, 0, 0)),
            pl.BlockSpec((tq, iq_w), lambda b, i, h: (b * nq + i, 0)),
            pl.BlockSpec((tq, LANES), lambda b, i, h: (b * nq + i, small_blk)),
            pl.BlockSpec((seq, LANES), lambda b, i, h: (b, small_blk)),
            pl.BlockSpec((hps, 2, LANES, LANES), lambda b, i, h: (h, 0, 0, 0)),
        ],
        out_specs=pl.BlockSpec((tq, hps * hd), lambda b, i, h: (b * nq + i, h)),
        out_shape=jax.ShapeDtypeStruct((t, heads * hd), jnp.bfloat16),
        scratch_shapes=[
            pltpu.VMEM((2, seq, LANES), jnp.bfloat16),
            pltpu.VMEM((tq, iq_w), jnp.bfloat16),
            pltpu.VMEM((nq * nb + nb + 1, LANES, tq), jnp.float32),
            pltpu.VMEM((tq, tq), jnp.float32),
            pltpu.VMEM((tq, tq), jnp.float32),
            pltpu.VMEM((1, tq), jnp.float32),
            pltpu.VMEM((SUBLANES, tq), jnp.float32),
            pltpu.VMEM((vrows, tq), jnp.float32),
        ],
        compiler_params=_cparams(("arbitrary", "arbitrary", "arbitrary")),
    )(q, g, k_hm, v_t, idx, idx, idx, bias_tiles)


def _rel_bucket_of(n):
    max_exact = REL_BUCKETS // 2
    nf = jnp.maximum(n, 1).astype(jnp.float32)
    large = max_exact + (jnp.log(nf / max_exact) / math.log(REL_MAX_DIST / max_exact)
                         * (REL_BUCKETS - max_exact)).astype(jnp.int32)
    large = jnp.minimum(large, REL_BUCKETS - 1)
    return jnp.where(n < max_exact, n, large)


def _bias_tiles(rel_bias):
    assert REL_MAX_DIST <= LANES
    n = LANES
    heads = rel_bias.shape[1]
    dist = jnp.arange(2 * n, dtype=jnp.int32)
    bucket = jnp.where(dist >= REL_MAX_DIST, REL_BUCKETS - 1, _rel_bucket_of(dist))
    bv = (rel_bias[bucket] - rel_bias[REL_BUCKETS - 1][None, :]) * LOG2E
    rows = jnp.concatenate([bv.T, jnp.zeros((heads, n), bv.dtype)], axis=1)
    flat = jnp.broadcast_to(rows[:, None, :], (heads, n, 3 * n)).reshape(heads, 3 * n * n)
    toep = flat[:, :n * (3 * n - 1)].reshape(heads, n, 3 * n - 1)[:, :, :2 * n]
    return jnp.stack([toep[:, :, :n], toep[:, :, n:]], axis=1).astype(jnp.float32)


def _ssd_kernel(z_ref, x_ref, bc_ref, dt_ref, shift_ref, cw_ref, cbb_ref, dtb_ref, alog_ref,
                dexp_ref, nw_ref, rexp_ref, o_ref, xe_ref, taps_ref, state_ref, y_ref,
                *, inner, groups):
    f32 = jnp.float32
    bf16 = jnp.bfloat16
    L = SSM_CHUNK
    P = SSM_HEAD_DIM
    N = SSM_STATE
    heads = inner // P
    hpg = heads // groups
    gw = hpg * P
    gn = groups * N
    halo = BF16_ROWS
    ext = halo + L
    dot = functools.partial(jnp.dot, preferred_element_type=f32)

    @pl.when(pl.program_id(1) == 0)
    def _():
        xe_ref[0:halo, :] = jnp.zeros((halo, xe_ref.shape[1]), bf16)
        taps_ref[SSM_CONV * ext:SSM_CONV * ext + halo, :] = cbb_ref[...]
        state_ref[...] = jnp.zeros(state_ref.shape, f32)

    xe_ref[halo:ext, 0:inner] = x_ref[...]
    xe_ref[halo:ext, inner:inner + 2 * gn] = bc_ref[...]
    xe = xe_ref[...]
    for j in range(SSM_CONV):
        taps_ref[j * ext:(j + 1) * ext, :] = xe * cw_ref[j:j + 1, :]
    xe_ref[0:halo, :] = xe_ref[L:ext, :]
    shift = shift_ref[...]
    bcm = _silu(dot(shift, taps_ref[:, inner:inner + 2 * gn]))

    dtr = dt_ref[...] + dtb_ref[...]
    dt = jnp.maximum(dtr, 0.0) + jnp.log1p(jnp.exp(-jnp.abs(dtr)))
    adt = dt * (-jnp.exp(alog_ref[...]))
    ri = lax.broadcasted_iota(jnp.int32, (L, L), 0)
    ci = lax.broadcasted_iota(jnp.int32, (L, L), 1)
    tri = ri >= ci
    cs = _dot_exact_lhs(jnp.where(tri, 1.0, 0.0).astype(bf16), adt)
    cst = cs.T
    csl = cs[L - 1:L, :]
    dt_hi = dt.astype(bf16)
    dt_lo = (dt - dt_hi.astype(f32)).astype(bf16)
    ecs_b = jnp.exp(cs).astype(bf16)
    dec_b = jnp.exp(csl - cs).astype(bf16)
    cdec = jnp.broadcast_to(jnp.exp(csl), (SUBLANES, LANES))
    lane = lax.broadcasted_iota(jnp.int32, (L, LANES), 1)

    ssq = jnp.zeros((L, 1), f32)
    for g in range(groups):
        sl = slice(g * gw, (g + 1) * gw)
        rex = rexp_ref[:, sl]
        xs = _silu(dot(shift, taps_ref[:, sl]))
        bg = bcm[:, g * N:(g + 1) * N]
        cg = bcm[:, gn + g * N:gn + (g + 1) * N].astype(bf16)
        xdt = xs * (dot(dt_hi, rex) + dot(dt_lo, rex))
        xdt_b = xdt.astype(bf16)
        xdec_b = (xdt * dot(dec_b, rex)).astype(bf16)
        gmat = lax.dot_general(cg, bg.astype(bf16), _NT, preferred_element_type=f32)
        bgt = bg.T.astype(bf16)
        st_prev = state_ref[g]
        y = dot(cg, st_prev.astype(bf16)) * dot(ecs_b, rex) + xs * dexp_ref[:, sl]
        state_ref[g] = st_prev * _dot_exact_rhs(cdec, rex)[0:1, :] + dot(bgt, xdec_b)
        diag = []
        for pr in range(gw // LANES):
            xp = xdt_b[:, pr * LANES:(pr + 1) * LANES]
            lhs, rhs = [], []
            for sub in range(LANES // P):
                hh = (g * gw + pr * LANES) // P + sub
                seg = cs[:, hh:hh + 1] - cst[hh:hh + 1, :]
                lm = jnp.exp(jnp.where(tri, seg, -jnp.inf))
                lhs.append((gmat * lm).astype(bf16))
                rhs.append(jnp.where((lane >= sub * P) & (lane < (sub + 1) * P), xp,
                                     jnp.zeros_like(xp)))
            diag.append(dot(jnp.concatenate(lhs, axis=1), jnp.concatenate(rhs, axis=0)))
        yg = (y + jnp.concatenate(diag, axis=1)) * _silu(z_ref[:, sl].astype(f32))
        ssq = ssq + jnp.sum(yg * yg, axis=-1, keepdims=True)
        y_ref[:, sl] = yg

    scale = lax.rsqrt(ssq * (1.0 / inner) + NORM_EPS)
    o_ref[...] = (y_ref[...] * scale * nw_ref[...]).astype(o_ref.dtype)


def _ssd(zxbc, dt, conv_w, conv_b, dt_bias, a_log, d_skip, norm_w, batch, seq, inner, groups):
    t = zxbc.shape[0]
    L = SSM_CHUNK
    heads = inner // SSM_HEAD_DIM
    gn = groups * SSM_STATE
    conv_ch = inner + 2 * gn
    assert heads <= LANES and LANES % SSM_HEAD_DIM == 0 and inner % (2 * gn) == 0
    nc = seq // L
    pad_h = LANES - heads
    dtb = jnp.pad(dt_bias, (0, pad_h)).reshape(1, LANES)
    alog = jnp.pad(a_log, (0, pad_h)).reshape(1, LANES)
    dexp = jnp.repeat(d_skip, SSM_HEAD_DIM).reshape(1, inner)
    rexp = (jnp.arange(LANES, dtype=jnp.int32)[:, None]
            == (jnp.arange(inner, dtype=jnp.int32) // SSM_HEAD_DIM)[None, :]).astype(jnp.bfloat16)
    bc_blk = inner * 2 // (2 * gn)
    halo = BF16_ROWS
    ext = halo + L
    rows = jnp.arange(L, dtype=jnp.int32)[:, None]
    cols = jnp.arange(SSM_CONV * ext + halo, dtype=jnp.int32)[None, :]
    blk, pos = cols // ext, cols % ext
    shift = jnp.where(blk < SSM_CONV, pos == halo + rows - (SSM_CONV - 1 - blk),
                      pos < 2).astype(jnp.bfloat16)
    cb_hi = conv_b.astype(jnp.bfloat16)
    cb_lo = (conv_b - cb_hi.astype(jnp.float32)).astype(jnp.bfloat16)
    cbb = jnp.zeros((halo, conv_ch), jnp.bfloat16).at[0].set(cb_hi).at[1].set(cb_lo)
    kern = functools.partial(_ssd_kernel, inner=inner, groups=groups)
    const = lambda b, c: (0, 0)
    return pl.pallas_call(
        kern,
        grid=(batch, nc),
        in_specs=[
            pl.BlockSpec((L, inner), lambda b, c: (b * nc + c, 0)),
            pl.BlockSpec((L, inner), lambda b, c: (b * nc + c, 1)),
            pl.BlockSpec((L, 2 * gn), lambda b, c: (b * nc + c, bc_blk)),
            pl.BlockSpec((L, LANES), lambda b, c: (b * nc + c, 0)),
            pl.BlockSpec((L, SSM_CONV * ext + halo), const),
            pl.BlockSpec((SSM_CONV, conv_ch), const),
            pl.BlockSpec((halo, conv_ch), const),
            pl.BlockSpec((1, LANES), const),
            pl.BlockSpec((1, LANES), const),
            pl.BlockSpec((1, inner), const),
            pl.BlockSpec((1, inner), const),
            pl.BlockSpec((LANES, inner), const),
        ],
        out_specs=pl.BlockSpec((L, inner), lambda b, c: (b * nc + c, 0)),
        out_shape=jax.ShapeDtypeStruct((t, inner), jnp.bfloat16),
        scratch_shapes=[
            pltpu.VMEM((ext, conv_ch), jnp.bfloat16),
            pltpu.VMEM((SSM_CONV * ext + halo, conv_ch), jnp.bfloat16),
            pltpu.VMEM((groups, SSM_STATE, inner // groups), jnp.float32),
            pltpu.VMEM((L, inner), jnp.float32),
        ],
        compiler_params=_cparams(("arbitrary", "arbitrary")),
    )(zxbc, zxbc, zxbc, dt, shift, conv_w.astype(jnp.bfloat16), cbb, dtb, alog, dexp,
      norm_w.reshape(1, inner), rexp)


def _proj_residual_final_kernel(a_ref, w_ref, x_ref, nw_ref, o_ref):
    x2 = x_ref[...] + jnp.dot(a_ref[...], w_ref[...], preferred_element_type=jnp.float32)
    ms = jnp.mean(x2 * x2, axis=-1, keepdims=True)
    o_ref[...] = (x2 * lax.rsqrt(ms + NORM_EPS) * nw_ref[...]).astype(o_ref.dtype)


def _proj_residual_final(a, w, x, nw, tm):
    t, k = a.shape
    d = w.shape[1]
    return pl.pallas_call(
        _proj_residual_final_kernel,
        grid=(t // tm,),
        in_specs=[pl.BlockSpec((tm, k), lambda i: (i, 0)),
                  pl.BlockSpec((k, d), lambda i: (0, 0), pipeline_mode=pl.Buffered(1)),
                  pl.BlockSpec((tm, d), lambda i: (i, 0)),
                  pl.BlockSpec((1, d), lambda i: (0, 0))],
        out_specs=pl.BlockSpec((tm, d), lambda i: (i, 0)),
        out_shape=jax.ShapeDtypeStruct((t, d), jnp.float32),
        compiler_params=_cparams(("parallel",)),
    )(a, w, x, nw.reshape(1, d))


def _row_tile(t, want):
    while t % want:
        want //= 2
    return want


def kernel(x, norm_w, a_w_in, a_w_out, rel_bias, b_w_in, b_conv_w, b_conv_b, b_dt_bias, b_a_log,
           b_d, b_norm_w, b_w_out, final_norm_w):
    batch, seq, d = x.shape
    t = batch * seq
    bf16 = jnp.bfloat16
    assert norm_w.shape[0] == 2 and a_w_in.shape[0] == 1 and b_w_in.shape[0] == 1
    xf = x.reshape(t, d)

    att_w = a_w_out.shape[1]
    heads = att_w // ATT_HEAD_DIM
    iq_w = IDX_HEADS * IDX_HEAD_DIM
    wa = jnp.swapaxes(a_w_in, 1, 2).reshape(a_w_in.shape[2], d)
    n_small = wa.shape[0] - 4 * att_w - iq_w
    assert n_small == IDX_HEAD_DIM + IDX_HEADS <= LANES
    w_idx = jnp.pad(wa[4 * att_w:, :], ((0, LANES - n_small), (0, 0)))

    tm = _row_tile(seq, 2048)
    tn = _row_tile(att_w, 1024)
    tq = _row_tile(seq, ATT_Q_BLOCK)
    h0 = _rmsnorm(xf, norm_w[0], bf16, _row_tile(t, 512))
    q = _proj(h0, wa, 0, att_w, bf16, tm, tn, scale=ATT_HEAD_DIM ** -0.5 * LOG2E)
    k_hm = _proj(h0, wa, att_w, att_w, bf16, tm, tn, layout="heads")
    v_t = _proj(h0, wa, 2 * att_w, att_w, bf16, _row_tile(seq, 1024), tn, layout="t",
                batch=batch, tq=tq)
    g = _proj(h0, wa, 3 * att_w, att_w, bf16, tm, tn)
    idx = _proj(h0, w_idx, 0, iq_w + LANES, jnp.float32, _row_tile(seq, 1024), iq_w + LANES)
    att = _dsa_attention(q, g, k_hm, v_t, idx, _bias_tiles(rel_bias), batch, seq, heads, tq)
    x1, h1 = _proj_residual_norm(att, a_w_out[0].astype(bf16), xf, norm_w[1], bf16,
                                 _row_tile(t, 512))

    inner = b_w_out.shape[1]
    ssm_heads = b_dt_bias.shape[1]
    conv_ch = b_conv_w.shape[2]
    groups = (conv_ch - inner) // (2 * SSM_STATE)
    wb = jnp.swapaxes(b_w_in, 1, 2).reshape(b_w_in.shape[2], d)
    w_dt = jnp.pad(wb[inner + conv_ch:, :], ((0, LANES - ssm_heads), (0, 0)))
    zxbc = _proj(h1, wb, 0, inner + conv_ch, bf16, tm, _row_tile(inner + conv_ch, 1024))
    dt = _proj(h1, w_dt, 0, LANES, jnp.float32, tm, LANES)
    y = _ssd(zxbc, dt, b_conv_w[0], b_conv_b[0], b_dt_bias[0], b_a_log[0], b_d[0], b_norm_w[0],
             batch, seq, inner, groups)
    out = _proj_residual_final(y, b_w_out[0].astype(bf16), x1, final_norm_w, _row_tile(t, 512))
    return out.reshape(batch, seq, d)
```

```python
import functools
import math

import jax
import jax.numpy as jnp
from jax import lax
from jax.experimental import pallas as pl
from jax.experimental.pallas import tpu as pltpu

NORM_EPS = 1e-6

ATT_HEAD_DIM = 128
IDX_HEADS = 16
IDX_HEAD_DIM = 64
TOPK_MAX = 256
REL_BUCKETS = 32
REL_MAX_DIST = 128

SSM_HEAD_DIM = 64
SSM_STATE = 128
SSM_CONV = 4
SSM_CHUNK = 128

LANES = 128
SUBLANES = 8
BF16_ROWS = 16
VMEM_LIMIT_BYTES = 56 * 1024 * 1024

INT_MIN = -2 ** 31
NEG_BIG = -1e30
LOG2E = math.log2(math.e)
ATT_Q_BLOCK = 512
ATT_LOOP_CHUNKS = 4
ATT_HEADS_PER_STEP = 4

_NT = (((1,), (1,)), ((), ()))


def _cparams(sem):
    return pltpu.CompilerParams(dimension_semantics=sem, vmem_limit_bytes=VMEM_LIMIT_BYTES)


def _silu(x):
    h = 0.5 * x
    return h + h * jnp.tanh(h)


def _split3(x):
    hi = x.astype(jnp.bfloat16)
    r1 = x - hi.astype(jnp.float32)
    mid = r1.astype(jnp.bfloat16)
    lo = (r1 - mid.astype(jnp.float32)).astype(jnp.bfloat16)
    return hi, mid, lo


def _dot_exact_lhs(a01, x):
    hi, mid, lo = _split3(x)
    f = functools.partial(jnp.dot, preferred_element_type=jnp.float32)
    return f(a01, hi) + f(a01, mid) + f(a01, lo)


def _dot_exact_rhs(x, b01):
    hi, mid, lo = _split3(x)
    f = functools.partial(jnp.dot, preferred_element_type=jnp.float32)
    return f(hi, b01) + f(mid, b01) + f(lo, b01)


def _rmsnorm_kernel(x_ref, nw_ref, o_ref):
    x = x_ref[...]
    ms = jnp.mean(x * x, axis=-1, keepdims=True)
    o_ref[...] = (x * lax.rsqrt(ms + NORM_EPS) * nw_ref[...]).astype(o_ref.dtype)


def _rmsnorm(x, nw, out_dtype, tm):
    t, d = x.shape
    return pl.pallas_call(
        _rmsnorm_kernel,
        grid=(t // tm,),
        in_specs=[pl.BlockSpec((tm, d), lambda i: (i, 0)),
                  pl.BlockSpec((1, d), lambda i: (0, 0))],
        out_specs=pl.BlockSpec((tm, d), lambda i: (i, 0)),
        out_shape=jax.ShapeDtypeStruct((t, d), out_dtype),
        compiler_params=_cparams(("parallel",)),
    )(x, nw.reshape(1, d))


def _proj_kernel(a_ref, w_ref, o_ref, wb_ref, *, layout, scale, tq):
    @pl.when(pl.program_id(1) == 0)
    def _():
        w = w_ref[...]
        if scale != 1.0:
            w = w * scale
        wb_ref[...] = (w if layout == "t" else w.T).astype(wb_ref.dtype)

    f32 = jnp.float32
    if layout == "t":
        acc = lax.dot_general(wb_ref[...], a_ref[...], _NT, preferred_element_type=f32)
        hd = ATT_HEAD_DIM
        for hh in range(o_ref.shape[1]):
            for cc in range(o_ref.shape[2]):
                o_ref[0, hh, cc, 0:hd, :] = acc[hh * hd:(hh + 1) * hd,
                                                cc * tq:(cc + 1) * tq].astype(o_ref.dtype)
                o_ref[0, hh, cc, hd:hd + BF16_ROWS, :] = jnp.ones((BF16_ROWS, tq), o_ref.dtype)
    else:
        acc = jnp.dot(a_ref[...], wb_ref[...], preferred_element_type=f32)
        if layout == "heads":
            for j in range(o_ref.shape[0]):
                o_ref[j] = acc[:, j * LANES:(j + 1) * LANES].astype(o_ref.dtype)
        else:
            o_ref[...] = acc.astype(o_ref.dtype)


def _proj(a, wt, col_off, n, out_dtype, tm, tn, layout="rows", scale=1.0, batch=1, tq=LANES):
    t, d = a.shape
    assert col_off % tn == 0 and n % tn == 0 and t % tm == 0
    off = col_off // tn
    seq = t // batch
    mb = seq // tm
    if layout == "rows":
        out_spec = pl.BlockSpec((tm, tn), lambda j, i: (i, j))
        out_shape = (t, n)
    elif layout == "heads":
        out_spec = pl.BlockSpec((tn // LANES, tm, LANES), lambda j, i: (j, i, 0))
        out_shape = (n // LANES, t, LANES)
    else:
        rows = ATT_HEAD_DIM + BF16_ROWS
        out_spec = pl.BlockSpec((1, tn // ATT_HEAD_DIM, tm // tq, rows, tq),
                                lambda j, i: (i // mb, j, i % mb, 0, 0))
        out_shape = (batch, n // ATT_HEAD_DIM, seq // tq, rows, tq)
    wb_shape = (tn, d) if layout == "t" else (d, tn)
    return pl.pallas_call(
        functools.partial(_proj_kernel, layout=layout, scale=scale, tq=tq),
        grid=(n // tn, t // tm),
        in_specs=[pl.BlockSpec((tm, d), lambda j, i: (i, 0)),
                  pl.BlockSpec((tn, d), lambda j, i: (j + off, 0))],
        out_specs=out_spec,
        out_shape=jax.ShapeDtypeStruct(out_shape, out_dtype),
        scratch_shapes=[pltpu.VMEM(wb_shape, jnp.bfloat16)],
        compiler_params=_cparams(("parallel", "arbitrary")),
    )(a, wt)


def _proj_residual_norm_kernel(a_ref, w_ref, x_ref, nw_ref, xo_ref, ho_ref):
    x1 = x_ref[...] + jnp.dot(a_ref[...], w_ref[...], preferred_element_type=jnp.float32)
    xo_ref[...] = x1
    ms = jnp.mean(x1 * x1, axis=-1, keepdims=True)
    ho_ref[...] = (x1 * lax.rsqrt(ms + NORM_EPS) * nw_ref[...]).astype(ho_ref.dtype)


def _proj_residual_norm(a, w, x, nw, h_dtype, tm):
    t, k = a.shape
    d = w.shape[1]
    return pl.pallas_call(
        _proj_residual_norm_kernel,
        grid=(t // tm,),
        in_specs=[pl.BlockSpec((tm, k), lambda i: (i, 0)),
                  pl.BlockSpec((k, d), lambda i: (0, 0), pipeline_mode=pl.Buffered(1)),
                  pl.BlockSpec((tm, d), lambda i: (i, 0)),
                  pl.BlockSpec((1, d), lambda i: (0, 0))],
        out_specs=[pl.BlockSpec((tm, d), lambda i: (i, 0)),
                   pl.BlockSpec((tm, d), lambda i: (i, 0))],
        out_shape=[jax.ShapeDtypeStruct((t, d), jnp.float32),
                   jax.ShapeDtypeStruct((t, d), h_dtype)],
        compiler_params=_cparams(("parallel",)),
    )(a, w, x, nw.reshape(1, d))


def _dsa_kernel(q_ref, g_ref, k_ref, vt_ref, iq_ref, ikq_ref, ika_ref, bias_ref, o_ref,
                ikbd_ref, iqb_ref, madd_ref, s0_ref, s1_ref, m_ref, tied_ref, acc_ref,
                *, tq, topk, nq):
    qb = pl.program_id(1)
    h = pl.program_id(2)
    nck = qb + 1
    nb = tq // LANES
    base_d = nq * nb
    base_p = base_d + nb
    pairs = IDX_HEADS // 2
    f32 = jnp.float32
    bf16 = jnp.bfloat16
    i32 = jnp.int32
    hd = ATT_HEAD_DIM

    @pl.when((h == 0) & (qb == 0))
    def _():
        blk = ika_ref[...]
        lane = lax.broadcasted_iota(jnp.int32, blk.shape, 1)
        a = jnp.where(lane < IDX_HEAD_DIM, blk, 0.0)
        ikbd_ref[0] = a.astype(bf16)
        ikbd_ref[1] = pltpu.roll(a, IDX_HEAD_DIM, 1).astype(bf16)

    @pl.when(h == 0)
    def _():
        iqb_ref[...] = iq_ref[...].astype(bf16)
        iwt = ikq_ref[...].T[IDX_HEAD_DIM:IDX_HEAD_DIM + IDX_HEADS, :] * (
            IDX_HEADS ** -0.5 * IDX_HEAD_DIM ** -0.5)
        krow = lax.broadcasted_iota(jnp.int32, (tq, tq), 0)
        qcol = qb * tq + lax.broadcasted_iota(jnp.int32, (tq, tq), 1)

        def score_chunk(c, carry):
            start = pl.multiple_of(c * tq, tq)
            ka = ikbd_ref[0, pl.ds(start, tq), :]
            kb = ikbd_ref[1, pl.ds(start, tq), :]
            sc = jnp.zeros((tq, tq), f32)
            for j in range(pairs):
                rhs = iqb_ref[:, j * LANES:(j + 1) * LANES]
                d0 = lax.dot_general(ka, rhs, _NT, preferred_element_type=f32)
                d1 = lax.dot_general(kb, rhs, _NT, preferred_element_type=f32)
                sc = sc + jnp.maximum(d0, 0.0) * iwt[2 * j:2 * j + 1, :]
                sc = sc + jnp.maximum(d1, 0.0) * iwt[2 * j + 1:2 * j + 2, :]
            bits = lax.bitcast_convert_type(sc, i32)
            key = bits ^ ((bits >> 31) & i32(0x7FFFFFFF))
            key = jnp.where(c * tq + krow <= qcol, key, i32(INT_MIN))
            for a in range(nb):
                madd_ref[c * nb + a] = lax.bitcast_convert_type(
                    key[a * LANES:(a + 1) * LANES, :], f32)
            return carry

        lax.fori_loop(0, nck, score_chunk, 0)

        def keys_of(c, a):
            return lax.bitcast_convert_type(madd_ref[c * nb + a], i32)

        srow = lax.broadcasted_iota(i32, (LANES, tq), 0)

        def count(pred):
            def count_chunk(c, part):
                for a in range(nb):
                    pos = c * tq + a * LANES + srow
                    w = jnp.where(pred(keys_of(c, a), pos), 1.0, 0.0)
                    part = part + jnp.sum(w.reshape(LANES // SUBLANES, SUBLANES, tq), axis=0)
                return part

            part = lax.fori_loop(0, nck, count_chunk, jnp.zeros((SUBLANES, tq), f32))
            return jnp.sum(part, axis=0, keepdims=True)

        def bit_step(i, carry):
            tau_u, cnt_tau = carry
            cand_u = tau_u | lax.shift_left(i32(1), 31 - i)
            cand_s = cand_u ^ i32(INT_MIN)
            cnt = count(lambda key, pos: key >= cand_s)
            ok = cnt >= topk
            return jnp.where(ok, cand_u, tau_u), jnp.where(ok, cnt, cnt_tau)

        tau_u, cnt_tau = lax.fori_loop(0, 32, bit_step,
                                       (jnp.zeros((1, tq), i32), jnp.zeros((1, tq), f32)))
        tau = tau_u ^ i32(INT_MIN)

        def write_mask(sel_fn):
            def mask_chunk(c, carry):
                for a in range(nb):
                    key = keys_of(c, a)
                    pos = c * tq + a * LANES + srow
                    sel = sel_fn(key, pos) & (key != i32(INT_MIN))
                    madd_ref[c * nb + a] = jnp.where(sel, 0.0, NEG_BIG).astype(f32)
                return carry

            lax.fori_loop(0, nck, mask_chunk, 0)

        tied_ref[...] = jnp.broadcast_to(jnp.where(cnt_tau > topk, 1.0, 0.0), tied_ref.shape)
        n_tied_rows = jnp.sum(tied_ref[...])

        @pl.when(n_tied_rows == 0.0)
        def _():
            write_mask(lambda key, pos: key >= tau)

        @pl.when(n_tied_rows > 0.0)
        def _():
            need = topk - count(lambda key, pos: key > tau)
            nbits = max(1, (nq * tq - 1).bit_length())

            def pos_step(i, v):
                cand = v | lax.shift_left(i32(1), nbits - 1 - i)
                below = count(lambda key, pos: (key == tau) & (pos < cand))
                return jnp.where(below < need, cand, v)

            v = lax.fori_loop(0, nbits, pos_step, jnp.zeros((1, tq), i32))
            write_mask(lambda key, pos: (key > tau) | ((key == tau) & (pos <= v)))

    def attend(hh):
        dtile = bias_ref[hh, 0]
        ptile = bias_ref[hh, 1]
        def biased(slab, tiles):
            cols = [slab[:, b * LANES:(b + 1) * LANES] for b in range(nb)]
            for b, tile in tiles:
                cols[b] = cols[b] + tile
            return jnp.concatenate(cols, axis=1)

        for a in range(nb):
            tiles = [(a, dtile)] + ([(a + 1, ptile)] if a + 1 < nb else [])
            madd_ref[base_d + a] = biased(madd_ref[qb * nb + a], tiles)
        madd_ref[base_p] = biased(madd_ref[jnp.maximum(qb - 1, 0) * nb + nb - 1], [(0, ptile)])

        m_ref[...] = jnp.full(m_ref.shape, NEG_BIG, f32)
        acc_ref[...] = jnp.zeros(acc_ref.shape, f32)
        q = q_ref[:, hh * hd:(hh + 1) * hd]

        s_bufs = (s0_ref, s1_ref)

        def qk(c, par):
            kc = k_ref[hh, pl.ds(pl.multiple_of(c * tq, tq), tq), :]
            s = lax.dot_general(kc, q, _NT, preferred_element_type=f32)
            is_diag = c == qb
            is_prev = c == qb - 1
            cmax = None
            for a in range(nb):
                sl = jnp.where(is_diag, base_d + a, c * nb + a)
                if a == nb - 1:
                    sl = jnp.where(is_prev, base_p, sl)
                sa = s[a * LANES:(a + 1) * LANES, :] + madd_ref[sl]
                s_bufs[par][a * LANES:(a + 1) * LANES, :] = sa
                sm = jnp.max(sa.reshape(LANES // SUBLANES, SUBLANES, tq), axis=0)
                cmax = sm if cmax is None else jnp.maximum(cmax, sm)
            return jnp.max(cmax, axis=0, keepdims=True)

        def softmax_pv(c, par, cmax):
            m_prev = m_ref[...]
            m_new = jnp.maximum(m_prev, cmax)
            m_ref[...] = m_new
            p = jnp.exp2(s_bufs[par][...] - m_new).astype(bf16)
            acc_ref[...] = jnp.exp2(m_prev - m_new) * acc_ref[...] + jnp.dot(
                vt_ref[0, hh, c], p, preferred_element_type=f32)

        def run(c, n, cm, feed_next):
            for j in range(n):
                cm_next = qk(c + j + 1, (j + 1) & 1) if (j + 1 < n or feed_next) else None
                softmax_pv(c + j, j & 1, cm)
                cm = cm_next
            return cm

        unroll = ATT_LOOP_CHUNKS
        nloops = (nck - 1) // unroll
        cm = lax.fori_loop(0, nloops, lambda i, cm: run(unroll * i, unroll, cm, True), qk(0, 0))
        c0 = unroll * nloops
        for rem in range(1, unroll + 1):
            pl.when(nck - c0 == rem)(functools.partial(run, c0, rem, cm, False))

        out_t = acc_ref[0:hd, :] / acc_ref[hd:hd + 1, :]
        o_ref[:, hh * hd:(hh + 1) * hd] = (
            out_t.T * _silu(g_ref[:, hh * hd:(hh + 1) * hd].astype(f32))).astype(o_ref.dtype)

    for hh in range(k_ref.shape[0]):
        attend(hh)


def _dsa_attention(q, g, k_hm, v_t, idx, bias_tiles, batch, seq, heads, tq):
    t = q.shape[0]
    nq = seq // tq
    nb = tq // LANES
    topk = min(TOPK_MAX, seq // 4)
    iq_w = IDX_HEADS * IDX_HEAD_DIM
    small_blk = iq_w // LANES
    hd = ATT_HEAD_DIM
    vrows = hd + BF16_ROWS
    hps = ATT_HEADS_PER_STEP if heads % ATT_HEADS_PER_STEP == 0 else 1
    kern = functools.partial(_dsa_kernel, tq=tq, topk=topk, nq=nq)
    return pl.pallas_call(
        kern,
        grid=(batch, nq, heads // hps),
        in_specs=[
            pl.BlockSpec((tq, hps * hd), lambda b, i, h: (b * nq + i, h)),
            pl.BlockSpec((tq, hps * hd), lambda b, i, h: (b * nq + i, h)),
            pl.BlockSpec((hps, seq, hd), lambda b, i, h: (h, b, 0)),
            pl.BlockSpec((1, hps, nq, vrows, tq), lambda b, i, h: (b, h, 0, 0, 0)),
            pl.BlockSpec((tq, iq_w), lambda b, i, h: (b * nq + i, 0)),
            pl.BlockSpec((tq, LANES), lambda b, i, h: (b * nq + i, small_blk)),
            pl.BlockSpec((seq, LANES), lambda b, i, h: (b, small_blk)),
            pl.BlockSpec((hps, 2, LANES, LANES), lambda b, i, h: (h, 0, 0, 0)),
        ],
        out_specs=pl.BlockSpec((tq, hps * hd), lambda b, i, h: (b * nq + i, h)),
        out_shape=jax.ShapeDtypeStruct((t, heads * hd), jnp.bfloat16),
        scratch_shapes=[
            pltpu.VMEM((2, seq, LANES), jnp.bfloat16),
            pltpu.VMEM((tq, iq_w), jnp.bfloat16),
            pltpu.VMEM((nq * nb + nb + 1, LANES, tq), jnp.float32),
            pltpu.VMEM((tq, tq), jnp.float32),
            pltpu.VMEM((tq, tq), jnp.float32),
            pltpu.VMEM((1, tq), jnp.float32),
            pltpu.VMEM((SUBLANES, tq), jnp.float32),
            pltpu.VMEM((vrows, tq), jnp.float32),
        ],
        compiler_params=_cparams(("arbitrary", "arbitrary", "arbitrary")),
    )(q, g, k_hm, v_t, idx, idx, idx, bias_tiles)


def _rel_bucket_of(n):
    max_exact = REL_BUCKETS // 2
    nf = jnp.maximum(n, 1).astype(jnp.float32)
    large = max_exact + (jnp.log(nf / max_exact) / math.log(REL_MAX_DIST / max_exact)
                         * (REL_BUCKETS - max_exact)).astype(jnp.int32)
    large = jnp.minimum(large, REL_BUCKETS - 1)
    return jnp.where(n < max_exact, n, large)


def _bias_tiles(rel_bias):
    assert REL_MAX_DIST <= LANES
    n = LANES
    heads = rel_bias.shape[1]
    dist = jnp.arange(2 * n, dtype=jnp.int32)
    bucket = jnp.where(dist >= REL_MAX_DIST, REL_BUCKETS - 1, _rel_bucket_of(dist))
    bv = (rel_bias[bucket] - rel_bias[REL_BUCKETS - 1][None, :]) * LOG2E
    rows = jnp.concatenate([bv.T, jnp.zeros((heads, n), bv.dtype)], axis=1)
    flat = jnp.broadcast_to(rows[:, None, :], (heads, n, 3 * n)).reshape(heads, 3 * n * n)
    toep = flat[:, :n * (3 * n - 1)].reshape(heads, n, 3 * n - 1)[:, :, :2 * n]
    return jnp.stack([toep[:, :, :n], toep[:, :, n:]], axis=1).astype(jnp.float32)


def _ssd_kernel(z_ref, x_ref, bc_ref, dt_ref, shift_ref, cw_ref, cbb_ref, dtb_ref, alog_ref,
                dexp_ref, nw_ref, rexp_ref, o_ref, xe_ref, taps_ref, state_ref, y_ref,
                *, inner, groups):
    f32 = jnp.float32
    bf16 = jnp.bfloat16
    L = SSM_CHUNK
    P = SSM_HEAD_DIM
    N = SSM_STATE
    heads = inner // P
    hpg = heads // groups
    gw = hpg * P
    gn = groups * N
    halo = BF16_ROWS
    ext = halo + L
    dot = functools.partial(jnp.dot, preferred_element_type=f32)

    @pl.when(pl.program_id(1) == 0)
    def _():
        xe_ref[0:halo, :] = jnp.zeros((halo, xe_ref.shape[1]), bf16)
        taps_ref[SSM_CONV * ext:SSM_CONV * ext + halo, :] = cbb_ref[...]
        state_ref[...] = jnp.zeros(state_ref.shape, f32)

    xe_ref[halo:ext, 0:inner] = x_ref[...]
    xe_ref[halo:ext, inner:inner + 2 * gn] = bc_ref[...]
    xe = xe_ref[...]
    for j in range(SSM_CONV):
        taps_ref[j * ext:(j + 1) * ext, :] = xe * cw_ref[j:j + 1, :]
    xe_ref[0:halo, :] = xe_ref[L:ext, :]
    shift = shift_ref[...]
    bcm = _silu(dot(shift, taps_ref[:, inner:inner + 2 * gn]))

    dtr = dt_ref[...] + dtb_ref[...]
    dt = jnp.maximum(dtr, 0.0) + jnp.log1p(jnp.exp(-jnp.abs(dtr)))
    adt = dt * (-jnp.exp(alog_ref[...]))
    ri = lax.broadcasted_iota(jnp.int32, (L, L), 0)
    ci = lax.broadcasted_iota(jnp.int32, (L, L), 1)
    tri = ri >= ci
    cs = _dot_exact_lhs(jnp.where(tri, 1.0, 0.0).astype(bf16), adt)
    cst = cs.T
    csl = cs[L - 1:L, :]
    dt_hi = dt.astype(bf16)
    dt_lo = (dt - dt_hi.astype(f32)).astype(bf16)
    ecs_b = jnp.exp(cs).astype(bf16)
    dec_b = jnp.exp(csl - cs).astype(bf16)
    cdec = jnp.broadcast_to(jnp.exp(csl), (SUBLANES, LANES))
    lane = lax.broadcasted_iota(jnp.int32, (L, LANES), 1)

    ssq = jnp.zeros((L, 1), f32)
    for g in range(groups):
        sl = slice(g * gw, (g + 1) * gw)
        rex = rexp_ref[:, sl]
        xs = _silu(dot(shift, taps_ref[:, sl]))
        bg = bcm[:, g * N:(g + 1) * N]
        cg = bcm[:, gn + g * N:gn + (g + 1) * N].astype(bf16)
        xdt = xs * (dot(dt_hi, rex) + dot(dt_lo, rex))
        xdt_b = xdt.astype(bf16)
        xdec_b = (xdt * dot(dec_b, rex)).astype(bf16)
        gmat = lax.dot_general(cg, bg.astype(bf16), _NT, preferred_element_type=f32)
        bgt = bg.T.astype(bf16)
        st_prev = state_ref[g]
        y = dot(cg, st_prev.astype(bf16)) * dot(ecs_b, rex) + xs * dexp_ref[:, sl]
        state_ref[g] = st_prev * _dot_exact_rhs(cdec, rex)[0:1, :] + dot(bgt, xdec_b)
        diag = []
        for pr in range(gw // LANES):
            xp = xdt_b[:, pr * LANES:(pr + 1) * LANES]
            lhs, rhs = [], []
            for sub in range(LANES // P):
                hh = (g * gw + pr * LANES) // P + sub
                seg = cs[:, hh:hh + 1] - cst[hh:hh + 1, :]
                lm = jnp.exp(jnp.where(tri, seg, -jnp.inf))
                lhs.append((gmat * lm).astype(bf16))
                rhs.append(jnp.where((lane >= sub * P) & (lane < (sub + 1) * P), xp,
                                     jnp.zeros_like(xp)))
            diag.append(dot(jnp.concatenate(lhs, axis=1), jnp.concatenate(rhs, axis=0)))
        yg = (y + jnp.concatenate(diag, axis=1)) * _silu(z_ref[:, sl].astype(f32))
        ssq = ssq + jnp.sum(yg * yg, axis=-1, keepdims=True)
        y_ref[:, sl] = yg

    scale = lax.rsqrt(ssq * (1.0 / inner) + NORM_EPS)
    o_ref[...] = (y_ref[...] * scale * nw_ref[...]).astype(o_ref.dtype)


def _ssd(zxbc, dt, conv_w, conv_b, dt_bias, a_log, d_skip, norm_w, batch, seq, inner, groups):
    t = zxbc.shape[0]
    L = SSM_CHUNK
    heads = inner // SSM_HEAD_DIM
    gn = groups * SSM_STATE
    conv_ch = inner + 2 * gn
    assert heads <= LANES and LANES % SSM_HEAD_DIM == 0 and inner % (2 * gn) == 0
    nc = seq // L
    pad_h = LANES - heads
    dtb = jnp.pad(dt_bias, (0, pad_h)).reshape(1, LANES)
    alog = jnp.pad(a_log, (0, pad_h)).reshape(1, LANES)
    dexp = jnp.repeat(d_skip, SSM_HEAD_DIM).reshape(1, inner)
    rexp = (jnp.arange(LANES, dtype=jnp.int32)[:, None]
            == (jnp.arange(inner, dtype=jnp.int32) // SSM_HEAD_DIM)[None, :]).astype(jnp.bfloat16)
    bc_blk = inner * 2 // (2 * gn)
    halo = BF16_ROWS
    ext = halo + L
    rows = jnp.arange(L, dtype=jnp.int32)[:, None]
    cols = jnp.arange(SSM_CONV * ext + halo, dtype=jnp.int32)[None, :]
    blk, pos = cols // ext, cols % ext
    shift = jnp.where(blk < SSM_CONV, pos == halo + rows - (SSM_CONV - 1 - blk),
                      pos < 2).astype(jnp.bfloat16)
    cb_hi = conv_b.astype(jnp.bfloat16)
    cb_lo = (conv_b - cb_hi.astype(jnp.float32)).astype(jnp.bfloat16)
    cbb = jnp.zeros((halo, conv_ch), jnp.bfloat16).at[0].set(cb_hi).at[1].set(cb_lo)
    kern = functools.partial(_ssd_kernel, inner=inner, groups=groups)
    const = lambda b, c: (0, 0)
    return pl.pallas_call(
        kern,
        grid=(batch, nc),
        in_specs=[
            pl.BlockSpec((L, inner), lambda b, c: (b * nc + c, 0)),
            pl.BlockSpec((L, inner), lambda b, c: (b * nc + c, 1)),
            pl.BlockSpec((L, 2 * gn), lambda b, c: (b * nc + c, bc_blk)),
            pl.BlockSpec((L, LANES), lambda b, c: (b * nc + c, 0)),
            pl.BlockSpec((L, SSM_CONV * ext + halo), const),
            pl.BlockSpec((SSM_CONV, conv_ch), const),
            pl.BlockSpec((halo, conv_ch), const),
            pl.BlockSpec((1, LANES), const),
            pl.BlockSpec((1, LANES), const),
            pl.BlockSpec((1, inner), const),
            pl.BlockSpec((1, inner), const),
            pl.BlockSpec((LANES, inner), const),
        ],
        out_specs=pl.BlockSpec((L, inner), lambda b, c: (b * nc + c, 0)),
        out_shape=jax.ShapeDtypeStruct((t, inner), jnp.bfloat16),
        scratch_shapes=[
            pltpu.VMEM((ext, conv_ch), jnp.bfloat16),
            pltpu.VMEM((SSM_CONV * ext + halo, conv_ch), jnp.bfloat16),
            pltpu.VMEM((groups, SSM_STATE, inner // groups), jnp.float32),
            pltpu.VMEM((L, inner), jnp.float32),
        ],
        compiler_params=_cparams(("arbitrary", "arbitrary")),
    )(zxbc, zxbc, zxbc, dt, shift, conv_w.astype(jnp.bfloat16), cbb, dtb, alog, dexp,
      norm_w.reshape(1, inner), rexp)


def _proj_residual_final_kernel(a_ref, w_ref, x_ref, nw_ref, o_ref):
    x2 = x_ref[...] + jnp.dot(a_ref[...], w_ref[...], preferred_element_type=jnp.float32)
    ms = jnp.mean(x2 * x2, axis=-1, keepdims=True)
    o_ref[...] = (x2 * lax.rsqrt(ms + NORM_EPS) * nw_ref[...]).astype(o_ref.dtype)


def _proj_residual_final(a, w, x, nw, tm):
    t, k = a.shape
    d = w.shape[1]
    return pl.pallas_call(
        _proj_residual_final_kernel,
        grid=(t // tm,),
        in_specs=[pl.BlockSpec((tm, k), lambda i: (i, 0)),
                  pl.BlockSpec((k, d), lambda i: (0, 0), pipeline_mode=pl.Buffered(1)),
                  pl.BlockSpec((tm, d), lambda i: (i, 0)),
                  pl.BlockSpec((1, d), lambda i: (0, 0))],
        out_specs=pl.BlockSpec((tm, d), lambda i: (i, 0)),
        out_shape=jax.ShapeDtypeStruct((t, d), jnp.float32),
        compiler_params=_cparams(("parallel",)),
    )(a, w, x, nw.reshape(1, d))


def _row_tile(t, want):
    while t % want:
        want //= 2
    return want


def kernel(x, norm_w, a_w_in, a_w_out, rel_bias, b_w_in, b_conv_w, b_conv_b, b_dt_bias, b_a_log,
           b_d, b_norm_w, b_w_out, final_norm_w):
    batch, seq, d = x.shape
    t = batch * seq
    bf16 = jnp.bfloat16
    assert norm_w.shape[0] == 2 and a_w_in.shape[0] == 1 and b_w_in.shape[0] == 1
    xf = x.reshape(t, d)

    att_w = a_w_out.shape[1]
    heads = att_w // ATT_HEAD_DIM
    iq_w = IDX_HEADS * IDX_HEAD_DIM
    wa = jnp.swapaxes(a_w_in, 1, 2).reshape(a_w_in.shape[2], d)
    n_small = wa.shape[0] - 4 * att_w - iq_w
    assert n_small == IDX_HEAD_DIM + IDX_HEADS <= LANES
    w_idx = jnp.pad(wa[4 * att_w:, :], ((0, LANES - n_small), (0, 0)))

    tm = _row_tile(seq, 2048)
    tn = _row_tile(att_w, 1024)
    tq = _row_tile(seq, ATT_Q_BLOCK)
    h0 = _rmsnorm(xf, norm_w[0], bf16, _row_tile(t, 512))
    q = _proj(h0, wa, 0, att_w, bf16, tm, tn, scale=ATT_HEAD_DIM ** -0.5 * LOG2E)
    k_hm = _proj(h0, wa, att_w, att_w, bf16, tm, tn, layout="heads")
    v_t = _proj(h0, wa, 2 * att_w, att_w, bf16, _row_tile(seq, 1024), tn, layout="t",
                batch=batch, tq=tq)
    g = _proj(h0, wa, 3 * att_w, att_w, bf16, tm, tn)
    idx = _proj(h0, w_idx, 0, iq_w + LANES, jnp.float32, _row_tile(seq, 1024), iq_w + LANES)
    att = _dsa_attention(q, g, k_hm, v_t, idx, _bias_tiles(rel_bias), batch, seq, heads, tq)
    x1, h1 = _proj_residual_norm(att, a_w_out[0].astype(bf16), xf, norm_w[1], bf16,
                                 _row_tile(t, 512))

    inner = b_w_out.shape[1]
    ssm_heads = b_dt_bias.shape[1]
    conv_ch = b_conv_w.shape[2]
    groups = (conv_ch - inner) // (2 * SSM_STATE)
    wb = jnp.swapaxes(b_w_in, 1, 2).reshape(b_w_in.shape[2], d)
    w_dt = jnp.pad(wb[inner + conv_ch:, :], ((0, LANES - ssm_heads), (0, 0)))
    zxbc = _proj(h1, wb, 0, inner + conv_ch, bf16, tm, _row_tile(inner + conv_ch, 1024))
    dt = _proj(h1, w_dt, 0, LANES, jnp.float32, tm, LANES)
    y = _ssd(zxbc, dt, b_conv_w[0], b_conv_b[0], b_dt_bias[0], b_a_log[0], b_d[0], b_norm_w[0],
             batch, seq, inner, groups)
    out = _proj_residual_final(y, b_w_out[0].astype(bf16), x1, final_norm_w, _row_tile(t, 512))
    return out.reshape(batch, seq, d)
```

```python
import functools
import math

import jax
import jax.numpy as jnp
from jax import lax
from jax.experimental import pallas as pl
from jax.experimental.pallas import tpu as pltpu

NORM_EPS = 1e-6

ATT_HEAD_DIM = 128
IDX_HEADS = 16
IDX_HEAD_DIM = 64
TOPK_MAX = 256
REL_BUCKETS = 32
REL_MAX_DIST = 128

SSM_HEAD_DIM = 64
SSM_STATE = 128
SSM_CONV = 4
SSM_CHUNK = 128

LANES = 128
SUBLANES = 8
BF16_ROWS = 16
VMEM_LIMIT_BYTES = 56 * 1024 * 1024

INT_MIN = -2 ** 31
NEG_INF_BITS = 0xFF800000 - 2 ** 32
HI16_MASK = 0xFFFF0000 - 2 ** 32
NEG_BIG = -1e30
LOG2E = math.log2(math.e)
ATT_Q_BLOCK = 512
ATT_LOOP_CHUNKS = 4
ATT_HEADS_PER_STEP = 4

_NT = (((1,), (1,)), ((), ()))


def _cparams(sem):
    return pltpu.CompilerParams(dimension_semantics=sem, vmem_limit_bytes=VMEM_LIMIT_BYTES)


def _silu(x):
    h = 0.5 * x
    return h + h * jnp.tanh(h)


def _split3(x):
    hi = x.astype(jnp.bfloat16)
    r1 = x - hi.astype(jnp.float32)
    mid = r1.astype(jnp.bfloat16)
    lo = (r1 - mid.astype(jnp.float32)).astype(jnp.bfloat16)
    return hi, mid, lo


def _dot_exact_lhs(a01, x):
    hi, mid, lo = _split3(x)
    f = functools.partial(jnp.dot, preferred_element_type=jnp.float32)
    return f(a01, hi) + f(a01, mid) + f(a01, lo)


def _dot_exact_rhs(x, b01):
    hi, mid, lo = _split3(x)
    f = functools.partial(jnp.dot, preferred_element_type=jnp.float32)
    return f(hi, b01) + f(mid, b01) + f(lo, b01)


def _rmsnorm_kernel(x_ref, nw_ref, o_ref):
    x = x_ref[...]
    ms = jnp.mean(x * x, axis=-1, keepdims=True)
    o_ref[...] = (x * lax.rsqrt(ms + NORM_EPS) * nw_ref[...]).astype(o_ref.dtype)


def _rmsnorm(x, nw, out_dtype, tm):
    t, d = x.shape
    return pl.pallas_call(
        _rmsnorm_kernel,
        grid=(t // tm,),
        in_specs=[pl.BlockSpec((tm, d), lambda i: (i, 0)),
                  pl.BlockSpec((1, d), lambda i: (0, 0))],
        out_specs=pl.BlockSpec((tm, d), lambda i: (i, 0)),
        out_shape=jax.ShapeDtypeStruct((t, d), out_dtype),
        compiler_params=_cparams(("parallel",)),
    )(x, nw.reshape(1, d))


def _proj_kernel(a_ref, w_ref, o_ref, wb_ref, *, layout, scale, tq):
    @pl.when(pl.program_id(1) == 0)
    def _():
        w = w_ref[...]
        if scale != 1.0:
            w = w * scale
        wb_ref[...] = (w if layout == "t" else w.T).astype(wb_ref.dtype)

    f32 = jnp.float32
    if layout == "t":
        acc = lax.dot_general(wb_ref[...], a_ref[...], _NT, preferred_element_type=f32)
        hd = ATT_HEAD_DIM
        for hh in range(o_ref.shape[1]):
            for cc in range(o_ref.shape[2]):
                o_ref[0, hh, cc, 0:hd, :] = acc[hh * hd:(hh + 1) * hd,
                                                cc * tq:(cc + 1) * tq].astype(o_ref.dtype)
                o_ref[0, hh, cc, hd:hd + BF16_ROWS, :] = jnp.ones((BF16_ROWS, tq), o_ref.dtype)
    else:
        acc = jnp.dot(a_ref[...], wb_ref[...], preferred_element_type=f32)
        if layout == "heads":
            for j in range(o_ref.shape[0]):
                o_ref[j] = acc[:, j * LANES:(j + 1) * LANES].astype(o_ref.dtype)
        else:
            o_ref[...] = acc.astype(o_ref.dtype)


def _proj(a, wt, col_off, n, out_dtype, tm, tn, layout="rows", scale=1.0, batch=1, tq=LANES):
    t, d = a.shape
    assert col_off % tn == 0 and n % tn == 0 and t % tm == 0
    off = col_off // tn
    seq = t // batch
    mb = seq // tm
    if layout == "rows":
        out_spec = pl.BlockSpec((tm, tn), lambda j, i: (i, j))
        out_shape = (t, n)
    elif layout == "heads":
        out_spec = pl.BlockSpec((tn // LANES, tm, LANES), lambda j, i: (j, i, 0))
        out_shape = (n // LANES, t, LANES)
    else:
        rows = ATT_HEAD_DIM + BF16_ROWS
        out_spec = pl.BlockSpec((1, tn // ATT_HEAD_DIM, tm // tq, rows, tq),
                                lambda j, i: (i // mb, j, i % mb, 0, 0))
        out_shape = (batch, n // ATT_HEAD_DIM, seq // tq, rows, tq)
    wb_shape = (tn, d) if layout == "t" else (d, tn)
    return pl.pallas_call(
        functools.partial(_proj_kernel, layout=layout, scale=scale, tq=tq),
        grid=(n // tn, t // tm),
        in_specs=[pl.BlockSpec((tm, d), lambda j, i: (i, 0)),
                  pl.BlockSpec((tn, d), lambda j, i: (j + off, 0))],
        out_specs=out_spec,
        out_shape=jax.ShapeDtypeStruct(out_shape, out_dtype),
        scratch_shapes=[pltpu.VMEM(wb_shape, jnp.bfloat16)],
        compiler_params=_cparams(("parallel", "arbitrary")),
    )(a, wt)


def _proj_residual_norm_kernel(a_ref, w_ref, x_ref, nw_ref, xo_ref, ho_ref):
    x1 = x_ref[...] + jnp.dot(a_ref[...], w_ref[...], preferred_element_type=jnp.float32)
    xo_ref[...] = x1
    ms = jnp.mean(x1 * x1, axis=-1, keepdims=True)
    ho_ref[...] = (x1 * lax.rsqrt(ms + NORM_EPS) * nw_ref[...]).astype(ho_ref.dtype)


def _proj_residual_norm(a, w, x, nw, h_dtype, tm):
    t, k = a.shape
    d = w.shape[1]
    return pl.pallas_call(
        _proj_residual_norm_kernel,
        grid=(t // tm,),
        in_specs=[pl.BlockSpec((tm, k), lambda i: (i, 0)),
                  pl.BlockSpec((k, d), lambda i: (0, 0), pipeline_mode=pl.Buffered(1)),
                  pl.BlockSpec((tm, d), lambda i: (i, 0)),
                  pl.BlockSpec((1, d), lambda i: (0, 0))],
        out_specs=[pl.BlockSpec((tm, d), lambda i: (i, 0)),
                   pl.BlockSpec((tm, d), lambda i: (i, 0))],
        out_shape=[jax.ShapeDtypeStruct((t, d), jnp.float32),
                   jax.ShapeDtypeStruct((t, d), h_dtype)],
        compiler_params=_cparams(("parallel",)),
    )(a, w, x, nw.reshape(1, d))


def _dsa_kernel(q_ref, g_ref, k_ref, vt_ref, iq_ref, ikq_ref, ika_ref, bias_ref, o_ref,
                ikbd_ref, iqb_ref, madd_ref, hi_ref, s0_ref, s1_ref, m_ref, tied_ref, acc_ref,
                *, tq, topk, nq):
    qb = pl.program_id(1)
    h = pl.program_id(2)
    nck = qb + 1
    nb = tq // LANES
    base_d = nq * nb
    base_p = base_d + nb
    pairs = IDX_HEADS // 2
    f32 = jnp.float32
    bf16 = jnp.bfloat16
    i32 = jnp.int32
    hd = ATT_HEAD_DIM

    @pl.when((h == 0) & (qb == 0))
    def _():
        blk = ika_ref[...]
        lane = lax.broadcasted_iota(jnp.int32, blk.shape, 1)
        a = jnp.where(lane < IDX_HEAD_DIM, blk, 0.0)
        ikbd_ref[0] = a.astype(bf16)
        ikbd_ref[1] = pltpu.roll(a, IDX_HEAD_DIM, 1).astype(bf16)

    @pl.when(h == 0)
    def _():
        iqb_ref[...] = iq_ref[...].astype(bf16)
        iwt = ikq_ref[...].T[IDX_HEAD_DIM:IDX_HEAD_DIM + IDX_HEADS, :] * (
            IDX_HEADS ** -0.5 * IDX_HEAD_DIM ** -0.5)
        krow = lax.broadcasted_iota(jnp.int32, (tq, tq), 0)
        qcol = qb * tq + lax.broadcasted_iota(jnp.int32, (tq, tq), 1)

        def score_chunk(c, carry):
            start = pl.multiple_of(c * tq, tq)
            ka = ikbd_ref[0, pl.ds(start, tq), :]
            kb = ikbd_ref[1, pl.ds(start, tq), :]
            sc = jnp.zeros((tq, tq), f32)
            for j in range(pairs):
                rhs = iqb_ref[:, j * LANES:(j + 1) * LANES]
                d0 = lax.dot_general(ka, rhs, _NT, preferred_element_type=f32)
                d1 = lax.dot_general(kb, rhs, _NT, preferred_element_type=f32)
                sc = sc + jnp.maximum(d0, 0.0) * iwt[2 * j:2 * j + 1, :]
                sc = sc + jnp.maximum(d1, 0.0) * iwt[2 * j + 1:2 * j + 2, :]
            causal = c * tq + krow <= qcol
            bits = lax.bitcast_convert_type(sc, i32)
            key = bits ^ ((bits >> 31) & i32(0x7FFFFFFF))
            key = jnp.where(causal, key, i32(INT_MIN))
            hi = lax.bitcast_convert_type(
                jnp.where(causal, bits, i32(NEG_INF_BITS)) & i32(HI16_MASK), f32).astype(bf16)
            for a in range(nb):
                madd_ref[c * nb + a] = lax.bitcast_convert_type(
                    key[a * LANES:(a + 1) * LANES, :], f32)
                hi_ref[c * nb + a] = hi[a * LANES:(a + 1) * LANES, :]
            return carry

        lax.fori_loop(0, nck, score_chunk, 0)

        def keys_of(c, a):
            return lax.bitcast_convert_type(madd_ref[c * nb + a], i32)

        srow = lax.broadcasted_iota(i32, (LANES, tq), 0)

        def count(pred):
            def count_chunk(c, part):
                for a in range(nb):
                    pos = c * tq + a * LANES + srow
                    w = jnp.where(pred(keys_of(c, a), pos), 1.0, 0.0)
                    part = part + jnp.sum(w.reshape(LANES // SUBLANES, SUBLANES, tq), axis=0)
                return part

            part = lax.fori_loop(0, nck, count_chunk, jnp.zeros((SUBLANES, tq), f32))
            return jnp.sum(part, axis=0, keepdims=True)

        def count_hi(cand_b):
            one = jnp.ones((), bf16)
            zero = jnp.zeros((), bf16)

            def count_chunk(c, part):
                for a in range(nb):
                    w = jnp.where(hi_ref[c * nb + a] >= cand_b, one, zero)
                    s = w[0:BF16_ROWS, :]
                    for r in range(1, LANES // BF16_ROWS):
                        s = s + w[r * BF16_ROWS:(r + 1) * BF16_ROWS, :]
                    part = part + s.astype(f32)
                return part

            part = lax.fori_loop(0, nck, count_chunk, jnp.zeros((BF16_ROWS, tq), f32))
            return jnp.sum(part, axis=0, keepdims=True)

        def bit_step(i, carry, hi_half):
            tau_u, cnt_tau = carry
            cand_u = tau_u | lax.shift_left(i32(1), 31 - i)
            cand_s = cand_u ^ i32(INT_MIN)
            if hi_half:
                fbits = cand_s ^ ((cand_s >> 31) & i32(0x7FFFFFFF))
                cnt = count_hi(lax.bitcast_convert_type(fbits & i32(HI16_MASK), f32).astype(bf16))
            else:
                cnt = count(lambda key, pos: key >= cand_s)
            ok = cnt >= topk
            return jnp.where(ok, cand_u, tau_u), jnp.where(ok, cnt, cnt_tau)

        carry = (jnp.zeros((1, tq), i32), jnp.zeros((1, tq), f32))
        carry = lax.fori_loop(0, 16, functools.partial(bit_step, hi_half=True), carry)
        tau_u, cnt_tau = lax.fori_loop(16, 32, functools.partial(bit_step, hi_half=False), carry)
        tau = tau_u ^ i32(INT_MIN)

        def write_mask(sel_fn):
            def mask_chunk(c, carry):
                for a in range(nb):
                    key = keys_of(c, a)
                    pos = c * tq + a * LANES + srow
                    sel = sel_fn(key, pos) & (key != i32(INT_MIN))
                    madd_ref[c * nb + a] = jnp.where(sel, 0.0, NEG_BIG).astype(f32)
                return carry

            lax.fori_loop(0, nck, mask_chunk, 0)

        tied_ref[...] = jnp.broadcast_to(jnp.where(cnt_tau > topk, 1.0, 0.0), tied_ref.shape)
        n_tied_rows = jnp.sum(tied_ref[...])

        @pl.when(n_tied_rows == 0.0)
        def _():
            write_mask(lambda key, pos: key >= tau)

        @pl.when(n_tied_rows > 0.0)
        def _():
            need = topk - count(lambda key, pos: key > tau)
            nbits = max(1, (nq * tq - 1).bit_length())

            def pos_step(i, v):
                cand = v | lax.shift_left(i32(1), nbits - 1 - i)
                below = count(lambda key, pos: (key == tau) & (pos < cand))
                return jnp.where(below < need, cand, v)

            v = lax.fori_loop(0, nbits, pos_step, jnp.zeros((1, tq), i32))
            write_mask(lambda key, pos: (key > tau) | ((key == tau) & (pos <= v)))

    def attend(hh):
        dtile = bias_ref[hh, 0]
        ptile = bias_ref[hh, 1]
        def biased(slab, tiles):
            cols = [slab[:, b * LANES:(b + 1) * LANES] for b in range(nb)]
            for b, tile in tiles:
                cols[b] = cols[b] + tile
            return jnp.concatenate(cols, axis=1)

        for a in range(nb):
            tiles = [(a, dtile)] + ([(a + 1, ptile)] if a + 1 < nb else [])
            madd_ref[base_d + a] = biased(madd_ref[qb * nb + a], tiles)
        madd_ref[base_p] = biased(madd_ref[jnp.maximum(qb - 1, 0) * nb + nb - 1], [(0, ptile)])

        m_ref[...] = jnp.full(m_ref.shape, NEG_BIG, f32)
        acc_ref[...] = jnp.zeros(acc_ref.shape, f32)
        q = q_ref[:, hh * hd:(hh + 1) * hd]

        s_bufs = (s0_ref, s1_ref)

        def qk(c, par):
            kc = k_ref[hh, pl.ds(pl.multiple_of(c * tq, tq), tq), :]
            s = lax.dot_general(kc, q, _NT, preferred_element_type=f32)
            is_diag = c == qb
            is_prev = c == qb - 1
            cmax = None
            for a in range(nb):
                sl = jnp.where(is_diag, base_d + a, c * nb + a)
                if a == nb - 1:
                    sl = jnp.where(is_prev, base_p, sl)
                sa = s[a * LANES:(a + 1) * LANES, :] + madd_ref[sl]
                s_bufs[par][a * LANES:(a + 1) * LANES, :] = sa
                sm = jnp.max(sa.reshape(LANES // SUBLANES, SUBLANES, tq), axis=0)
                cmax = sm if cmax is None else jnp.maximum(cmax, sm)
            return jnp.max(cmax, axis=0, keepdims=True)

        def softmax_pv(c, par, cmax):
            m_prev = m_ref[...]
            m_new = jnp.maximum(m_prev, cmax)
            m_ref[...] = m_new
            p = jnp.exp2(s_bufs[par][...] - m_new).astype(bf16)
            acc_ref[...] = jnp.exp2(m_prev - m_new) * acc_ref[...] + jnp.dot(
                vt_ref[0, hh, c], p, preferred_element_type=f32)

        def run(c, n, cm, feed_next):
            for j in range(n):
                cm_next = qk(c + j + 1, (j + 1) & 1) if (j + 1 < n or feed_next) else None
                softmax_pv(c + j, j & 1, cm)
                cm = cm_next
            return cm

        unroll = ATT_LOOP_CHUNKS
        nloops = (nck - 1) // unroll
        cm = lax.fori_loop(0, nloops, lambda i, cm: run(unroll * i, unroll, cm, True), qk(0, 0))
        c0 = unroll * nloops
        for rem in range(1, unroll + 1):
            pl.when(nck - c0 == rem)(functools.partial(run, c0, rem, cm, False))

        out_t = acc_ref[0:hd, :] * (1.0 / acc_ref[hd:hd + 1, :])
        o_ref[:, hh * hd:(hh + 1) * hd] = (
            out_t.T * _silu(g_ref[:, hh * hd:(hh + 1) * hd].astype(f32))).astype(o_ref.dtype)

    for hh in range(k_ref.shape[0]):
        attend(hh)


def _dsa_attention(q, g, k_hm, v_t, idx, bias_tiles, batch, seq, heads, tq):
    t = q.shape[0]
    nq = seq // tq
    nb = tq // LANES
    topk = min(TOPK_MAX, seq // 4)
    iq_w = IDX_HEADS * IDX_HEAD_DIM
    small_blk = iq_w // LANES
    hd = ATT_HEAD_DIM
    vrows = hd + BF16_ROWS
    hps = ATT_HEADS_PER_STEP if heads % ATT_HEADS_PER_STEP == 0 else 1
    kern = functools.partial(_dsa_kernel, tq=tq, topk=topk, nq=nq)
    return pl.pallas_call(
        kern,
        grid=(batch, nq, heads // hps),
        in_specs=[
            pl.BlockSpec((tq, hps * hd), lambda b, i, h: (b * nq + i, h)),
            pl.BlockSpec((tq, hps * hd), lambda b, i, h: (b * nq + i, h)),
            pl.BlockSpec((hps, seq, hd), lambda b, i, h: (h, b, 0)),
            pl.BlockSpec((1, hps, nq, vrows, tq), lambda b, i, h: (b, h, 0, 0, 0)),
            pl.BlockSpec((tq, iq_w), lambda b, i, h: (b * nq + i, 0)),
            pl.BlockSpec((tq, LANES), lambda b, i, h: (b * nq + i, small_blk)),
            pl.BlockSpec((seq, LANES), lambda b, i, h: (b, small_blk)),
            pl.BlockSpec((hps, 2, LANES, LANES), lambda b, i, h: (h, 0, 0, 0)),
        ],
        out_specs=pl.BlockSpec((tq, hps * hd), lambda b, i, h: (b * nq + i, h)),
        out_shape=jax.ShapeDtypeStruct((t, heads * hd), jnp.bfloat16),
        scratch_shapes=[
            pltpu.VMEM((2, seq, LANES), jnp.bfloat16),
            pltpu.VMEM((tq, iq_w), jnp.bfloat16),
            pltpu.VMEM((nq * nb + nb + 1, LANES, tq), jnp.float32),
            pltpu.VMEM((nq * nb, LANES, tq), jnp.bfloat16),
            pltpu.VMEM((tq, tq), jnp.float32),
            pltpu.VMEM((tq, tq), jnp.float32),
            pltpu.VMEM((1, tq), jnp.float32),
            pltpu.VMEM((SUBLANES, tq), jnp.float32),
            pltpu.VMEM((vrows, tq), jnp.float32),
        ],
        compiler_params=_cparams(("arbitrary", "arbitrary", "arbitrary")),
    )(q, g, k_hm, v_t, idx, idx, idx, bias_tiles)


def _rel_bucket_of(n):
    max_exact = REL_BUCKETS // 2
    nf = jnp.maximum(n, 1).astype(jnp.float32)
    large = max_exact + (jnp.log(nf / max_exact) / math.log(REL_MAX_DIST / max_exact)
                         * (REL_BUCKETS - max_exact)).astype(jnp.int32)
    large = jnp.minimum(large, REL_BUCKETS - 1)
    return jnp.where(n < max_exact, n, large)


def _bias_tiles(rel_bias):
    assert REL_MAX_DIST <= LANES
    n = LANES
    heads = rel_bias.shape[1]
    dist = jnp.arange(2 * n, dtype=jnp.int32)
    bucket = jnp.where(dist >= REL_MAX_DIST, REL_BUCKETS - 1, _rel_bucket_of(dist))
    bv = (rel_bias[bucket] - rel_bias[REL_BUCKETS - 1][None, :]) * LOG2E
    rows = jnp.concatenate([bv.T, jnp.zeros((heads, n), bv.dtype)], axis=1)
    flat = jnp.broadcast_to(rows[:, None, :], (heads, n, 3 * n)).reshape(heads, 3 * n * n)
    toep = flat[:, :n * (3 * n - 1)].reshape(heads, n, 3 * n - 1)[:, :, :2 * n]
    return jnp.stack([toep[:, :, :n], toep[:, :, n:]], axis=1).astype(jnp.float32)


def _ssd_kernel(z_ref, x_ref, bc_ref, dt_ref, shift_ref, cw_ref, cbb_ref, dtb_ref, alog_ref,
                dexp_ref, nw_ref, rexp_ref, o_ref, xe_ref, taps_ref, state_ref, y_ref,
                *, inner, groups):
    f32 = jnp.float32
    bf16 = jnp.bfloat16
    L = SSM_CHUNK
    P = SSM_HEAD_DIM
    N = SSM_STATE
    heads = inner // P
    hpg = heads // groups
    gw = hpg * P
    gn = groups * N
    halo = BF16_ROWS
    ext = halo + L
    dot = functools.partial(jnp.dot, preferred_element_type=f32)

    @pl.when(pl.program_id(1) == 0)
    def _():
        xe_ref[0:halo, :] = jnp.zeros((halo, xe_ref.shape[1]), bf16)
        taps_ref[SSM_CONV * ext:SSM_CONV * ext + halo, :] = cbb_ref[...]
        state_ref[...] = jnp.zeros(state_ref.shape, f32)

    xe_ref[halo:ext, 0:inner] = x_ref[...]
    xe_ref[halo:ext, inner:inner + 2 * gn] = bc_ref[...]
    xe = xe_ref[...]
    for j in range(SSM_CONV):
        taps_ref[j * ext:(j + 1) * ext, :] = xe * cw_ref[j:j + 1, :]
    xe_ref[0:halo, :] = xe_ref[L:ext, :]
    shift = shift_ref[...]
    bcm = _silu(dot(shift, taps_ref[:, inner:inner + 2 * gn]))

    dtr = dt_ref[...] + dtb_ref[...]
    dt = jnp.maximum(dtr, 0.0) + jnp.log1p(jnp.exp(-jnp.abs(dtr)))
    adt = dt * (-jnp.exp(alog_ref[...]))
    ri = lax.broadcasted_iota(jnp.int32, (L, L), 0)
    ci = lax.broadcasted_iota(jnp.int32, (L, L), 1)
    tri = ri >= ci
    cs = _dot_exact_lhs(jnp.where(tri, 1.0, 0.0).astype(bf16), adt)
    cst = cs.T
    csl = cs[L - 1:L, :]
    dt_hi = dt.astype(bf16)
    dt_lo = (dt - dt_hi.astype(f32)).astype(bf16)
    ecs_b = jnp.exp(cs).astype(bf16)
    dec_b = jnp.exp(csl - cs).astype(bf16)
    cdec = jnp.broadcast_to(jnp.exp(csl), (SUBLANES, LANES))
    lane = lax.broadcasted_iota(jnp.int32, (L, LANES), 1)

    ssq = jnp.zeros((L, 1), f32)
    for g in range(groups):
        sl = slice(g * gw, (g + 1) * gw)
        rex = rexp_ref[:, sl]
        xs = _silu(dot(shift, taps_ref[:, sl]))
        bg = bcm[:, g * N:(g + 1) * N]
        cg = bcm[:, gn + g * N:gn + (g + 1) * N].astype(bf16)
        xdt = xs * (dot(dt_hi, rex) + dot(dt_lo, rex))
        xdt_b = xdt.astype(bf16)
        xdec_b = (xdt * dot(dec_b, rex)).astype(bf16)
        gmat = lax.dot_general(cg, bg.astype(bf16), _NT, preferred_element_type=f32)
        bgt = bg.T.astype(bf16)
        st_prev = state_ref[g]
        y = dot(cg, st_prev.astype(bf16)) * dot(ecs_b, rex) + xs * dexp_ref[:, sl]
        state_ref[g] = st_prev * _dot_exact_rhs(cdec, rex)[0:1, :] + dot(bgt, xdec_b)
        diag = []
        for pr in range(gw // LANES):
            xp = xdt_b[:, pr * LANES:(pr + 1) * LANES]
            lhs, rhs = [], []
            for sub in range(LANES // P):
                hh = (g * gw + pr * LANES) // P + sub
                seg = cs[:, hh:hh + 1] - cst[hh:hh + 1, :]
                lm = jnp.exp(jnp.where(tri, seg, -jnp.inf))
                lhs.append((gmat * lm).astype(bf16))
                rhs.append(jnp.where((lane >= sub * P) & (lane < (sub + 1) * P), xp,
                                     jnp.zeros_like(xp)))
            diag.append(dot(jnp.concatenate(lhs, axis=1), jnp.concatenate(rhs, axis=0)))
        yg = (y + jnp.concatenate(diag, axis=1)) * _silu(z_ref[:, sl].astype(f32))
        ssq = ssq + jnp.sum(yg * yg, axis=-1, keepdims=True)
        y_ref[:, sl] = yg

    scale = lax.rsqrt(ssq * (1.0 / inner) + NORM_EPS)
    o_ref[...] = (y_ref[...] * scale * nw_ref[...]).astype(o_ref.dtype)


def _ssd(zxbc, dt, conv_w, conv_b, dt_bias, a_log, d_skip, norm_w, batch, seq, inner, groups):
    t = zxbc.shape[0]
    L = SSM_CHUNK
    heads = inner // SSM_HEAD_DIM
    gn = groups * SSM_STATE
    conv_ch = inner + 2 * gn
    assert heads <= LANES and LANES % SSM_HEAD_DIM == 0 and inner % (2 * gn) == 0
    nc = seq // L
    pad_h = LANES - heads
    dtb = jnp.pad(dt_bias, (0, pad_h)).reshape(1, LANES)
    alog = jnp.pad(a_log, (0, pad_h)).reshape(1, LANES)
    dexp = jnp.repeat(d_skip, SSM_HEAD_DIM).reshape(1, inner)
    rexp = (jnp.arange(LANES, dtype=jnp.int32)[:, None]
            == (jnp.arange(inner, dtype=jnp.int32) // SSM_HEAD_DIM)[None, :]).astype(jnp.bfloat16)
    bc_blk = inner * 2 // (2 * gn)
    halo = BF16_ROWS
    ext = halo + L
    rows = jnp.arange(L, dtype=jnp.int32)[:, None]
    cols = jnp.arange(SSM_CONV * ext + halo, dtype=jnp.int32)[None, :]
    blk, pos = cols // ext, cols % ext
    shift = jnp.where(blk < SSM_CONV, pos == halo + rows - (SSM_CONV - 1 - blk),
                      pos < 2).astype(jnp.bfloat16)
    cb_hi = conv_b.astype(jnp.bfloat16)
    cb_lo = (conv_b - cb_hi.astype(jnp.float32)).astype(jnp.bfloat16)
    cbb = jnp.zeros((halo, conv_ch), jnp.bfloat16).at[0].set(cb_hi).at[1].set(cb_lo)
    kern = functools.partial(_ssd_kernel, inner=inner, groups=groups)
    const = lambda b, c: (0, 0)
    return pl.pallas_call(
        kern,
        grid=(batch, nc),
        in_specs=[
            pl.BlockSpec((L, inner), lambda b, c: (b * nc + c, 0)),
            pl.BlockSpec((L, inner), lambda b, c: (b * nc + c, 1)),
            pl.BlockSpec((L, 2 * gn), lambda b, c: (b * nc + c, bc_blk)),
            pl.BlockSpec((L, LANES), lambda b, c: (b * nc + c, 0)),
            pl.BlockSpec((L, SSM_CONV * ext + halo), const),
            pl.BlockSpec((SSM_CONV, conv_ch), const),
            pl.BlockSpec((halo, conv_ch), const),
            pl.BlockSpec((1, LANES), const),
            pl.BlockSpec((1, LANES), const),
            pl.BlockSpec((1, inner), const),
            pl.BlockSpec((1, inner), const),
            pl.BlockSpec((LANES, inner), const),
        ],
        out_specs=pl.BlockSpec((L, inner), lambda b, c: (b * nc + c, 0)),
        out_shape=jax.ShapeDtypeStruct((t, inner), jnp.bfloat16),
        scratch_shapes=[
            pltpu.VMEM((ext, conv_ch), jnp.bfloat16),
            pltpu.VMEM((SSM_CONV * ext + halo, conv_ch), jnp.bfloat16),
            pltpu.VMEM((groups, SSM_STATE, inner // groups), jnp.float32),
            pltpu.VMEM((L, inner), jnp.float32),
        ],
        compiler_params=_cparams(("arbitrary", "arbitrary")),
    )(zxbc, zxbc, zxbc, dt, shift, conv_w.astype(jnp.bfloat16), cbb, dtb, alog, dexp,
      norm_w.reshape(1, inner), rexp)


def _proj_residual_final_kernel(a_ref, w_ref, x_ref, nw_ref, o_ref):
    x2 = x_ref[...] + jnp.dot(a_ref[...], w_ref[...], preferred_element_type=jnp.float32)
    ms = jnp.mean(x2 * x2, axis=-1, keepdims=True)
    o_ref[...] = (x2 * lax.rsqrt(ms + NORM_EPS) * nw_ref[...]).astype(o_ref.dtype)


def _proj_residual_final(a, w, x, nw, tm):
    t, k = a.shape
    d = w.shape[1]
    return pl.pallas_call(
        _proj_residual_final_kernel,
        grid=(t // tm,),
        in_specs=[pl.BlockSpec((tm, k), lambda i: (i, 0)),
                  pl.BlockSpec((k, d), lambda i: (0, 0), pipeline_mode=pl.Buffered(1)),
                  pl.BlockSpec((tm, d), lambda i: (i, 0)),
                  pl.BlockSpec((1, d), lambda i: (0, 0))],
        out_specs=pl.BlockSpec((tm, d), lambda i: (i, 0)),
        out_shape=jax.ShapeDtypeStruct((t, d), jnp.float32),
        compiler_params=_cparams(("parallel",)),
    )(a, w, x, nw.reshape(1, d))


def _row_tile(t, want):
    while t % want:
        want //= 2
    return want


def kernel(x, norm_w, a_w_in, a_w_out, rel_bias, b_w_in, b_conv_w, b_conv_b, b_dt_bias, b_a_log,
           b_d, b_norm_w, b_w_out, final_norm_w):
    batch, seq, d = x.shape
    t = batch * seq
    bf16 = jnp.bfloat16
    assert norm_w.shape[0] == 2 and a_w_in.shape[0] == 1 and b_w_in.shape[0] == 1
    xf = x.reshape(t, d)

    att_w = a_w_out.shape[1]
    heads = att_w // ATT_HEAD_DIM
    iq_w = IDX_HEADS * IDX_HEAD_DIM
    wa = jnp.swapaxes(a_w_in, 1, 2).reshape(a_w_in.shape[2], d)
    n_small = wa.shape[0] - 4 * att_w - iq_w
    assert n_small == IDX_HEAD_DIM + IDX_HEADS <= LANES
    w_idx = jnp.pad(wa[4 * att_w:, :], ((0, LANES - n_small), (0, 0)))

    tm = _row_tile(seq, 2048)
    tn = _row_tile(att_w, 1024)
    tq = _row_tile(seq, ATT_Q_BLOCK)
    h0 = _rmsnorm(xf, norm_w[0], bf16, _row_tile(t, 512))
    q = _proj(h0, wa, 0, att_w, bf16, tm, tn, scale=ATT_HEAD_DIM ** -0.5 * LOG2E)
    k_hm = _proj(h0, wa, att_w, att_w, bf16, tm, tn, layout="heads")
    v_t = _proj(h0, wa, 2 * att_w, att_w, bf16, _row_tile(seq, 1024), tn, layout="t",
                batch=batch, tq=tq)
    g = _proj(h0, wa, 3 * att_w, att_w, bf16, tm, tn)
    idx = _proj(h0, w_idx, 0, iq_w + LANES, jnp.float32, _row_tile(seq, 1024), iq_w + LANES)
    att = _dsa_attention(q, g, k_hm, v_t, idx, _bias_tiles(rel_bias), batch, seq, heads, tq)
    x1, h1 = _proj_residual_norm(att, a_w_out[0].astype(bf16), xf, norm_w[1], bf16,
                                 _row_tile(t, 512))

    inner = b_w_out.shape[1]
    ssm_heads = b_dt_bias.shape[1]
    conv_ch = b_conv_w.shape[2]
    groups = (conv_ch - inner) // (2 * SSM_STATE)
    wb = jnp.swapaxes(b_w_in, 1, 2).reshape(b_w_in.shape[2], d)
    w_dt = jnp.pad(wb[inner + conv_ch:, :], ((0, LANES - ssm_heads), (0, 0)))
    zxbc = _proj(h1, wb, 0, inner + conv_ch, bf16, tm, _row_tile(inner + conv_ch, 1024))
    dt = _proj(h1, w_dt, 0, LANES, jnp.float32, tm, LANES)
    y = _ssd(zxbc, dt, b_conv_w[0], b_conv_b[0], b_dt_bias[0], b_a_log[0], b_d[0], b_norm_w[0],
             batch, seq, inner, groups)
    out = _proj_residual_final(y, b_w_out[0].astype(bf16), x1, final_norm_w, _row_tile(t, 512))
    return out.reshape(batch, seq, d)
```

```python
import functools
import math

import jax
import jax.numpy as jnp
from jax import lax
from jax.experimental import pallas as pl
from jax.experimental.pallas import tpu as pltpu

NORM_EPS = 1e-6

ATT_HEAD_DIM = 128
IDX_HEADS = 16
IDX_HEAD_DIM = 64
TOPK_MAX = 256
REL_BUCKETS = 32
REL_MAX_DIST = 128

SSM_HEAD_DIM = 64
SSM_STATE = 128
SSM_CONV = 4
SSM_CHUNK = 128

LANES = 128
SUBLANES = 8
BF16_ROWS = 16
VMEM_LIMIT_BYTES = 56 * 1024 * 1024

INT_MIN = -2 ** 31
NEG_INF_BITS = 0xFF800000 - 2 ** 32
HI16_MASK = 0xFFFF0000 - 2 ** 32
LO_BIAS = 2 ** 15
NEG_BIG = -1e30
LOG2E = math.log2(math.e)
ATT_Q_BLOCK = 512
ATT_LOOP_CHUNKS = 4
ATT_HEADS_PER_STEP = 4

_NT = (((1,), (1,)), ((), ()))


def _cparams(sem):
    return pltpu.CompilerParams(dimension_semantics=sem, vmem_limit_bytes=VMEM_LIMIT_BYTES)


def _silu(x):
    h = 0.5 * x
    return h + h * jnp.tanh(h)


def _split3(x):
    hi = x.astype(jnp.bfloat16)
    r1 = x - hi.astype(jnp.float32)
    mid = r1.astype(jnp.bfloat16)
    lo = (r1 - mid.astype(jnp.float32)).astype(jnp.bfloat16)
    return hi, mid, lo


def _dot_exact_lhs(a01, x):
    hi, mid, lo = _split3(x)
    f = functools.partial(jnp.dot, preferred_element_type=jnp.float32)
    return f(a01, hi) + f(a01, mid) + f(a01, lo)


def _dot_exact_rhs(x, b01):
    hi, mid, lo = _split3(x)
    f = functools.partial(jnp.dot, preferred_element_type=jnp.float32)
    return f(hi, b01) + f(mid, b01) + f(lo, b01)


def _rmsnorm_kernel(x_ref, nw_ref, o_ref):
    x = x_ref[...]
    ms = jnp.mean(x * x, axis=-1, keepdims=True)
    o_ref[...] = (x * lax.rsqrt(ms + NORM_EPS) * nw_ref[...]).astype(o_ref.dtype)


def _rmsnorm(x, nw, out_dtype, tm):
    t, d = x.shape
    return pl.pallas_call(
        _rmsnorm_kernel,
        grid=(t // tm,),
        in_specs=[pl.BlockSpec((tm, d), lambda i: (i, 0)),
                  pl.BlockSpec((1, d), lambda i: (0, 0))],
        out_specs=pl.BlockSpec((tm, d), lambda i: (i, 0)),
        out_shape=jax.ShapeDtypeStruct((t, d), out_dtype),
        compiler_params=_cparams(("parallel",)),
    )(x, nw.reshape(1, d))


def _proj_kernel(a_ref, w_ref, o_ref, wb_ref, *, layout, scale, tq):
    @pl.when(pl.program_id(1) == 0)
    def _():
        w = w_ref[...]
        if scale != 1.0:
            w = w * scale
        wb_ref[...] = (w if layout == "t" else w.T).astype(wb_ref.dtype)

    f32 = jnp.float32
    if layout == "t":
        acc = lax.dot_general(wb_ref[...], a_ref[...], _NT, preferred_element_type=f32)
        hd = ATT_HEAD_DIM
        for hh in range(o_ref.shape[1]):
            for cc in range(o_ref.shape[2]):
                o_ref[0, hh, cc, 0:hd, :] = acc[hh * hd:(hh + 1) * hd,
                                                cc * tq:(cc + 1) * tq].astype(o_ref.dtype)
                o_ref[0, hh, cc, hd:hd + BF16_ROWS, :] = jnp.ones((BF16_ROWS, tq), o_ref.dtype)
    else:
        acc = jnp.dot(a_ref[...], wb_ref[...], preferred_element_type=f32)
        if layout == "heads":
            for j in range(o_ref.shape[0]):
                o_ref[j] = acc[:, j * LANES:(j + 1) * LANES].astype(o_ref.dtype)
        else:
            o_ref[...] = acc.astype(o_ref.dtype)


def _proj(a, wt, col_off, n, out_dtype, tm, tn, layout="rows", scale=1.0, batch=1, tq=LANES):
    t, d = a.shape
    assert col_off % tn == 0 and n % tn == 0 and t % tm == 0
    off = col_off // tn
    seq = t // batch
    mb = seq // tm
    if layout == "rows":
        out_spec = pl.BlockSpec((tm, tn), lambda j, i: (i, j))
        out_shape = (t, n)
    elif layout == "heads":
        out_spec = pl.BlockSpec((tn // LANES, tm, LANES), lambda j, i: (j, i, 0))
        out_shape = (n // LANES, t, LANES)
    else:
        rows = ATT_HEAD_DIM + BF16_ROWS
        out_spec = pl.BlockSpec((1, tn // ATT_HEAD_DIM, tm // tq, rows, tq),
                                lambda j, i: (i // mb, j, i % mb, 0, 0))
        out_shape = (batch, n // ATT_HEAD_DIM, seq // tq, rows, tq)
    wb_shape = (tn, d) if layout == "t" else (d, tn)
    return pl.pallas_call(
        functools.partial(_proj_kernel, layout=layout, scale=scale, tq=tq),
        grid=(n // tn, t // tm),
        in_specs=[pl.BlockSpec((tm, d), lambda j, i: (i, 0)),
                  pl.BlockSpec((tn, d), lambda j, i: (j + off, 0))],
        out_specs=out_spec,
        out_shape=jax.ShapeDtypeStruct(out_shape, out_dtype),
        scratch_shapes=[pltpu.VMEM(wb_shape, jnp.bfloat16)],
        compiler_params=_cparams(("parallel", "arbitrary")),
    )(a, wt)


def _proj_residual_norm_kernel(a_ref, w_ref, x_ref, nw_ref, xo_ref, ho_ref):
    x1 = x_ref[...] + jnp.dot(a_ref[...], w_ref[...], preferred_element_type=jnp.float32)
    xo_ref[...] = x1
    ms = jnp.mean(x1 * x1, axis=-1, keepdims=True)
    ho_ref[...] = (x1 * lax.rsqrt(ms + NORM_EPS) * nw_ref[...]).astype(ho_ref.dtype)


def _proj_residual_norm(a, w, x, nw, h_dtype, tm):
    t, k = a.shape
    d = w.shape[1]
    return pl.pallas_call(
        _proj_residual_norm_kernel,
        grid=(t // tm,),
        in_specs=[pl.BlockSpec((tm, k), lambda i: (i, 0)),
                  pl.BlockSpec((k, d), lambda i: (0, 0), pipeline_mode=pl.Buffered(1)),
                  pl.BlockSpec((tm, d), lambda i: (i, 0)),
                  pl.BlockSpec((1, d), lambda i: (0, 0))],
        out_specs=[pl.BlockSpec((tm, d), lambda i: (i, 0)),
                   pl.BlockSpec((tm, d), lambda i: (i, 0))],
        out_shape=[jax.ShapeDtypeStruct((t, d), jnp.float32),
                   jax.ShapeDtypeStruct((t, d), h_dtype)],
        compiler_params=_cparams(("parallel",)),
    )(a, w, x, nw.reshape(1, d))


def _dsa_kernel(q_ref, g_ref, k_ref, vt_ref, iq_ref, ikq_ref, ika_ref, bias_ref, o_ref,
                ikbd_ref, iqb_ref, madd_ref, hi_ref, lo_ref, s0_ref, s1_ref, m_ref, tied_ref, acc_ref,
                *, tq, topk, nq):
    qb = pl.program_id(1)
    h = pl.program_id(2)
    nck = qb + 1
    nb = tq // LANES
    base_d = nq * nb
    base_p = base_d + nb
    pairs = IDX_HEADS // 2
    f32 = jnp.float32
    bf16 = jnp.bfloat16
    i32 = jnp.int32
    hd = ATT_HEAD_DIM

    @pl.when((h == 0) & (qb == 0))
    def _():
        blk = ika_ref[...]
        lane = lax.broadcasted_iota(jnp.int32, blk.shape, 1)
        a = jnp.where(lane < IDX_HEAD_DIM, blk, 0.0)
        ikbd_ref[0] = a.astype(bf16)
        ikbd_ref[1] = pltpu.roll(a, IDX_HEAD_DIM, 1).astype(bf16)

    @pl.when(h == 0)
    def _():
        iqb_ref[...] = iq_ref[...].astype(bf16)
        iwt = ikq_ref[...].T[IDX_HEAD_DIM:IDX_HEAD_DIM + IDX_HEADS, :] * (
            IDX_HEADS ** -0.5 * IDX_HEAD_DIM ** -0.5)
        krow = lax.broadcasted_iota(jnp.int32, (tq, tq), 0)
        qcol = qb * tq + lax.broadcasted_iota(jnp.int32, (tq, tq), 1)

        def score_chunk(c, carry):
            start = pl.multiple_of(c * tq, tq)
            ka = ikbd_ref[0, pl.ds(start, tq), :]
            kb = ikbd_ref[1, pl.ds(start, tq), :]
            sc = jnp.zeros((tq, tq), f32)
            for j in range(pairs):
                rhs = iqb_ref[:, j * LANES:(j + 1) * LANES]
                d0 = lax.dot_general(ka, rhs, _NT, preferred_element_type=f32)
                d1 = lax.dot_general(kb, rhs, _NT, preferred_element_type=f32)
                sc = sc + jnp.maximum(d0, 0.0) * iwt[2 * j:2 * j + 1, :]
                sc = sc + jnp.maximum(d1, 0.0) * iwt[2 * j + 1:2 * j + 2, :]
            causal = c * tq + krow <= qcol
            bits = lax.bitcast_convert_type(sc, i32)
            key = bits ^ ((bits >> 31) & i32(0x7FFFFFFF))
            key = jnp.where(causal, key, i32(INT_MIN))
            hi = lax.bitcast_convert_type(
                jnp.where(causal, bits, i32(NEG_INF_BITS)) & i32(HI16_MASK), f32).astype(bf16)
            for a in range(nb):
                madd_ref[c * nb + a] = lax.bitcast_convert_type(
                    key[a * LANES:(a + 1) * LANES, :], f32)
                hi_ref[c * nb + a] = hi[a * LANES:(a + 1) * LANES, :]
            return carry

        lax.fori_loop(0, nck, score_chunk, 0)

        def keys_of(c, a):
            return lax.bitcast_convert_type(madd_ref[c * nb + a], i32)

        srow = lax.broadcasted_iota(i32, (LANES, tq), 0)

        def count(pred):
            def count_chunk(c, part):
                for a in range(nb):
                    pos = c * tq + a * LANES + srow
                    w = jnp.where(pred(keys_of(c, a), pos), 1.0, 0.0)
                    part = part + jnp.sum(w.reshape(LANES // SUBLANES, SUBLANES, tq), axis=0)
                return part

            part = lax.fori_loop(0, nck, count_chunk, jnp.zeros((SUBLANES, tq), f32))
            return jnp.sum(part, axis=0, keepdims=True)

        def count_packed(ref, cand):
            one = jnp.ones((), ref.dtype)
            zero = jnp.zeros((), ref.dtype)

            def count_chunk(c, part):
                for a in range(nb):
                    w = jnp.where(ref[c * nb + a] >= cand, one, zero)
                    s = w[0:BF16_ROWS, :]
                    for r in range(1, LANES // BF16_ROWS):
                        s = s + w[r * BF16_ROWS:(r + 1) * BF16_ROWS, :]
                    part = part + s.astype(f32)
                return part

            part = lax.fori_loop(0, nck, count_chunk, jnp.zeros((BF16_ROWS, tq), f32))
            return jnp.sum(part, axis=0, keepdims=True)

        def bit_step(i, carry, hi_half):
            tau_u, cnt_tau = carry
            cand_u = tau_u | lax.shift_left(i32(1), 31 - i)
            if hi_half:
                cand_s = cand_u ^ i32(INT_MIN)
                fbits = cand_s ^ ((cand_s >> 31) & i32(0x7FFFFFFF))
                cnt = count_packed(
                    hi_ref, lax.bitcast_convert_type(fbits & i32(HI16_MASK), f32).astype(bf16))
            else:
                cnt = count_packed(lo_ref, ((cand_u & i32(0xFFFF)) - LO_BIAS).astype(jnp.int16))
            ok = cnt >= topk
            return jnp.where(ok, cand_u, tau_u), jnp.where(ok, cnt, cnt_tau)

        carry = (jnp.zeros((1, tq), i32), jnp.zeros((1, tq), f32))
        carry = lax.fori_loop(0, 16, functools.partial(bit_step, hi_half=True), carry)

        tau_hi = (carry[0] ^ i32(INT_MIN)) >> 16

        def lo_chunk(c, carry_):
            for a in range(nb):
                key = keys_of(c, a)
                khi = key >> 16
                lo = jnp.where(khi > tau_hi, i32(LO_BIAS - 1),
                               jnp.where(khi == tau_hi, (key & i32(0xFFFF)) - LO_BIAS,
                                         i32(-LO_BIAS)))
                lo_ref[c * nb + a] = lo.astype(jnp.int16)
            return carry_

        lax.fori_loop(0, nck, lo_chunk, 0)
        tau_u, cnt_tau = lax.fori_loop(16, 32, functools.partial(bit_step, hi_half=False), carry)
        tau = tau_u ^ i32(INT_MIN)

        def write_mask(sel_fn):
            def mask_chunk(c, carry):
                for a in range(nb):
                    key = keys_of(c, a)
                    pos = c * tq + a * LANES + srow
                    sel = sel_fn(key, pos) & (key != i32(INT_MIN))
                    madd_ref[c * nb + a] = jnp.where(sel, 0.0, NEG_BIG).astype(f32)
                return carry

            lax.fori_loop(0, nck, mask_chunk, 0)

        tied_ref[...] = jnp.broadcast_to(jnp.where(cnt_tau > topk, 1.0, 0.0), tied_ref.shape)
        n_tied_rows = jnp.sum(tied_ref[...])

        @pl.when(n_tied_rows == 0.0)
        def _():
            write_mask(lambda key, pos: key >= tau)

        @pl.when(n_tied_rows > 0.0)
        def _():
            need = topk - count(lambda key, pos: key > tau)
            nbits = max(1, (nq * tq - 1).bit_length())

            def pos_step(i, v):
                cand = v | lax.shift_left(i32(1), nbits - 1 - i)
                below = count(lambda key, pos: (key == tau) & (pos < cand))
                return jnp.where(below < need, cand, v)

            v = lax.fori_loop(0, nbits, pos_step, jnp.zeros((1, tq), i32))
            write_mask(lambda key, pos: (key > tau) | ((key == tau) & (pos <= v)))

    def attend(hh):
        dtile = bias_ref[hh, 0]
        ptile = bias_ref[hh, 1]
        def biased(slab, tiles):
            cols = [slab[:, b * LANES:(b + 1) * LANES] for b in range(nb)]
            for b, tile in tiles:
                cols[b] = cols[b] + tile
            return jnp.concatenate(cols, axis=1)

        for a in range(nb):
            tiles = [(a, dtile)] + ([(a + 1, ptile)] if a + 1 < nb else [])
            madd_ref[base_d + a] = biased(madd_ref[qb * nb + a], tiles)
        madd_ref[base_p] = biased(madd_ref[jnp.maximum(qb - 1, 0) * nb + nb - 1], [(0, ptile)])

        m_ref[...] = jnp.full(m_ref.shape, NEG_BIG, f32)
        acc_ref[...] = jnp.zeros(acc_ref.shape, f32)
        q = q_ref[:, hh * hd:(hh + 1) * hd]

        s_bufs = (s0_ref, s1_ref)

        def qk(c, par):
            kc = k_ref[hh, pl.ds(pl.multiple_of(c * tq, tq), tq), :]
            s = lax.dot_general(kc, q, _NT, preferred_element_type=f32)
            is_diag = c == qb
            is_prev = c == qb - 1
            cmax = None
            for a in range(nb):
                sl = jnp.where(is_diag, base_d + a, c * nb + a)
                if a == nb - 1:
                    sl = jnp.where(is_prev, base_p, sl)
                sa = s[a * LANES:(a + 1) * LANES, :] + madd_ref[sl]
                s_bufs[par][a * LANES:(a + 1) * LANES, :] = sa
                sm = jnp.max(sa.reshape(LANES // SUBLANES, SUBLANES, tq), axis=0)
                cmax = sm if cmax is None else jnp.maximum(cmax, sm)
            return jnp.max(cmax, axis=0, keepdims=True)

        def softmax_pv(c, par, cmax):
            m_prev = m_ref[...]
            m_new = jnp.maximum(m_prev, cmax)
            m_ref[...] = m_new
            p = jnp.exp2(s_bufs[par][...] - m_new).astype(bf16)
            acc_ref[...] = jnp.exp2(m_prev - m_new) * acc_ref[...] + jnp.dot(
                vt_ref[0, hh, c], p, preferred_element_type=f32)

        def run(c, n, cm, feed_next):
            for j in range(n):
                cm_next = qk(c + j + 1, (j + 1) & 1) if (j + 1 < n or feed_next) else None
                softmax_pv(c + j, j & 1, cm)
                cm = cm_next
            return cm

        unroll = ATT_LOOP_CHUNKS
        nloops = (nck - 1) // unroll
        cm = lax.fori_loop(0, nloops, lambda i, cm: run(unroll * i, unroll, cm, True), qk(0, 0))
        c0 = unroll * nloops
        for rem in range(1, unroll + 1):
            pl.when(nck - c0 == rem)(functools.partial(run, c0, rem, cm, False))

        out_t = acc_ref[0:hd, :] * (1.0 / acc_ref[hd:hd + 1, :])
        o_ref[:, hh * hd:(hh + 1) * hd] = (
            out_t.T * _silu(g_ref[:, hh * hd:(hh + 1) * hd].astype(f32))).astype(o_ref.dtype)

    for hh in range(k_ref.shape[0]):
        attend(hh)


def _dsa_attention(q, g, k_hm, v_t, idx, bias_tiles, batch, seq, heads, tq):
    t = q.shape[0]
    nq = seq // tq
    nb = tq // LANES
    topk = min(TOPK_MAX, seq // 4)
    iq_w = IDX_HEADS * IDX_HEAD_DIM
    small_blk = iq_w // LANES
    hd = ATT_HEAD_DIM
    vrows = hd + BF16_ROWS
    hps = ATT_HEADS_PER_STEP if heads % ATT_HEADS_PER_STEP == 0 else 1
    kern = functools.partial(_dsa_kernel, tq=tq, topk=topk, nq=nq)
    return pl.pallas_call(
        kern,
        grid=(batch, nq, heads // hps),
        in_specs=[
            pl.BlockSpec((tq, hps * hd), lambda b, i, h: (b * nq + i, h)),
            pl.BlockSpec((tq, hps * hd), lambda b, i, h: (b * nq + i, h)),
            pl.BlockSpec((hps, seq, hd), lambda b, i, h: (h, b, 0)),
            pl.BlockSpec((1, hps, nq, vrows, tq), lambda b, i, h: (b, h, 0, 0, 0)),
            pl.BlockSpec((tq, iq_w), lambda b, i, h: (b * nq + i, 0)),
            pl.BlockSpec((tq, LANES), lambda b, i, h: (b * nq + i, small_blk)),
            pl.BlockSpec((seq, LANES), lambda b, i, h: (b, small_blk)),
            pl.BlockSpec((hps, 2, LANES, LANES), lambda b, i, h: (h, 0, 0, 0)),
        ],
        out_specs=pl.BlockSpec((tq, hps * hd), lambda b, i, h: (b * nq + i, h)),
        out_shape=jax.ShapeDtypeStruct((t, heads * hd), jnp.bfloat16),
        scratch_shapes=[
            pltpu.VMEM((2, seq, LANES), jnp.bfloat16),
            pltpu.VMEM((tq, iq_w), jnp.bfloat16),
            pltpu.VMEM((nq * nb + nb + 1, LANES, tq), jnp.float32),
            pltpu.VMEM((nq * nb, LANES, tq), jnp.bfloat16),
            pltpu.VMEM((nq * nb, LANES, tq), jnp.int16),
            pltpu.VMEM((tq, tq), jnp.float32),
            pltpu.VMEM((tq, tq), jnp.float32),
            pltpu.VMEM((1, tq), jnp.float32),
            pltpu.VMEM((SUBLANES, tq), jnp.float32),
            pltpu.VMEM((vrows, tq), jnp.float32),
        ],
        compiler_params=_cparams(("arbitrary", "arbitrary", "arbitrary")),
    )(q, g, k_hm, v_t, idx, idx, idx, bias_tiles)


def _rel_bucket_of(n):
    max_exact = REL_BUCKETS // 2
    nf = jnp.maximum(n, 1).astype(jnp.float32)
    large = max_exact + (jnp.log(nf / max_exact) / math.log(REL_MAX_DIST / max_exact)
                         * (REL_BUCKETS - max_exact)).astype(jnp.int32)
    large = jnp.minimum(large, REL_BUCKETS - 1)
    return jnp.where(n < max_exact, n, large)


def _bias_tiles(rel_bias):
    assert REL_MAX_DIST <= LANES
    n = LANES
    heads = rel_bias.shape[1]
    dist = jnp.arange(2 * n, dtype=jnp.int32)
    bucket = jnp.where(dist >= REL_MAX_DIST, REL_BUCKETS - 1, _rel_bucket_of(dist))
    bv = (rel_bias[bucket] - rel_bias[REL_BUCKETS - 1][None, :]) * LOG2E
    rows = jnp.concatenate([bv.T, jnp.zeros((heads, n), bv.dtype)], axis=1)
    flat = jnp.broadcast_to(rows[:, None, :], (heads, n, 3 * n)).reshape(heads, 3 * n * n)
    toep = flat[:, :n * (3 * n - 1)].reshape(heads, n, 3 * n - 1)[:, :, :2 * n]
    return jnp.stack([toep[:, :, :n], toep[:, :, n:]], axis=1).astype(jnp.float32)


def _ssd_kernel(z_ref, x_ref, bc_ref, dt_ref, shift_ref, cw_ref, cbb_ref, dtb_ref, alog_ref,
                dexp_ref, nw_ref, rexp_ref, o_ref, xe_ref, taps_ref, state_ref, y_ref,
                *, inner, groups):
    f32 = jnp.float32
    bf16 = jnp.bfloat16
    L = SSM_CHUNK
    P = SSM_HEAD_DIM
    N = SSM_STATE
    heads = inner // P
    hpg = heads // groups
    gw = hpg * P
    gn = groups * N
    halo = BF16_ROWS
    ext = halo + L
    dot = functools.partial(jnp.dot, preferred_element_type=f32)

    @pl.when(pl.program_id(1) == 0)
    def _():
        xe_ref[0:halo, :] = jnp.zeros((halo, xe_ref.shape[1]), bf16)
        taps_ref[SSM_CONV * ext:SSM_CONV * ext + halo, :] = cbb_ref[...]
        state_ref[...] = jnp.zeros(state_ref.shape, f32)

    xe_ref[halo:ext, 0:inner] = x_ref[...]
    xe_ref[halo:ext, inner:inner + 2 * gn] = bc_ref[...]
    xe = xe_ref[...]
    for j in range(SSM_CONV):
        taps_ref[j * ext:(j + 1) * ext, :] = xe * cw_ref[j:j + 1, :]
    xe_ref[0:halo, :] = xe_ref[L:ext, :]
    shift = shift_ref[...]
    bcm = _silu(dot(shift, taps_ref[:, inner:inner + 2 * gn]))

    dtr = dt_ref[...] + dtb_ref[...]
    dt = jnp.maximum(dtr, 0.0) + jnp.log1p(jnp.exp(-jnp.abs(dtr)))
    adt = dt * (-jnp.exp(alog_ref[...]))
    ri = lax.broadcasted_iota(jnp.int32, (L, L), 0)
    ci = lax.broadcasted_iota(jnp.int32, (L, L), 1)
    tri = ri >= ci
    cs = _dot_exact_lhs(jnp.where(tri, 1.0, 0.0).astype(bf16), adt)
    cst = cs.T
    csl = cs[L - 1:L, :]
    dt_hi = dt.astype(bf16)
    dt_lo = (dt - dt_hi.astype(f32)).astype(bf16)
    ecs_b = jnp.exp(cs).astype(bf16)
    dec_b = jnp.exp(csl - cs).astype(bf16)
    cdec = jnp.broadcast_to(jnp.exp(csl), (SUBLANES, LANES))
    lane = lax.broadcasted_iota(jnp.int32, (L, LANES), 1)

    ssq = jnp.zeros((L, 1), f32)
    for g in range(groups):
        sl = slice(g * gw, (g + 1) * gw)
        rex = rexp_ref[:, sl]
        xs = _silu(dot(shift, taps_ref[:, sl]))
        bg = bcm[:, g * N:(g + 1) * N]
        cg = bcm[:, gn + g * N:gn + (g + 1) * N].astype(bf16)
        xdt = xs * (dot(dt_hi, rex) + dot(dt_lo, rex))
        xdt_b = xdt.astype(bf16)
        xdec_b = (xdt * dot(dec_b, rex)).astype(bf16)
        gmat = lax.dot_general(cg, bg.astype(bf16), _NT, preferred_element_type=f32)
        bgt = bg.T.astype(bf16)
        st_prev = state_ref[g]
        y = dot(cg, st_prev.astype(bf16)) * dot(ecs_b, rex) + xs * dexp_ref[:, sl]
        state_ref[g] = st_prev * _dot_exact_rhs(cdec, rex)[0:1, :] + dot(bgt, xdec_b)
        diag = []
        for pr in range(gw // LANES):
            xp = xdt_b[:, pr * LANES:(pr + 1) * LANES]
            lhs, rhs = [], []
            for sub in range(LANES // P):
                hh = (g * gw + pr * LANES) // P + sub
                seg = cs[:, hh:hh + 1] - cst[hh:hh + 1, :]
                lm = jnp.exp(jnp.where(tri, seg, -jnp.inf))
                lhs.append((gmat * lm).astype(bf16))
                rhs.append(jnp.where((lane >= sub * P) & (lane < (sub + 1) * P), xp,
                                     jnp.zeros_like(xp)))
            diag.append(dot(jnp.concatenate(lhs, axis=1), jnp.concatenate(rhs, axis=0)))
        yg = (y + jnp.concatenate(diag, axis=1)) * _silu(z_ref[:, sl].astype(f32))
        ssq = ssq + jnp.sum(yg * yg, axis=-1, keepdims=True)
        y_ref[:, sl] = yg

    scale = lax.rsqrt(ssq * (1.0 / inner) + NORM_EPS)
    o_ref[...] = (y_ref[...] * scale * nw_ref[...]).astype(o_ref.dtype)


def _ssd(zxbc, dt, conv_w, conv_b, dt_bias, a_log, d_skip, norm_w, batch, seq, inner, groups):
    t = zxbc.shape[0]
    L = SSM_CHUNK
    heads = inner // SSM_HEAD_DIM
    gn = groups * SSM_STATE
    conv_ch = inner + 2 * gn
    assert heads <= LANES and LANES % SSM_HEAD_DIM == 0 and inner % (2 * gn) == 0
    nc = seq // L
    pad_h = LANES - heads
    dtb = jnp.pad(dt_bias, (0, pad_h)).reshape(1, LANES)
    alog = jnp.pad(a_log, (0, pad_h)).reshape(1, LANES)
    dexp = jnp.repeat(d_skip, SSM_HEAD_DIM).reshape(1, inner)
    rexp = (jnp.arange(LANES, dtype=jnp.int32)[:, None]
            == (jnp.arange(inner, dtype=jnp.int32) // SSM_HEAD_DIM)[None, :]).astype(jnp.bfloat16)
    bc_blk = inner * 2 // (2 * gn)
    halo = BF16_ROWS
    ext = halo + L
    rows = jnp.arange(L, dtype=jnp.int32)[:, None]
    cols = jnp.arange(SSM_CONV * ext + halo, dtype=jnp.int32)[None, :]
    blk, pos = cols // ext, cols % ext
    shift = jnp.where(blk < SSM_CONV, pos == halo + rows - (SSM_CONV - 1 - blk),
                      pos < 2).astype(jnp.bfloat16)
    cb_hi = conv_b.astype(jnp.bfloat16)
    cb_lo = (conv_b - cb_hi.astype(jnp.float32)).astype(jnp.bfloat16)
    cbb = jnp.zeros((halo, conv_ch), jnp.bfloat16).at[0].set(cb_hi).at[1].set(cb_lo)
    kern = functools.partial(_ssd_kernel, inner=inner, groups=groups)
    const = lambda b, c: (0, 0)
    return pl.pallas_call(
        kern,
        grid=(batch, nc),
        in_specs=[
            pl.BlockSpec((L, inner), lambda b, c: (b * nc + c, 0)),
            pl.BlockSpec((L, inner), lambda b, c: (b * nc + c, 1)),
            pl.BlockSpec((L, 2 * gn), lambda b, c: (b * nc + c, bc_blk)),
            pl.BlockSpec((L, LANES), lambda b, c: (b * nc + c, 0)),
            pl.BlockSpec((L, SSM_CONV * ext + halo), const),
            pl.BlockSpec((SSM_CONV, conv_ch), const),
            pl.BlockSpec((halo, conv_ch), const),
            pl.BlockSpec((1, LANES), const),
            pl.BlockSpec((1, LANES), const),
            pl.BlockSpec((1, inner), const),
            pl.BlockSpec((1, inner), const),
            pl.BlockSpec((LANES, inner), const),
        ],
        out_specs=pl.BlockSpec((L, inner), lambda b, c: (b * nc + c, 0)),
        out_shape=jax.ShapeDtypeStruct((t, inner), jnp.bfloat16),
        scratch_shapes=[
            pltpu.VMEM((ext, conv_ch), jnp.bfloat16),
            pltpu.VMEM((SSM_CONV * ext + halo, conv_ch), jnp.bfloat16),
            pltpu.VMEM((groups, SSM_STATE, inner // groups), jnp.float32),
            pltpu.VMEM((L, inner), jnp.float32),
        ],
        compiler_params=_cparams(("arbitrary", "arbitrary")),
    )(zxbc, zxbc, zxbc, dt, shift, conv_w.astype(jnp.bfloat16), cbb, dtb, alog, dexp,
      norm_w.reshape(1, inner), rexp)


def _proj_residual_final_kernel(a_ref, w_ref, x_ref, nw_ref, o_ref):
    x2 = x_ref[...] + jnp.dot(a_ref[...], w_ref[...], preferred_element_type=jnp.float32)
    ms = jnp.mean(x2 * x2, axis=-1, keepdims=True)
    o_ref[...] = (x2 * lax.rsqrt(ms + NORM_EPS) * nw_ref[...]).astype(o_ref.dtype)


def _proj_residual_final(a, w, x, nw, tm):
    t, k = a.shape
    d = w.shape[1]
    return pl.pallas_call(
        _proj_residual_final_kernel,
        grid=(t // tm,),
        in_specs=[pl.BlockSpec((tm, k), lambda i: (i, 0)),
                  pl.BlockSpec((k, d), lambda i: (0, 0), pipeline_mode=pl.Buffered(1)),
                  pl.BlockSpec((tm, d), lambda i: (i, 0)),
                  pl.BlockSpec((1, d), lambda i: (0, 0))],
        out_specs=pl.BlockSpec((tm, d), lambda i: (i, 0)),
        out_shape=jax.ShapeDtypeStruct((t, d), jnp.float32),
        compiler_params=_cparams(("parallel",)),
    )(a, w, x, nw.reshape(1, d))


def _row_tile(t, want):
    while t % want:
        want //= 2
    return want


def kernel(x, norm_w, a_w_in, a_w_out, rel_bias, b_w_in, b_conv_w, b_conv_b, b_dt_bias, b_a_log,
           b_d, b_norm_w, b_w_out, final_norm_w):
    batch, seq, d = x.shape
    t = batch * seq
    bf16 = jnp.bfloat16
    assert norm_w.shape[0] == 2 and a_w_in.shape[0] == 1 and b_w_in.shape[0] == 1
    xf = x.reshape(t, d)

    att_w = a_w_out.shape[1]
    heads = att_w // ATT_HEAD_DIM
    iq_w = IDX_HEADS * IDX_HEAD_DIM
    wa = jnp.swapaxes(a_w_in, 1, 2).reshape(a_w_in.shape[2], d)
    n_small = wa.shape[0] - 4 * att_w - iq_w
    assert n_small == IDX_HEAD_DIM + IDX_HEADS <= LANES
    w_idx = jnp.pad(wa[4 * att_w:, :], ((0, LANES - n_small), (0, 0)))

    tm = _row_tile(seq, 2048)
    tn = _row_tile(att_w, 1024)
    tq = _row_tile(seq, ATT_Q_BLOCK)
    h0 = _rmsnorm(xf, norm_w[0], bf16, _row_tile(t, 512))
    q = _proj(h0, wa, 0, att_w, bf16, tm, tn, scale=ATT_HEAD_DIM ** -0.5 * LOG2E)
    k_hm = _proj(h0, wa, att_w, att_w, bf16, tm, tn, layout="heads")
    v_t = _proj(h0, wa, 2 * att_w, att_w, bf16, _row_tile(seq, 1024), tn, layout="t",
                batch=batch, tq=tq)
    g = _proj(h0, wa, 3 * att_w, att_w, bf16, tm, tn)
    idx = _proj(h0, w_idx, 0, iq_w + LANES, jnp.float32, _row_tile(seq, 1024), iq_w + LANES)
    att = _dsa_attention(q, g, k_hm, v_t, idx, _bias_tiles(rel_bias), batch, seq, heads, tq)
    x1, h1 = _proj_residual_norm(att, a_w_out[0].astype(bf16), xf, norm_w[1], bf16,
                                 _row_tile(t, 512))

    inner = b_w_out.shape[1]
    ssm_heads = b_dt_bias.shape[1]
    conv_ch = b_conv_w.shape[2]
    groups = (conv_ch - inner) // (2 * SSM_STATE)
    wb = jnp.swapaxes(b_w_in, 1, 2).reshape(b_w_in.shape[2], d)
    w_dt = jnp.pad(wb[inner + conv_ch:, :], ((0, LANES - ssm_heads), (0, 0)))
    zxbc = _proj(h1, wb, 0, inner + conv_ch, bf16, tm, _row_tile(inner + conv_ch, 1024))
    dt = _proj(h1, w_dt, 0, LANES, jnp.float32, tm, LANES)
    y = _ssd(zxbc, dt, b_conv_w[0], b_conv_b[0], b_dt_bias[0], b_a_log[0], b_d[0], b_norm_w[0],
             batch, seq, inner, groups)
    out = _proj_residual_final(y, b_w_out[0].astype(bf16), x1, final_norm_w, _row_tile(t, 512))
    return out.reshape(batch, seq, d)
```

```python
import functools
import math

import jax
import jax.numpy as jnp
from jax import lax
from jax.experimental import pallas as pl
from jax.experimental.pallas import tpu as pltpu

NORM_EPS = 1e-6

ATT_HEAD_DIM = 128
IDX_HEADS = 16
IDX_HEAD_DIM = 64
TOPK_MAX = 256
REL_BUCKETS = 32
REL_MAX_DIST = 128

SSM_HEAD_DIM = 64
SSM_STATE = 128
SSM_CONV = 4
SSM_CHUNK = 128

LANES = 128
SUBLANES = 8
BF16_ROWS = 16
VMEM_LIMIT_BYTES = 56 * 1024 * 1024

INT_MIN = -2 ** 31
LO_BIAS = 2 ** 15
NEG_BIG = -1e30
LOG2E = math.log2(math.e)
ATT_Q_BLOCK = 512
ATT_LOOP_CHUNKS = 4
ATT_HEADS_PER_STEP = 4

_NT = (((1,), (1,)), ((), ()))


def _cparams(sem):
    return pltpu.CompilerParams(dimension_semantics=sem, vmem_limit_bytes=VMEM_LIMIT_BYTES)


def _silu(x):
    h = 0.5 * x
    return h + h * jnp.tanh(h)


def _split3(x):
    hi = x.astype(jnp.bfloat16)
    r1 = x - hi.astype(jnp.float32)
    mid = r1.astype(jnp.bfloat16)
    lo = (r1 - mid.astype(jnp.float32)).astype(jnp.bfloat16)
    return hi, mid, lo


def _dot_exact_lhs(a01, x):
    hi, mid, lo = _split3(x)
    f = functools.partial(jnp.dot, preferred_element_type=jnp.float32)
    return f(a01, hi) + f(a01, mid) + f(a01, lo)


def _dot_exact_rhs(x, b01):
    hi, mid, lo = _split3(x)
    f = functools.partial(jnp.dot, preferred_element_type=jnp.float32)
    return f(hi, b01) + f(mid, b01) + f(lo, b01)


def _rmsnorm_kernel(x_ref, nw_ref, o_ref):
    x = x_ref[...]
    ms = jnp.mean(x * x, axis=-1, keepdims=True)
    o_ref[...] = (x * lax.rsqrt(ms + NORM_EPS) * nw_ref[...]).astype(o_ref.dtype)


def _rmsnorm(x, nw, out_dtype, tm):
    t, d = x.shape
    return pl.pallas_call(
        _rmsnorm_kernel,
        grid=(t // tm,),
        in_specs=[pl.BlockSpec((tm, d), lambda i: (i, 0)),
                  pl.BlockSpec((1, d), lambda i: (0, 0))],
        out_specs=pl.BlockSpec((tm, d), lambda i: (i, 0)),
        out_shape=jax.ShapeDtypeStruct((t, d), out_dtype),
        compiler_params=_cparams(("parallel",)),
    )(x, nw.reshape(1, d))


def _proj_kernel(a_ref, w_ref, o_ref, wb_ref, *, layout, scale, tq):
    @pl.when(pl.program_id(1) == 0)
    def _():
        w = w_ref[...]
        if scale != 1.0:
            w = w * scale
        wb_ref[...] = (w if layout == "t" else w.T).astype(wb_ref.dtype)

    f32 = jnp.float32
    if layout == "t":
        acc = lax.dot_general(wb_ref[...], a_ref[...], _NT, preferred_element_type=f32)
        hd = ATT_HEAD_DIM
        for hh in range(o_ref.shape[1]):
            for cc in range(o_ref.shape[2]):
                o_ref[0, hh, cc, 0:hd, :] = acc[hh * hd:(hh + 1) * hd,
                                                cc * tq:(cc + 1) * tq].astype(o_ref.dtype)
                o_ref[0, hh, cc, hd:hd + BF16_ROWS, :] = jnp.ones((BF16_ROWS, tq), o_ref.dtype)
    else:
        acc = jnp.dot(a_ref[...], wb_ref[...], preferred_element_type=f32)
        if layout == "heads":
            for j in range(o_ref.shape[0]):
                o_ref[j] = acc[:, j * LANES:(j + 1) * LANES].astype(o_ref.dtype)
        else:
            o_ref[...] = acc.astype(o_ref.dtype)


def _proj(a, wt, col_off, n, out_dtype, tm, tn, layout="rows", scale=1.0, batch=1, tq=LANES):
    t, d = a.shape
    assert col_off % tn == 0 and n % tn == 0 and t % tm == 0
    off = col_off // tn
    seq = t // batch
    mb = seq // tm
    if layout == "rows":
        out_spec = pl.BlockSpec((tm, tn), lambda j, i: (i, j))
        out_shape = (t, n)
    elif layout == "heads":
        out_spec = pl.BlockSpec((tn // LANES, tm, LANES), lambda j, i: (j, i, 0))
        out_shape = (n // LANES, t, LANES)
    else:
        rows = ATT_HEAD_DIM + BF16_ROWS
        out_spec = pl.BlockSpec((1, tn // ATT_HEAD_DIM, tm // tq, rows, tq),
                                lambda j, i: (i // mb, j, i % mb, 0, 0))
        out_shape = (batch, n // ATT_HEAD_DIM, seq // tq, rows, tq)
    wb_shape = (tn, d) if layout == "t" else (d, tn)
    return pl.pallas_call(
        functools.partial(_proj_kernel, layout=layout, scale=scale, tq=tq),
        grid=(n // tn, t // tm),
        in_specs=[pl.BlockSpec((tm, d), lambda j, i: (i, 0)),
                  pl.BlockSpec((tn, d), lambda j, i: (j + off, 0))],
        out_specs=out_spec,
        out_shape=jax.ShapeDtypeStruct(out_shape, out_dtype),
        scratch_shapes=[pltpu.VMEM(wb_shape, jnp.bfloat16)],
        compiler_params=_cparams(("parallel", "arbitrary")),
    )(a, wt)


def _proj_residual_norm_kernel(a_ref, w_ref, x_ref, nw_ref, xo_ref, ho_ref):
    x1 = x_ref[...] + jnp.dot(a_ref[...], w_ref[...], preferred_element_type=jnp.float32)
    xo_ref[...] = x1
    ms = jnp.mean(x1 * x1, axis=-1, keepdims=True)
    ho_ref[...] = (x1 * lax.rsqrt(ms + NORM_EPS) * nw_ref[...]).astype(ho_ref.dtype)


def _proj_residual_norm(a, w, x, nw, h_dtype, tm):
    t, k = a.shape
    d = w.shape[1]
    return pl.pallas_call(
        _proj_residual_norm_kernel,
        grid=(t // tm,),
        in_specs=[pl.BlockSpec((tm, k), lambda i: (i, 0)),
                  pl.BlockSpec((k, d), lambda i: (0, 0), pipeline_mode=pl.Buffered(1)),
                  pl.BlockSpec((tm, d), lambda i: (i, 0)),
                  pl.BlockSpec((1, d), lambda i: (0, 0))],
        out_specs=[pl.BlockSpec((tm, d), lambda i: (i, 0)),
                   pl.BlockSpec((tm, d), lambda i: (i, 0))],
        out_shape=[jax.ShapeDtypeStruct((t, d), jnp.float32),
                   jax.ShapeDtypeStruct((t, d), h_dtype)],
        compiler_params=_cparams(("parallel",)),
    )(a, w, x, nw.reshape(1, d))


def _dsa_kernel(q_ref, g_ref, k_ref, vt_ref, iq_ref, ikq_ref, ika_ref, bias_ref, o_ref,
                ikbd_ref, iqb_ref, madd_ref, hi_ref, lo_ref, s0_ref, s1_ref, m_ref, tied_ref, acc_ref,
                *, tq, topk, nq):
    qb = pl.program_id(1)
    h = pl.program_id(2)
    nck = qb + 1
    nb = tq // LANES
    base_d = nq * nb
    base_p = base_d + nb
    pairs = IDX_HEADS // 2
    f32 = jnp.float32
    bf16 = jnp.bfloat16
    i32 = jnp.int32
    hd = ATT_HEAD_DIM

    @pl.when((h == 0) & (qb == 0))
    def _():
        blk = ika_ref[...]
        lane = lax.broadcasted_iota(jnp.int32, blk.shape, 1)
        a = jnp.where(lane < IDX_HEAD_DIM, blk, 0.0)
        ikbd_ref[0] = a.astype(bf16)
        ikbd_ref[1] = pltpu.roll(a, IDX_HEAD_DIM, 1).astype(bf16)

    @pl.when(h == 0)
    def _():
        iqb_ref[...] = iq_ref[...].astype(bf16)
        iwt = ikq_ref[...].T[IDX_HEAD_DIM:IDX_HEAD_DIM + IDX_HEADS, :] * (
            IDX_HEADS ** -0.5 * IDX_HEAD_DIM ** -0.5)
        krow = lax.broadcasted_iota(jnp.int32, (tq, tq), 0)
        qcol = qb * tq + lax.broadcasted_iota(jnp.int32, (tq, tq), 1)

        def score_chunk(c, carry):
            start = pl.multiple_of(c * tq, tq)
            ka = ikbd_ref[0, pl.ds(start, tq), :]
            kb = ikbd_ref[1, pl.ds(start, tq), :]
            sc = jnp.zeros((tq, tq), f32)
            for j in range(pairs):
                rhs = iqb_ref[:, j * LANES:(j + 1) * LANES]
                d0 = lax.dot_general(ka, rhs, _NT, preferred_element_type=f32)
                d1 = lax.dot_general(kb, rhs, _NT, preferred_element_type=f32)
                sc = sc + jnp.maximum(d0, 0.0) * iwt[2 * j:2 * j + 1, :]
                sc = sc + jnp.maximum(d1, 0.0) * iwt[2 * j + 1:2 * j + 2, :]
            causal = c * tq + krow <= qcol
            bits = lax.bitcast_convert_type(sc, i32)
            key = bits ^ ((bits >> 31) & i32(0x7FFFFFFF))
            key = jnp.where(causal, key, i32(INT_MIN))
            hi = (key >> 16).astype(jnp.int16)
            for a in range(nb):
                madd_ref[c * nb + a] = lax.bitcast_convert_type(
                    key[a * LANES:(a + 1) * LANES, :], f32)
                hi_ref[c * nb + a] = hi[a * LANES:(a + 1) * LANES, :]
            return carry

        lax.fori_loop(0, nck, score_chunk, 0)

        def keys_of(c, a):
            return lax.bitcast_convert_type(madd_ref[c * nb + a], i32)

        srow = lax.broadcasted_iota(i32, (LANES, tq), 0)

        def count(pred):
            def count_chunk(c, part):
                for a in range(nb):
                    pos = c * tq + a * LANES + srow
                    w = jnp.where(pred(keys_of(c, a), pos), 1.0, 0.0)
                    part = part + jnp.sum(w.reshape(LANES // SUBLANES, SUBLANES, tq), axis=0)
                return part

            part = lax.fori_loop(0, nck, count_chunk, jnp.zeros((SUBLANES, tq), f32))
            return jnp.sum(part, axis=0, keepdims=True)

        def count_packed(ref, cand):
            one = jnp.ones((), ref.dtype)
            zero = jnp.zeros((), ref.dtype)

            def count_chunk(c, part):
                for a in range(nb):
                    w = jnp.where(ref[c * nb + a] >= cand, one, zero)
                    s = w[0:BF16_ROWS, :]
                    for r in range(1, LANES // BF16_ROWS):
                        s = s + w[r * BF16_ROWS:(r + 1) * BF16_ROWS, :]
                    part = part + s.astype(f32)
                return part

            part = lax.fori_loop(0, nck, count_chunk, jnp.zeros((BF16_ROWS, tq), f32))
            return jnp.sum(part, axis=0, keepdims=True)

        def bit_step(i, carry, hi_half):
            tau_u, cnt_tau = carry
            cand_u = tau_u | lax.shift_left(i32(1), 31 - i)
            if hi_half:
                cnt = count_packed(hi_ref, ((cand_u ^ i32(INT_MIN)) >> 16).astype(jnp.int16))
            else:
                cnt = count_packed(lo_ref, ((cand_u & i32(0xFFFF)) - LO_BIAS).astype(jnp.int16))
            ok = cnt >= topk
            return jnp.where(ok, cand_u, tau_u), jnp.where(ok, cnt, cnt_tau)

        carry = (jnp.zeros((1, tq), i32), jnp.zeros((1, tq), f32))
        carry = lax.fori_loop(0, 16, functools.partial(bit_step, hi_half=True), carry)

        tau_hi = (carry[0] ^ i32(INT_MIN)) >> 16

        def lo_chunk(c, carry_):
            for a in range(nb):
                key = keys_of(c, a)
                khi = key >> 16
                lo = jnp.where(khi > tau_hi, i32(LO_BIAS - 1),
                               jnp.where(khi == tau_hi, (key & i32(0xFFFF)) - LO_BIAS,
                                         i32(-LO_BIAS)))
                lo_ref[c * nb + a] = lo.astype(jnp.int16)
            return carry_

        lax.fori_loop(0, nck, lo_chunk, 0)
        tau_u, cnt_tau = lax.fori_loop(16, 32, functools.partial(bit_step, hi_half=False), carry)
        tau = tau_u ^ i32(INT_MIN)

        def write_mask(sel_fn):
            def mask_chunk(c, carry):
                for a in range(nb):
                    key = keys_of(c, a)
                    pos = c * tq + a * LANES + srow
                    sel = sel_fn(key, pos) & (key != i32(INT_MIN))
                    madd_ref[c * nb + a] = jnp.where(sel, 0.0, NEG_BIG).astype(f32)
                return carry

            lax.fori_loop(0, nck, mask_chunk, 0)

        tied_ref[...] = jnp.broadcast_to(jnp.where(cnt_tau > topk, 1.0, 0.0), tied_ref.shape)
        n_tied_rows = jnp.sum(tied_ref[...])

        @pl.when(n_tied_rows == 0.0)
        def _():
            write_mask(lambda key, pos: key >= tau)

        @pl.when(n_tied_rows > 0.0)
        def _():
            need = topk - count(lambda key, pos: key > tau)
            nbits = max(1, (nq * tq - 1).bit_length())

            def pos_step(i, v):
                cand = v | lax.shift_left(i32(1), nbits - 1 - i)
                below = count(lambda key, pos: (key == tau) & (pos < cand))
                return jnp.where(below < need, cand, v)

            v = lax.fori_loop(0, nbits, pos_step, jnp.zeros((1, tq), i32))
            write_mask(lambda key, pos: (key > tau) | ((key == tau) & (pos <= v)))

    def attend(hh):
        dtile = bias_ref[hh, 0]
        ptile = bias_ref[hh, 1]
        def biased(slab, tiles):
            cols = [slab[:, b * LANES:(b + 1) * LANES] for b in range(nb)]
            for b, tile in tiles:
                cols[b] = cols[b] + tile
            return jnp.concatenate(cols, axis=1)

        for a in range(nb):
            tiles = [(a, dtile)] + ([(a + 1, ptile)] if a + 1 < nb else [])
            madd_ref[base_d + a] = biased(madd_ref[qb * nb + a], tiles)
        madd_ref[base_p] = biased(madd_ref[jnp.maximum(qb - 1, 0) * nb + nb - 1], [(0, ptile)])

        m_ref[...] = jnp.full(m_ref.shape, NEG_BIG, f32)
        acc_ref[...] = jnp.zeros(acc_ref.shape, f32)
        q = q_ref[:, hh * hd:(hh + 1) * hd]

        s_bufs = (s0_ref, s1_ref)

        def qk(c, par):
            kc = k_ref[hh, pl.ds(pl.multiple_of(c * tq, tq), tq), :]
            s = lax.dot_general(kc, q, _NT, preferred_element_type=f32)
            is_diag = c == qb
            is_prev = c == qb - 1
            cmax = None
            for a in range(nb):
                sl = jnp.where(is_diag, base_d + a, c * nb + a)
                if a == nb - 1:
                    sl = jnp.where(is_prev, base_p, sl)
                sa = s[a * LANES:(a + 1) * LANES, :] + madd_ref[sl]
                s_bufs[par][a * LANES:(a + 1) * LANES, :] = sa
                sm = jnp.max(sa.reshape(LANES // SUBLANES, SUBLANES, tq), axis=0)
                cmax = sm if cmax is None else jnp.maximum(cmax, sm)
            return jnp.max(cmax, axis=0, keepdims=True)

        def softmax_pv(c, par, cmax):
            m_prev = m_ref[...]
            m_new = jnp.maximum(m_prev, cmax)
            m_ref[...] = m_new
            p = jnp.exp2(s_bufs[par][...] - m_new).astype(bf16)
            acc_ref[...] = jnp.exp2(m_prev - m_new) * acc_ref[...] + jnp.dot(
                vt_ref[0, hh, c], p, preferred_element_type=f32)

        def run(c, n, cm, feed_next):
            for j in range(n):
                cm_next = qk(c + j + 1, (j + 1) & 1) if (j + 1 < n or feed_next) else None
                softmax_pv(c + j, j & 1, cm)
                cm = cm_next
            return cm

        unroll = ATT_LOOP_CHUNKS
        nloops = (nck - 1) // unroll
        cm = lax.fori_loop(0, nloops, lambda i, cm: run(unroll * i, unroll, cm, True), qk(0, 0))
        c0 = unroll * nloops
        for rem in range(1, unroll + 1):
            pl.when(nck - c0 == rem)(functools.partial(run, c0, rem, cm, False))

        out_t = acc_ref[0:hd, :] * (1.0 / acc_ref[hd:hd + 1, :])
        o_ref[:, hh * hd:(hh + 1) * hd] = (
            out_t.T * _silu(g_ref[:, hh * hd:(hh + 1) * hd].astype(f32))).astype(o_ref.dtype)

    for hh in range(k_ref.shape[0]):
        attend(hh)


def _dsa_attention(q, g, k_hm, v_t, idx, bias_tiles, batch, seq, heads, tq):
    t = q.shape[0]
    nq = seq // tq
    nb = tq // LANES
    topk = min(TOPK_MAX, seq // 4)
    iq_w = IDX_HEADS * IDX_HEAD_DIM
    small_blk = iq_w // LANES
    hd = ATT_HEAD_DIM
    vrows = hd + BF16_ROWS
    hps = ATT_HEADS_PER_STEP if heads % ATT_HEADS_PER_STEP == 0 else 1
    kern = functools.partial(_dsa_kernel, tq=tq, topk=topk, nq=nq)
    return pl.pallas_call(
        kern,
        grid=(batch, nq, heads // hps),
        in_specs=[
            pl.BlockSpec((tq, hps * hd), lambda b, i, h: (b * nq + i, h)),
            pl.BlockSpec((tq, hps * hd), lambda b, i, h: (b * nq + i, h)),
            pl.BlockSpec((hps, seq, hd), lambda b, i, h: (h, b, 0)),
            pl.BlockSpec((1, hps, nq, vrows, tq), lambda b, i, h: (b, h, 0, 0, 0)),
            pl.BlockSpec((tq, iq_w), lambda b, i, h: (b * nq + i, 0)),
            pl.BlockSpec((tq, LANES), lambda b, i, h: (b * nq + i, small_blk)),
            pl.BlockSpec((seq, LANES), lambda b, i, h: (b, small_blk)),
            pl.BlockSpec((hps, 2, LANES, LANES), lambda b, i, h: (h, 0, 0, 0)),
        ],
        out_specs=pl.BlockSpec((tq, hps * hd), lambda b, i, h: (b * nq + i, h)),
        out_shape=jax.ShapeDtypeStruct((t, heads * hd), jnp.bfloat16),
        scratch_shapes=[
            pltpu.VMEM((2, seq, LANES), jnp.bfloat16),
            pltpu.VMEM((tq, iq_w), jnp.bfloat16),
            pltpu.VMEM((nq * nb + nb + 1, LANES, tq), jnp.float32),
            pltpu.VMEM((nq * nb, LANES, tq), jnp.int16),
            pltpu.VMEM((nq * nb, LANES, tq), jnp.int16),
            pltpu.VMEM((tq, tq), jnp.float32),
            pltpu.VMEM((tq, tq), jnp.float32),
            pltpu.VMEM((1, tq), jnp.float32),
            pltpu.VMEM((SUBLANES, tq), jnp.float32),
            pltpu.VMEM((vrows, tq), jnp.float32),
        ],
        compiler_params=_cparams(("arbitrary", "arbitrary", "arbitrary")),
    )(q, g, k_hm, v_t, idx, idx, idx, bias_tiles)


def _rel_bucket_of(n):
    max_exact = REL_BUCKETS // 2
    nf = jnp.maximum(n, 1).astype(jnp.float32)
    large = max_exact + (jnp.log(nf / max_exact) / math.log(REL_MAX_DIST / max_exact)
                         * (REL_BUCKETS - max_exact)).astype(jnp.int32)
    large = jnp.minimum(large, REL_BUCKETS - 1)
    return jnp.where(n < max_exact, n, large)


def _bias_tiles(rel_bias):
    assert REL_MAX_DIST <= LANES
    n = LANES
    heads = rel_bias.shape[1]
    dist = jnp.arange(2 * n, dtype=jnp.int32)
    bucket = jnp.where(dist >= REL_MAX_DIST, REL_BUCKETS - 1, _rel_bucket_of(dist))
    bv = (rel_bias[bucket] - rel_bias[REL_BUCKETS - 1][None, :]) * LOG2E
    rows = jnp.concatenate([bv.T, jnp.zeros((heads, n), bv.dtype)], axis=1)
    flat = jnp.broadcast_to(rows[:, None, :], (heads, n, 3 * n)).reshape(heads, 3 * n * n)
    toep = flat[:, :n * (3 * n - 1)].reshape(heads, n, 3 * n - 1)[:, :, :2 * n]
    return jnp.stack([toep[:, :, :n], toep[:, :, n:]], axis=1).astype(jnp.float32)


def _ssd_kernel(z_ref, x_ref, bc_ref, dt_ref, shift_ref, cw_ref, cbb_ref, dtb_ref, alog_ref,
                dexp_ref, nw_ref, rexp_ref, o_ref, xe_ref, taps_ref, state_ref, y_ref,
                *, inner, groups):
    f32 = jnp.float32
    bf16 = jnp.bfloat16
    L = SSM_CHUNK
    P = SSM_HEAD_DIM
    N = SSM_STATE
    heads = inner // P
    hpg = heads // groups
    gw = hpg * P
    gn = groups * N
    halo = BF16_ROWS
    ext = halo + L
    dot = functools.partial(jnp.dot, preferred_element_type=f32)

    @pl.when(pl.program_id(1) == 0)
    def _():
        xe_ref[0:halo, :] = jnp.zeros((halo, xe_ref.shape[1]), bf16)
        taps_ref[SSM_CONV * ext:SSM_CONV * ext + halo, :] = cbb_ref[...]
        state_ref[...] = jnp.zeros(state_ref.shape, f32)

    xe_ref[halo:ext, 0:inner] = x_ref[...]
    xe_ref[halo:ext, inner:inner + 2 * gn] = bc_ref[...]
    xe = xe_ref[...]
    for j in range(SSM_CONV):
        taps_ref[j * ext:(j + 1) * ext, :] = xe * cw_ref[j:j + 1, :]
    xe_ref[0:halo, :] = xe_ref[L:ext, :]
    shift = shift_ref[...]
    bcm = _silu(dot(shift, taps_ref[:, inner:inner + 2 * gn]))

    dtr = dt_ref[...] + dtb_ref[...]
    dt = jnp.maximum(dtr, 0.0) + jnp.log1p(jnp.exp(-jnp.abs(dtr)))
    adt = dt * (-jnp.exp(alog_ref[...]))
    ri = lax.broadcasted_iota(jnp.int32, (L, L), 0)
    ci = lax.broadcasted_iota(jnp.int32, (L, L), 1)
    tri = ri >= ci
    cs = _dot_exact_lhs(jnp.where(tri, 1.0, 0.0).astype(bf16), adt)
    cst = cs.T
    csl = cs[L - 1:L, :]
    dt_hi = dt.astype(bf16)
    dt_lo = (dt - dt_hi.astype(f32)).astype(bf16)
    ecs_b = jnp.exp(cs).astype(bf16)
    dec_b = jnp.exp(csl - cs).astype(bf16)
    cdec = jnp.broadcast_to(jnp.exp(csl), (SUBLANES, LANES))
    lane = lax.broadcasted_iota(jnp.int32, (L, LANES), 1)

    ssq = jnp.zeros((L, 1), f32)
    for g in range(groups):
        sl = slice(g * gw, (g + 1) * gw)
        rex = rexp_ref[:, sl]
        xs = _silu(dot(shift, taps_ref[:, sl]))
        bg = bcm[:, g * N:(g + 1) * N]
        cg = bcm[:, gn + g * N:gn + (g + 1) * N].astype(bf16)
        xdt = xs * (dot(dt_hi, rex) + dot(dt_lo, rex))
        xdt_b = xdt.astype(bf16)
        xdec_b = (xdt * dot(dec_b, rex)).astype(bf16)
        gmat = lax.dot_general(cg, bg.astype(bf16), _NT, preferred_element_type=f32)
        bgt = bg.T.astype(bf16)
        st_prev = state_ref[g]
        y = dot(cg, st_prev.astype(bf16)) * dot(ecs_b, rex) + xs * dexp_ref[:, sl]
        state_ref[g] = st_prev * _dot_exact_rhs(cdec, rex)[0:1, :] + dot(bgt, xdec_b)
        diag = []
        for pr in range(gw // LANES):
            xp = xdt_b[:, pr * LANES:(pr + 1) * LANES]
            lhs, rhs = [], []
            for sub in range(LANES // P):
                hh = (g * gw + pr * LANES) // P + sub
                seg = cs[:, hh:hh + 1] - cst[hh:hh + 1, :]
                lm = jnp.exp(jnp.where(tri, seg, -jnp.inf))
                lhs.append((gmat * lm).astype(bf16))
                rhs.append(jnp.where((lane >= sub * P) & (lane < (sub + 1) * P), xp,
                                     jnp.zeros_like(xp)))
            diag.append(dot(jnp.concatenate(lhs, axis=1), jnp.concatenate(rhs, axis=0)))
        yg = (y + jnp.concatenate(diag, axis=1)) * _silu(z_ref[:, sl].astype(f32))
        ssq = ssq + jnp.sum(yg * yg, axis=-1, keepdims=True)
        y_ref[:, sl] = yg

    scale = lax.rsqrt(ssq * (1.0 / inner) + NORM_EPS)
    o_ref[...] = (y_ref[...] * scale * nw_ref[...]).astype(o_ref.dtype)


def _ssd(zxbc, dt, conv_w, conv_b, dt_bias, a_log, d_skip, norm_w, batch, seq, inner, groups):
    t = zxbc.shape[0]
    L = SSM_CHUNK
    heads = inner // SSM_HEAD_DIM
    gn = groups * SSM_STATE
    conv_ch = inner + 2 * gn
    assert heads <= LANES and LANES % SSM_HEAD_DIM == 0 and inner % (2 * gn) == 0
    nc = seq // L
    pad_h = LANES - heads
    dtb = jnp.pad(dt_bias, (0, pad_h)).reshape(1, LANES)
    alog = jnp.pad(a_log, (0, pad_h)).reshape(1, LANES)
    dexp = jnp.repeat(d_skip, SSM_HEAD_DIM).reshape(1, inner)
    rexp = (jnp.arange(LANES, dtype=jnp.int32)[:, None]
            == (jnp.arange(inner, dtype=jnp.int32) // SSM_HEAD_DIM)[None, :]).astype(jnp.bfloat16)
    bc_blk = inner * 2 // (2 * gn)
    halo = BF16_ROWS
    ext = halo + L
    rows = jnp.arange(L, dtype=jnp.int32)[:, None]
    cols = jnp.arange(SSM_CONV * ext + halo, dtype=jnp.int32)[None, :]
    blk, pos = cols // ext, cols % ext
    shift = jnp.where(blk < SSM_CONV, pos == halo + rows - (SSM_CONV - 1 - blk),
                      pos < 2).astype(jnp.bfloat16)
    cb_hi = conv_b.astype(jnp.bfloat16)
    cb_lo = (conv_b - cb_hi.astype(jnp.float32)).astype(jnp.bfloat16)
    cbb = jnp.zeros((halo, conv_ch), jnp.bfloat16).at[0].set(cb_hi).at[1].set(cb_lo)
    kern = functools.partial(_ssd_kernel, inner=inner, groups=groups)
    const = lambda b, c: (0, 0)
    return pl.pallas_call(
        kern,
        grid=(batch, nc),
        in_specs=[
            pl.BlockSpec((L, inner), lambda b, c: (b * nc + c, 0)),
            pl.BlockSpec((L, inner), lambda b, c: (b * nc + c, 1)),
            pl.BlockSpec((L, 2 * gn), lambda b, c: (b * nc + c, bc_blk)),
            pl.BlockSpec((L, LANES), lambda b, c: (b * nc + c, 0)),
            pl.BlockSpec((L, SSM_CONV * ext + halo), const),
            pl.BlockSpec((SSM_CONV, conv_ch), const),
            pl.BlockSpec((halo, conv_ch), const),
            pl.BlockSpec((1, LANES), const),
            pl.BlockSpec((1, LANES), const),
            pl.BlockSpec((1, inner), const),
            pl.BlockSpec((1, inner), const),
            pl.BlockSpec((LANES, inner), const),
        ],
        out_specs=pl.BlockSpec((L, inner), lambda b, c: (b * nc + c, 0)),
        out_shape=jax.ShapeDtypeStruct((t, inner), jnp.bfloat16),
        scratch_shapes=[
            pltpu.VMEM((ext, conv_ch), jnp.bfloat16),
            pltpu.VMEM((SSM_CONV * ext + halo, conv_ch), jnp.bfloat16),
            pltpu.VMEM((groups, SSM_STATE, inner // groups), jnp.float32),
            pltpu.VMEM((L, inner), jnp.float32),
        ],
        compiler_params=_cparams(("arbitrary", "arbitrary")),
    )(zxbc, zxbc, zxbc, dt, shift, conv_w.astype(jnp.bfloat16), cbb, dtb, alog, dexp,
      norm_w.reshape(1, inner), rexp)


def _proj_residual_final_kernel(a_ref, w_ref, x_ref, nw_ref, o_ref):
    x2 = x_ref[...] + jnp.dot(a_ref[...], w_ref[...], preferred_element_type=jnp.float32)
    ms = jnp.mean(x2 * x2, axis=-1, keepdims=True)
    o_ref[...] = (x2 * lax.rsqrt(ms + NORM_EPS) * nw_ref[...]).astype(o_ref.dtype)


def _proj_residual_final(a, w, x, nw, tm):
    t, k = a.shape
    d = w.shape[1]
    return pl.pallas_call(
        _proj_residual_final_kernel,
        grid=(t // tm,),
        in_specs=[pl.BlockSpec((tm, k), lambda i: (i, 0)),
                  pl.BlockSpec((k, d), lambda i: (0, 0), pipeline_mode=pl.Buffered(1)),
                  pl.BlockSpec((tm, d), lambda i: (i, 0)),
                  pl.BlockSpec((1, d), lambda i: (0, 0))],
        out_specs=pl.BlockSpec((tm, d), lambda i: (i, 0)),
        out_shape=jax.ShapeDtypeStruct((t, d), jnp.float32),
        compiler_params=_cparams(("parallel",)),
    )(a, w, x, nw.reshape(1, d))


def _row_tile(t, want):
    while t % want:
        want //= 2
    return want


def kernel(x, norm_w, a_w_in, a_w_out, rel_bias, b_w_in, b_conv_w, b_conv_b, b_dt_bias, b_a_log,
           b_d, b_norm_w, b_w_out, final_norm_w):
    batch, seq, d = x.shape
    t = batch * seq
    bf16 = jnp.bfloat16
    assert norm_w.shape[0] == 2 and a_w_in.shape[0] == 1 and b_w_in.shape[0] == 1
    xf = x.reshape(t, d)

    att_w = a_w_out.shape[1]
    heads = att_w // ATT_HEAD_DIM
    iq_w = IDX_HEADS * IDX_HEAD_DIM
    wa = jnp.swapaxes(a_w_in, 1, 2).reshape(a_w_in.shape[2], d)
    n_small = wa.shape[0] - 4 * att_w - iq_w
    assert n_small == IDX_HEAD_DIM + IDX_HEADS <= LANES
    w_idx = jnp.pad(wa[4 * att_w:, :], ((0, LANES - n_small), (0, 0)))

    tm = _row_tile(seq, 2048)
    tn = _row_tile(att_w, 1024)
    tq = _row_tile(seq, ATT_Q_BLOCK)
    h0 = _rmsnorm(xf, norm_w[0], bf16, _row_tile(t, 512))
    q = _proj(h0, wa, 0, att_w, bf16, tm, tn, scale=ATT_HEAD_DIM ** -0.5 * LOG2E)
    k_hm = _proj(h0, wa, att_w, att_w, bf16, tm, tn, layout="heads")
    v_t = _proj(h0, wa, 2 * att_w, att_w, bf16, _row_tile(seq, 1024), tn, layout="t",
                batch=batch, tq=tq)
    g = _proj(h0, wa, 3 * att_w, att_w, bf16, tm, tn)
    idx = _proj(h0, w_idx, 0, iq_w + LANES, jnp.float32, _row_tile(seq, 1024), iq_w + LANES)
    att = _dsa_attention(q, g, k_hm, v_t, idx, _bias_tiles(rel_bias), batch, seq, heads, tq)
    x1, h1 = _proj_residual_norm(att, a_w_out[0].astype(bf16), xf, norm_w[1], bf16,
                                 _row_tile(t, 512))

    inner = b_w_out.shape[1]
    ssm_heads = b_dt_bias.shape[1]
    conv_ch = b_conv_w.shape[2]
    groups = (conv_ch - inner) // (2 * SSM_STATE)
    wb = jnp.swapaxes(b_w_in, 1, 2).reshape(b_w_in.shape[2], d)
    w_dt = jnp.pad(wb[inner + conv_ch:, :], ((0, LANES - ssm_heads), (0, 0)))
    zxbc = _proj(h1, wb, 0, inner + conv_ch, bf16, tm, _row_tile(inner + conv_ch, 1024))
    dt = _proj(h1, w_dt, 0, LANES, jnp.float32, tm, LANES)
    y = _ssd(zxbc, dt, b_conv_w[0], b_conv_b[0], b_dt_bias[0], b_a_log[0], b_d[0], b_norm_w[0],
             batch, seq, inner, groups)
    out = _proj_residual_final(y, b_w_out[0].astype(bf16), x1, final_norm_w, _row_tile(t, 512))
    return out.reshape(batch, seq, d)
```

```python
import functools
import math

import jax
import jax.numpy as jnp
from jax import lax
from jax.experimental import pallas as pl
from jax.experimental.pallas import tpu as pltpu

NORM_EPS = 1e-6

ATT_HEAD_DIM = 128
IDX_HEADS = 16
IDX_HEAD_DIM = 64
TOPK_MAX = 256
REL_BUCKETS = 32
REL_MAX_DIST = 128

SSM_HEAD_DIM = 64
SSM_STATE = 128
SSM_CONV = 4
SSM_CHUNK = 128

LANES = 128
SUBLANES = 8
BF16_ROWS = 16
VMEM_LIMIT_BYTES = 56 * 1024 * 1024

INT_MIN = -2 ** 31
NEG_INF_BITS = 0xFF800000 - 2 ** 32
HI16_MASK = 0xFFFF0000 - 2 ** 32
LO_BIAS = 2 ** 15
NEG_BIG = -1e30
LOG2E = math.log2(math.e)
ATT_Q_BLOCK = 512
ATT_LOOP_CHUNKS = 4
ATT_HEADS_PER_STEP = 4

_NT = (((1,), (1,)), ((), ()))


def _cparams(sem):
    return pltpu.CompilerParams(dimension_semantics=sem, vmem_limit_bytes=VMEM_LIMIT_BYTES)


def _silu(x):
    h = 0.5 * x
    return h + h * jnp.tanh(h)


def _split3(x):
    hi = x.astype(jnp.bfloat16)
    r1 = x - hi.astype(jnp.float32)
    mid = r1.astype(jnp.bfloat16)
    lo = (r1 - mid.astype(jnp.float32)).astype(jnp.bfloat16)
    return hi, mid, lo


def _dot_exact_lhs(a01, x):
    hi, mid, lo = _split3(x)
    f = functools.partial(jnp.dot, preferred_element_type=jnp.float32)
    return f(a01, hi) + f(a01, mid) + f(a01, lo)


def _dot_exact_rhs(x, b01):
    hi, mid, lo = _split3(x)
    f = functools.partial(jnp.dot, preferred_element_type=jnp.float32)
    return f(hi, b01) + f(mid, b01) + f(lo, b01)


def _rmsnorm_kernel(x_ref, nw_ref, o_ref):
    x = x_ref[...]
    ms = jnp.mean(x * x, axis=-1, keepdims=True)
    o_ref[...] = (x * lax.rsqrt(ms + NORM_EPS) * nw_ref[...]).astype(o_ref.dtype)


def _rmsnorm(x, nw, out_dtype, tm):
    t, d = x.shape
    return pl.pallas_call(
        _rmsnorm_kernel,
        grid=(t // tm,),
        in_specs=[pl.BlockSpec((tm, d), lambda i: (i, 0)),
                  pl.BlockSpec((1, d), lambda i: (0, 0))],
        out_specs=pl.BlockSpec((tm, d), lambda i: (i, 0)),
        out_shape=jax.ShapeDtypeStruct((t, d), out_dtype),
        compiler_params=_cparams(("parallel",)),
    )(x, nw.reshape(1, d))


def _proj_kernel(a_ref, w_ref, o_ref, wb_ref, *, layout, scale, tq):
    @pl.when(pl.program_id(1) == 0)
    def _():
        w = w_ref[...]
        if scale != 1.0:
            w = w * scale
        wb_ref[...] = (w if layout == "t" else w.T).astype(wb_ref.dtype)

    f32 = jnp.float32
    if layout == "t":
        acc = lax.dot_general(wb_ref[...], a_ref[...], _NT, preferred_element_type=f32)
        hd = ATT_HEAD_DIM
        for hh in range(o_ref.shape[1]):
            for cc in range(o_ref.shape[2]):
                o_ref[0, hh, cc, 0:hd, :] = acc[hh * hd:(hh + 1) * hd,
                                                cc * tq:(cc + 1) * tq].astype(o_ref.dtype)
    else:
        acc = jnp.dot(a_ref[...], wb_ref[...], preferred_element_type=f32)
        if layout == "heads":
            for j in range(o_ref.shape[0]):
                o_ref[j] = acc[:, j * LANES:(j + 1) * LANES].astype(o_ref.dtype)
        else:
            o_ref[...] = acc.astype(o_ref.dtype)


def _proj(a, wt, col_off, n, out_dtype, tm, tn, layout="rows", scale=1.0, batch=1, tq=LANES):
    t, d = a.shape
    assert col_off % tn == 0 and n % tn == 0 and t % tm == 0
    off = col_off // tn
    seq = t // batch
    mb = seq // tm
    if layout == "rows":
        out_spec = pl.BlockSpec((tm, tn), lambda j, i: (i, j))
        out_shape = (t, n)
    elif layout == "heads":
        out_spec = pl.BlockSpec((tn // LANES, tm, LANES), lambda j, i: (j, i, 0))
        out_shape = (n // LANES, t, LANES)
    else:
        rows = ATT_HEAD_DIM
        out_spec = pl.BlockSpec((1, tn // ATT_HEAD_DIM, tm // tq, rows, tq),
                                lambda j, i: (i // mb, j, i % mb, 0, 0))
        out_shape = (batch, n // ATT_HEAD_DIM, seq // tq, rows, tq)
    wb_shape = (tn, d) if layout == "t" else (d, tn)
    return pl.pallas_call(
        functools.partial(_proj_kernel, layout=layout, scale=scale, tq=tq),
        grid=(n // tn, t // tm),
        in_specs=[pl.BlockSpec((tm, d), lambda j, i: (i, 0)),
                  pl.BlockSpec((tn, d), lambda j, i: (j + off, 0))],
        out_specs=out_spec,
        out_shape=jax.ShapeDtypeStruct(out_shape, out_dtype),
        scratch_shapes=[pltpu.VMEM(wb_shape, jnp.bfloat16)],
        compiler_params=_cparams(("parallel", "arbitrary")),
    )(a, wt)


def _proj_residual_norm_kernel(a_ref, w_ref, x_ref, nw_ref, xo_ref, ho_ref):
    x1 = x_ref[...] + jnp.dot(a_ref[...], w_ref[...], preferred_element_type=jnp.float32)
    xo_ref[...] = x1
    ms = jnp.mean(x1 * x1, axis=-1, keepdims=True)
    ho_ref[...] = (x1 * lax.rsqrt(ms + NORM_EPS) * nw_ref[...]).astype(ho_ref.dtype)


def _proj_residual_norm(a, w, x, nw, h_dtype, tm):
    t, k = a.shape
    d = w.shape[1]
    return pl.pallas_call(
        _proj_residual_norm_kernel,
        grid=(t // tm,),
        in_specs=[pl.BlockSpec((tm, k), lambda i: (i, 0)),
                  pl.BlockSpec((k, d), lambda i: (0, 0), pipeline_mode=pl.Buffered(1)),
                  pl.BlockSpec((tm, d), lambda i: (i, 0)),
                  pl.BlockSpec((1, d), lambda i: (0, 0))],
        out_specs=[pl.BlockSpec((tm, d), lambda i: (i, 0)),
                   pl.BlockSpec((tm, d), lambda i: (i, 0))],
        out_shape=[jax.ShapeDtypeStruct((t, d), jnp.float32),
                   jax.ShapeDtypeStruct((t, d), h_dtype)],
        compiler_params=_cparams(("parallel",)),
    )(a, w, x, nw.reshape(1, d))


def _dsa_kernel(q_ref, g_ref, k_ref, vt_ref, iq_ref, ikq_ref, ika_ref, bias_ref, o_ref,
                ikbd_ref, iqb_ref, madd_ref, hi_ref, lo_ref, s0_ref, s1_ref, m_ref, tied_ref, acc_ref,
                *, tq, topk, nq):
    qb = pl.program_id(1)
    h = pl.program_id(2)
    nck = qb + 1
    nb = tq // LANES
    base_d = nq * nb
    base_p = base_d + nb
    pairs = IDX_HEADS // 2
    f32 = jnp.float32
    bf16 = jnp.bfloat16
    i32 = jnp.int32
    hd = ATT_HEAD_DIM

    @pl.when((h == 0) & (qb == 0))
    def _():
        blk = ika_ref[...]
        lane = lax.broadcasted_iota(jnp.int32, blk.shape, 1)
        a = jnp.where(lane < IDX_HEAD_DIM, blk, 0.0)
        ikbd_ref[0] = a.astype(bf16)
        ikbd_ref[1] = pltpu.roll(a, IDX_HEAD_DIM, 1).astype(bf16)

    @pl.when(h == 0)
    def _():
        iqb_ref[...] = iq_ref[...].astype(bf16)
        iwt = ikq_ref[...].T[IDX_HEAD_DIM:IDX_HEAD_DIM + IDX_HEADS, :] * (
            IDX_HEADS ** -0.5 * IDX_HEAD_DIM ** -0.5)
        krow = lax.broadcasted_iota(jnp.int32, (tq, tq), 0)
        qcol = qb * tq + lax.broadcasted_iota(jnp.int32, (tq, tq), 1)

        def score_chunk(c, carry):
            start = pl.multiple_of(c * tq, tq)
            ka = ikbd_ref[0, pl.ds(start, tq), :]
            kb = ikbd_ref[1, pl.ds(start, tq), :]
            sc = jnp.zeros((tq, tq), f32)
            for j in range(pairs):
                rhs = iqb_ref[:, j * LANES:(j + 1) * LANES]
                d0 = lax.dot_general(ka, rhs, _NT, preferred_element_type=f32)
                d1 = lax.dot_general(kb, rhs, _NT, preferred_element_type=f32)
                sc = sc + jnp.maximum(d0, 0.0) * iwt[2 * j:2 * j + 1, :]
                sc = sc + jnp.maximum(d1, 0.0) * iwt[2 * j + 1:2 * j + 2, :]
            causal = c * tq + krow <= qcol
            bits = lax.bitcast_convert_type(sc, i32)
            key = bits ^ ((bits >> 31) & i32(0x7FFFFFFF))
            key = jnp.where(causal, key, i32(INT_MIN))
            hi = lax.bitcast_convert_type(
                jnp.where(causal, bits, i32(NEG_INF_BITS)) & i32(HI16_MASK), f32).astype(bf16)
            for a in range(nb):
                madd_ref[c * nb + a] = lax.bitcast_convert_type(
                    key[a * LANES:(a + 1) * LANES, :], f32)
                hi_ref[c * nb + a] = hi[a * LANES:(a + 1) * LANES, :]
            return carry

        lax.fori_loop(0, nck, score_chunk, 0)

        def keys_of(c, a):
            return lax.bitcast_convert_type(madd_ref[c * nb + a], i32)

        srow = lax.broadcasted_iota(i32, (LANES, tq), 0)

        def count(pred):
            def count_chunk(c, part):
                for a in range(nb):
                    pos = c * tq + a * LANES + srow
                    w = jnp.where(pred(keys_of(c, a), pos), 1.0, 0.0)
                    part = part + jnp.sum(w.reshape(LANES // SUBLANES, SUBLANES, tq), axis=0)
                return part

            part = lax.fori_loop(0, nck, count_chunk, jnp.zeros((SUBLANES, tq), f32))
            return jnp.sum(part, axis=0, keepdims=True)

        def count_packed(ref, cand):
            one = jnp.ones((), ref.dtype)
            zero = jnp.zeros((), ref.dtype)

            def count_chunk(c, part):
                for a in range(nb):
                    w = jnp.where(ref[c * nb + a] >= cand, one, zero)
                    s = w[0:BF16_ROWS, :]
                    for r in range(1, LANES // BF16_ROWS):
                        s = s + w[r * BF16_ROWS:(r + 1) * BF16_ROWS, :]
                    part = part + s.astype(f32)
                return part

            part = lax.fori_loop(0, nck, count_chunk, jnp.zeros((BF16_ROWS, tq), f32))
            return jnp.sum(part, axis=0, keepdims=True)

        def bit_step(i, carry, hi_half):
            tau_u, cnt_tau = carry
            cand_u = tau_u | lax.shift_left(i32(1), 31 - i)
            if hi_half:
                cand_s = cand_u ^ i32(INT_MIN)
                fbits = cand_s ^ ((cand_s >> 31) & i32(0x7FFFFFFF))
                cnt = count_packed(
                    hi_ref, lax.bitcast_convert_type(fbits & i32(HI16_MASK), f32).astype(bf16))
            else:
                cnt = count_packed(lo_ref, ((cand_u & i32(0xFFFF)) - LO_BIAS).astype(jnp.int16))
            ok = cnt >= topk
            return jnp.where(ok, cand_u, tau_u), jnp.where(ok, cnt, cnt_tau)

        carry = (jnp.zeros((1, tq), i32), jnp.zeros((1, tq), f32))
        carry = lax.fori_loop(0, 16, functools.partial(bit_step, hi_half=True), carry)

        tau_hi = (carry[0] ^ i32(INT_MIN)) >> 16

        def lo_chunk(c, carry_):
            for a in range(nb):
                key = keys_of(c, a)
                khi = key >> 16
                lo = jnp.where(khi > tau_hi, i32(LO_BIAS - 1),
                               jnp.where(khi == tau_hi, (key & i32(0xFFFF)) - LO_BIAS,
                                         i32(-LO_BIAS)))
                lo_ref[c * nb + a] = lo.astype(jnp.int16)
            return carry_

        lax.fori_loop(0, nck, lo_chunk, 0)
        tau_u, cnt_tau = lax.fori_loop(16, 32, functools.partial(bit_step, hi_half=False), carry)
        tau = tau_u ^ i32(INT_MIN)

        def write_mask(sel_fn):
            def mask_chunk(c, carry):
                for a in range(nb):
                    key = keys_of(c, a)
                    pos = c * tq + a * LANES + srow
                    sel = sel_fn(key, pos) & (key != i32(INT_MIN))
                    madd_ref[c * nb + a] = jnp.where(sel, 0.0, NEG_BIG).astype(f32)
                return carry

            lax.fori_loop(0, nck, mask_chunk, 0)

        tied_ref[...] = jnp.broadcast_to(jnp.where(cnt_tau > topk, 1.0, 0.0), tied_ref.shape)
        n_tied_rows = jnp.sum(tied_ref[...])

        @pl.when(n_tied_rows == 0.0)
        def _():
            write_mask(lambda key, pos: key >= tau)

        @pl.when(n_tied_rows > 0.0)
        def _():
            need = topk - count(lambda key, pos: key > tau)
            nbits = max(1, (nq * tq - 1).bit_length())

            def pos_step(i, v):
                cand = v | lax.shift_left(i32(1), nbits - 1 - i)
                below = count(lambda key, pos: (key == tau) & (pos < cand))
                return jnp.where(below < need, cand, v)

            v = lax.fori_loop(0, nbits, pos_step, jnp.zeros((1, tq), i32))
            write_mask(lambda key, pos: (key > tau) | ((key == tau) & (pos <= v)))

    def attend(hh):
        dtile = bias_ref[hh, 0]
        ptile = bias_ref[hh, 1]
        def biased(slab, tiles):
            cols = [slab[:, b * LANES:(b + 1) * LANES] for b in range(nb)]
            for b, tile in tiles:
                cols[b] = cols[b] + tile
            return jnp.concatenate(cols, axis=1)

        for a in range(nb):
            tiles = [(a, dtile)] + ([(a + 1, ptile)] if a + 1 < nb else [])
            madd_ref[base_d + a] = biased(madd_ref[qb * nb + a], tiles)
        madd_ref[base_p] = biased(madd_ref[jnp.maximum(qb - 1, 0) * nb + nb - 1], [(0, ptile)])

        m_ref[...] = jnp.full(m_ref.shape, NEG_BIG, f32)
        acc_ref[...] = jnp.zeros(acc_ref.shape, f32)
        q = q_ref[:, hh * hd:(hh + 1) * hd]

        s_bufs = (s0_ref, s1_ref)

        def qk(c, par):
            kc = k_ref[hh, pl.ds(pl.multiple_of(c * tq, tq), tq), :]
            s = lax.dot_general(kc, q, _NT, preferred_element_type=f32)
            is_diag = c == qb
            is_prev = c == qb - 1
            cmax = None
            for a in range(nb):
                sl = jnp.where(is_diag, base_d + a, c * nb + a)
                if a == nb - 1:
                    sl = jnp.where(is_prev, base_p, sl)
                sa = s[a * LANES:(a + 1) * LANES, :] + madd_ref[sl]
                s_bufs[par][a * LANES:(a + 1) * LANES, :] = sa
                sm = jnp.max(sa.reshape(LANES // SUBLANES, SUBLANES, tq), axis=0)
                cmax = sm if cmax is None else jnp.maximum(cmax, sm)
            return jnp.max(cmax, axis=0, keepdims=True)

        def softmax_pv(c, par, cmax):
            m_prev = m_ref[...]
            m_new = jnp.maximum(m_prev, cmax)
            m_ref[...] = m_new
            pf = jnp.exp2(s_bufs[par][...] - m_new)
            alpha = jnp.exp2(m_prev - m_new)
            acc_ref[0:hd, :] = alpha * acc_ref[0:hd, :] + jnp.dot(
                vt_ref[0, hh, c], pf.astype(bf16), preferred_element_type=f32)
            acc_ref[hd:hd + SUBLANES, :] = alpha * acc_ref[hd:hd + SUBLANES, :] + jnp.sum(
                pf.reshape(tq // SUBLANES, SUBLANES, tq), axis=0)

        def run(c, n, cm, feed_next):
            for j in range(n):
                cm_next = qk(c + j + 1, (j + 1) & 1) if (j + 1 < n or feed_next) else None
                softmax_pv(c + j, j & 1, cm)
                cm = cm_next
            return cm

        unroll = ATT_LOOP_CHUNKS
        nloops = (nck - 1) // unroll
        cm = lax.fori_loop(0, nloops, lambda i, cm: run(unroll * i, unroll, cm, True), qk(0, 0))
        c0 = unroll * nloops
        for rem in range(1, unroll + 1):
            pl.when(nck - c0 == rem)(functools.partial(run, c0, rem, cm, False))

        row_sum = jnp.sum(acc_ref[hd:hd + SUBLANES, :], axis=0, keepdims=True)
        out_t = acc_ref[0:hd, :] * (1.0 / row_sum)
        o_ref[:, hh * hd:(hh + 1) * hd] = (
            out_t.T * _silu(g_ref[:, hh * hd:(hh + 1) * hd].astype(f32))).astype(o_ref.dtype)

    for hh in range(k_ref.shape[0]):
        attend(hh)


def _dsa_attention(q, g, k_hm, v_t, idx, bias_tiles, batch, seq, heads, tq):
    t = q.shape[0]
    nq = seq // tq
    nb = tq // LANES
    topk = min(TOPK_MAX, seq // 4)
    iq_w = IDX_HEADS * IDX_HEAD_DIM
    small_blk = iq_w // LANES
    hd = ATT_HEAD_DIM
    vrows = hd
    hps = ATT_HEADS_PER_STEP if heads % ATT_HEADS_PER_STEP == 0 else 1
    kern = functools.partial(_dsa_kernel, tq=tq, topk=topk, nq=nq)
    return pl.pallas_call(
        kern,
        grid=(batch, nq, heads // hps),
        in_specs=[
            pl.BlockSpec((tq, hps * hd), lambda b, i, h: (b * nq + i, h)),
            pl.BlockSpec((tq, hps * hd), lambda b, i, h: (b * nq + i, h)),
            pl.BlockSpec((hps, seq, hd), lambda b, i, h: (h, b, 0)),
            pl.BlockSpec((1, hps, nq, vrows, tq), lambda b, i, h: (b, h, 0, 0, 0)),
            pl.BlockSpec((tq, iq_w), lambda b, i, h: (b * nq + i, 0)),
            pl.BlockSpec((tq, LANES), lambda b, i, h: (b * nq + i, small_blk)),
            pl.BlockSpec((seq, LANES), lambda b, i, h: (b, small_blk)),
            pl.BlockSpec((hps, 2, LANES, LANES), lambda b, i, h: (h, 0, 0, 0)),
        ],
        out_specs=pl.BlockSpec((tq, hps * hd), lambda b, i, h: (b * nq + i, h)),
        out_shape=jax.ShapeDtypeStruct((t, heads * hd), jnp.bfloat16),
        scratch_shapes=[
            pltpu.VMEM((2, seq, LANES), jnp.bfloat16),
            pltpu.VMEM((tq, iq_w), jnp.bfloat16),
            pltpu.VMEM((nq * nb + nb + 1, LANES, tq), jnp.float32),
            pltpu.VMEM((nq * nb, LANES, tq), jnp.bfloat16),
            pltpu.VMEM((nq * nb, LANES, tq), jnp.int16),
            pltpu.VMEM((tq, tq), jnp.float32),
            pltpu.VMEM((tq, tq), jnp.float32),
            pltpu.VMEM((1, tq), jnp.float32),
            pltpu.VMEM((SUBLANES, tq), jnp.float32),
            pltpu.VMEM((hd + SUBLANES, tq), jnp.float32),
        ],
        compiler_params=_cparams(("arbitrary", "arbitrary", "arbitrary")),
    )(q, g, k_hm, v_t, idx, idx, idx, bias_tiles)


def _rel_bucket_of(n):
    max_exact = REL_BUCKETS // 2
    nf = jnp.maximum(n, 1).astype(jnp.float32)
    large = max_exact + (jnp.log(nf / max_exact) / math.log(REL_MAX_DIST / max_exact)
                         * (REL_BUCKETS - max_exact)).astype(jnp.int32)
    large = jnp.minimum(large, REL_BUCKETS - 1)
    return jnp.where(n < max_exact, n, large)


def _bias_tiles(rel_bias):
    assert REL_MAX_DIST <= LANES
    n = LANES
    heads = rel_bias.shape[1]
    dist = jnp.arange(2 * n, dtype=jnp.int32)
    bucket = jnp.where(dist >= REL_MAX_DIST, REL_BUCKETS - 1, _rel_bucket_of(dist))
    bv = (rel_bias[bucket] - rel_bias[REL_BUCKETS - 1][None, :]) * LOG2E
    rows = jnp.concatenate([bv.T, jnp.zeros((heads, n), bv.dtype)], axis=1)
    flat = jnp.broadcast_to(rows[:, None, :], (heads, n, 3 * n)).reshape(heads, 3 * n * n)
    toep = flat[:, :n * (3 * n - 1)].reshape(heads, n, 3 * n - 1)[:, :, :2 * n]
    return jnp.stack([toep[:, :, :n], toep[:, :, n:]], axis=1).astype(jnp.float32)


def _ssd_kernel(z_ref, x_ref, bc_ref, dt_ref, shift_ref, cw_ref, cbb_ref, dtb_ref, alog_ref,
                dexp_ref, nw_ref, rexp_ref, o_ref, xe_ref, taps_ref, state_ref, y_ref,
                *, inner, groups):
    f32 = jnp.float32
    bf16 = jnp.bfloat16
    L = SSM_CHUNK
    P = SSM_HEAD_DIM
    N = SSM_STATE
    heads = inner // P
    hpg = heads // groups
    gw = hpg * P
    gn = groups * N
    halo = BF16_ROWS
    ext = halo + L
    dot = functools.partial(jnp.dot, preferred_element_type=f32)

    @pl.when(pl.program_id(1) == 0)
    def _():
        xe_ref[0:halo, :] = jnp.zeros((halo, xe_ref.shape[1]), bf16)
        taps_ref[SSM_CONV * ext:SSM_CONV * ext + halo, :] = cbb_ref[...]
        state_ref[...] = jnp.zeros(state_ref.shape, f32)

    xe_ref[halo:ext, 0:inner] = x_ref[...]
    xe_ref[halo:ext, inner:inner + 2 * gn] = bc_ref[...]
    xe = xe_ref[...]
    for j in range(SSM_CONV):
        taps_ref[j * ext:(j + 1) * ext, :] = xe * cw_ref[j:j + 1, :]
    xe_ref[0:halo, :] = xe_ref[L:ext, :]
    shift = shift_ref[...]
    bcm = _silu(dot(shift, taps_ref[:, inner:inner + 2 * gn]))

    dtr = dt_ref[...] + dtb_ref[...]
    dt = jnp.maximum(dtr, 0.0) + jnp.log1p(jnp.exp(-jnp.abs(dtr)))
    adt = dt * (-jnp.exp(alog_ref[...]))
    ri = lax.broadcasted_iota(jnp.int32, (L, L), 0)
    ci = lax.broadcasted_iota(jnp.int32, (L, L), 1)
    tri = ri >= ci
    cs = _dot_exact_lhs(jnp.where(tri, 1.0, 0.0).astype(bf16), adt)
    cst = cs.T
    csl = cs[L - 1:L, :]
    dt_hi = dt.astype(bf16)
    dt_lo = (dt - dt_hi.astype(f32)).astype(bf16)
    ecs_b = jnp.exp(cs).astype(bf16)
    dec_b = jnp.exp(csl - cs).astype(bf16)
    cdec = jnp.broadcast_to(jnp.exp(csl), (SUBLANES, LANES))
    lane = lax.broadcasted_iota(jnp.int32, (L, LANES), 1)

    ssq = jnp.zeros((L, 1), f32)
    for g in range(groups):
        sl = slice(g * gw, (g + 1) * gw)
        rex = rexp_ref[:, sl]
        xs = _silu(dot(shift, taps_ref[:, sl]))
        bg = bcm[:, g * N:(g + 1) * N]
        cg = bcm[:, gn + g * N:gn + (g + 1) * N].astype(bf16)
        xdt = xs * (dot(dt_hi, rex) + dot(dt_lo, rex))
        xdt_b = xdt.astype(bf16)
        xdec_b = (xdt * dot(dec_b, rex)).astype(bf16)
        gmat = lax.dot_general(cg, bg.astype(bf16), _NT, preferred_element_type=f32)
        bgt = bg.T.astype(bf16)
        st_prev = state_ref[g]
        y = dot(cg, st_prev.astype(bf16)) * dot(ecs_b, rex) + xs * dexp_ref[:, sl]
        state_ref[g] = st_prev * _dot_exact_rhs(cdec, rex)[0:1, :] + dot(bgt, xdec_b)
        diag = []
        for pr in range(gw // LANES):
            xp = xdt_b[:, pr * LANES:(pr + 1) * LANES]
            lhs, rhs = [], []
            for sub in range(LANES // P):
                hh = (g * gw + pr * LANES) // P + sub
                seg = cs[:, hh:hh + 1] - cst[hh:hh + 1, :]
                lm = jnp.exp(jnp.where(tri, seg, -jnp.inf))
                lhs.append((gmat * lm).astype(bf16))
                rhs.append(jnp.where((lane >= sub * P) & (lane < (sub + 1) * P), xp,
                                     jnp.zeros_like(xp)))
            diag.append(dot(jnp.concatenate(lhs, axis=1), jnp.concatenate(rhs, axis=0)))
        yg = (y + jnp.concatenate(diag, axis=1)) * _silu(z_ref[:, sl].astype(f32))
        ssq = ssq + jnp.sum(yg * yg, axis=-1, keepdims=True)
        y_ref[:, sl] = yg

    scale = lax.rsqrt(ssq * (1.0 / inner) + NORM_EPS)
    o_ref[...] = (y_ref[...] * scale * nw_ref[...]).astype(o_ref.dtype)


def _ssd(zxbc, dt, conv_w, conv_b, dt_bias, a_log, d_skip, norm_w, batch, seq, inner, groups):
    t = zxbc.shape[0]
    L = SSM_CHUNK
    heads = inner // SSM_HEAD_DIM
    gn = groups * SSM_STATE
    conv_ch = inner + 2 * gn
    assert heads <= LANES and LANES % SSM_HEAD_DIM == 0 and inner % (2 * gn) == 0
    nc = seq // L
    pad_h = LANES - heads
    dtb = jnp.pad(dt_bias, (0, pad_h)).reshape(1, LANES)
    alog = jnp.pad(a_log, (0, pad_h)).reshape(1, LANES)
    dexp = jnp.repeat(d_skip, SSM_HEAD_DIM).reshape(1, inner)
    rexp = (jnp.arange(LANES, dtype=jnp.int32)[:, None]
            == (jnp.arange(inner, dtype=jnp.int32) // SSM_HEAD_DIM)[None, :]).astype(jnp.bfloat16)
    bc_blk = inner * 2 // (2 * gn)
    halo = BF16_ROWS
    ext = halo + L
    rows = jnp.arange(L, dtype=jnp.int32)[:, None]
    cols = jnp.arange(SSM_CONV * ext + halo, dtype=jnp.int32)[None, :]
    blk, pos = cols // ext, cols % ext
    shift = jnp.where(blk < SSM_CONV, pos == halo + rows - (SSM_CONV - 1 - blk),
                      pos < 2).astype(jnp.bfloat16)
    cb_hi = conv_b.astype(jnp.bfloat16)
    cb_lo = (conv_b - cb_hi.astype(jnp.float32)).astype(jnp.bfloat16)
    cbb = jnp.zeros((halo, conv_ch), jnp.bfloat16).at[0].set(cb_hi).at[1].set(cb_lo)
    kern = functools.partial(_ssd_kernel, inner=inner, groups=groups)
    const = lambda b, c: (0, 0)
    return pl.pallas_call(
        kern,
        grid=(batch, nc),
        in_specs=[
            pl.BlockSpec((L, inner), lambda b, c: (b * nc + c, 0)),
            pl.BlockSpec((L, inner), lambda b, c: (b * nc + c, 1)),
            pl.BlockSpec((L, 2 * gn), lambda b, c: (b * nc + c, bc_blk)),
            pl.BlockSpec((L, LANES), lambda b, c: (b * nc + c, 0)),
            pl.BlockSpec((L, SSM_CONV * ext + halo), const),
            pl.BlockSpec((SSM_CONV, conv_ch), const),
            pl.BlockSpec((halo, conv_ch), const),
            pl.BlockSpec((1, LANES), const),
            pl.BlockSpec((1, LANES), const),
            pl.BlockSpec((1, inner), const),
            pl.BlockSpec((1, inner), const),
            pl.BlockSpec((LANES, inner), const),
        ],
        out_specs=pl.BlockSpec((L, inner), lambda b, c: (b * nc + c, 0)),
        out_shape=jax.ShapeDtypeStruct((t, inner), jnp.bfloat16),
        scratch_shapes=[
            pltpu.VMEM((ext, conv_ch), jnp.bfloat16),
            pltpu.VMEM((SSM_CONV * ext + halo, conv_ch), jnp.bfloat16),
            pltpu.VMEM((groups, SSM_STATE, inner // groups), jnp.float32),
            pltpu.VMEM((L, inner), jnp.float32),
        ],
        compiler_params=_cparams(("arbitrary", "arbitrary")),
    )(zxbc, zxbc, zxbc, dt, shift, conv_w.astype(jnp.bfloat16), cbb, dtb, alog, dexp,
      norm_w.reshape(1, inner), rexp)


def _proj_residual_final_kernel(a_ref, w_ref, x_ref, nw_ref, o_ref):
    x2 = x_ref[...] + jnp.dot(a_ref[...], w_ref[...], preferred_element_type=jnp.float32)
    ms = jnp.mean(x2 * x2, axis=-1, keepdims=True)
    o_ref[...] = (x2 * lax.rsqrt(ms + NORM_EPS) * nw_ref[...]).astype(o_ref.dtype)


def _proj_residual_final(a, w, x, nw, tm):
    t, k = a.shape
    d = w.shape[1]
    return pl.pallas_call(
        _proj_residual_final_kernel,
        grid=(t // tm,),
        in_specs=[pl.BlockSpec((tm, k), lambda i: (i, 0)),
                  pl.BlockSpec((k, d), lambda i: (0, 0), pipeline_mode=pl.Buffered(1)),
                  pl.BlockSpec((tm, d), lambda i: (i, 0)),
                  pl.BlockSpec((1, d), lambda i: (0, 0))],
        out_specs=pl.BlockSpec((tm, d), lambda i: (i, 0)),
        out_shape=jax.ShapeDtypeStruct((t, d), jnp.float32),
        compiler_params=_cparams(("parallel",)),
    )(a, w, x, nw.reshape(1, d))


def _row_tile(t, want):
    while t % want:
        want //= 2
    return want


def kernel(x, norm_w, a_w_in, a_w_out, rel_bias, b_w_in, b_conv_w, b_conv_b, b_dt_bias, b_a_log,
           b_d, b_norm_w, b_w_out, final_norm_w):
    batch, seq, d = x.shape
    t = batch * seq
    bf16 = jnp.bfloat16
    assert norm_w.shape[0] == 2 and a_w_in.shape[0] == 1 and b_w_in.shape[0] == 1
    xf = x.reshape(t, d)

    att_w = a_w_out.shape[1]
    heads = att_w // ATT_HEAD_DIM
    iq_w = IDX_HEADS * IDX_HEAD_DIM
    wa = jnp.swapaxes(a_w_in, 1, 2).reshape(a_w_in.shape[2], d)
    n_small = wa.shape[0] - 4 * att_w - iq_w
    assert n_small == IDX_HEAD_DIM + IDX_HEADS <= LANES
    w_idx = jnp.pad(wa[4 * att_w:, :], ((0, LANES - n_small), (0, 0)))

    tm = _row_tile(seq, 2048)
    tn = _row_tile(att_w, 1024)
    tq = _row_tile(seq, ATT_Q_BLOCK)
    h0 = _rmsnorm(xf, norm_w[0], bf16, _row_tile(t, 512))
    q = _proj(h0, wa, 0, att_w, bf16, tm, tn, scale=ATT_HEAD_DIM ** -0.5 * LOG2E)
    k_hm = _proj(h0, wa, att_w, att_w, bf16, tm, tn, layout="heads")
    v_t = _proj(h0, wa, 2 * att_w, att_w, bf16, _row_tile(seq, 1024), tn, layout="t",
                batch=batch, tq=tq)
    g = _proj(h0, wa, 3 * att_w, att_w, bf16, tm, tn)
    idx = _proj(h0, w_idx, 0, iq_w + LANES, jnp.float32, _row_tile(seq, 1024), iq_w + LANES)
    att = _dsa_attention(q, g, k_hm, v_t, idx, _bias_tiles(rel_bias), batch, seq, heads, tq)
    x1, h1 = _proj_residual_norm(att, a_w_out[0].astype(bf16), xf, norm_w[1], bf16,
                                 _row_tile(t, 512))

    inner = b_w_out.shape[1]
    ssm_heads = b_dt_bias.shape[1]
    conv_ch = b_conv_w.shape[2]
    groups = (conv_ch - inner) // (2 * SSM_STATE)
    wb = jnp.swapaxes(b_w_in, 1, 2).reshape(b_w_in.shape[2], d)
    w_dt = jnp.pad(wb[inner + conv_ch:, :], ((0, LANES - ssm_heads), (0, 0)))
    zxbc = _proj(h1, wb, 0, inner + conv_ch, bf16, tm, _row_tile(inner + conv_ch, 1024))
    dt = _proj(h1, w_dt, 0, LANES, jnp.float32, tm, LANES)
    y = _ssd(zxbc, dt, b_conv_w[0], b_conv_b[0], b_dt_bias[0], b_a_log[0], b_d[0], b_norm_w[0],
             batch, seq, inner, groups)
    out = _proj_residual_final(y, b_w_out[0].astype(bf16), x1, final_norm_w, _row_tile(t, 512))
    return out.reshape(batch, seq, d)
```
